```python
import math
import jax, jax.numpy as jnp
from jax import lax
import numpy as np

D_MODEL = 1024
BATCH = 1
SEQ = 16384
DEPTH = 2

HEAD_DIM = 64
ROT_DIM = HEAD_DIM // 4
ROPE_THETA = 500000.0
BLK = 128
DSW_HEADS = 8
DSW_WIDTH = DSW_HEADS * HEAD_DIM
DSW_PATTERNS = ((128, 1), (512, 4), (2048, 16))
DIFF_HEADS = 4
DIFF_QK_WIDTH = DIFF_HEADS * 2 * HEAD_DIM
DIFF_V_DIM = 2 * HEAD_DIM
DIFF_WIDTH = DIFF_HEADS * DIFF_V_DIM
ATT_SPLITS = (DSW_WIDTH, DSW_WIDTH, DSW_WIDTH, DSW_WIDTH,
              DIFF_QK_WIDTH, DIFF_QK_WIDTH, DIFF_WIDTH, DIFF_WIDTH)
ATT_IN = sum(ATT_SPLITS)
ATT_OUT_IN = DSW_WIDTH + DIFF_WIDTH
SGU_WIDTH = 2 * D_MODEL
SGU_GROUPS = 8
SGU_CHUNK = 128
SGU_IN = 3 * SGU_WIDTH
N_EVEN = (DEPTH + 1) // 2
N_ODD = DEPTH // 2
RMS_EPS = 1e-6
LN_EPS = 1e-5
NEG_INF = -1e30

kernel_name = "hybrid_dilated_diff_sgu_trunk"


def rms_norm(x, g, eps=RMS_EPS):
    xf = x.astype(jnp.float32)
    y = xf * lax.rsqrt(jnp.mean(xf * xf, axis=-1, keepdims=True) + eps)
    return (y * g.astype(jnp.float32)).astype(x.dtype)


def layer_norm(x, g, b, eps=LN_EPS):
    xf = x.astype(jnp.float32)
    mu = jnp.mean(xf, axis=-1, keepdims=True)
    xc = xf - mu
    y = xc * lax.rsqrt(jnp.mean(xc * xc, axis=-1, keepdims=True) + eps)
    return (y * g.astype(jnp.float32) + b.astype(jnp.float32)).astype(x.dtype)


def rope_tables(seq_len):
    pos = jnp.arange(seq_len, dtype=jnp.float32)
    inv = 1.0 / (ROPE_THETA ** (jnp.arange(0, ROT_DIM, 2, dtype=jnp.float32) / ROT_DIM))
    ang = pos[:, None] * inv[None, :]
    return jnp.cos(ang), jnp.sin(ang)


def partial_rope(x, cos, sin):
    half = ROT_DIM // 2
    shp = (cos.shape[0],) + (1,) * (x.ndim - 3) + (half,)
    c, s = cos.reshape(shp), sin.reshape(shp)
    xf = x.astype(jnp.float32)
    x1, x2 = xf[..., :half], xf[..., half:ROT_DIM]
    out = jnp.concatenate([x1 * c - x2 * s, x2 * c + x1 * s, xf[..., ROT_DIM:]], axis=-1)
    return out.astype(x.dtype)


def dilated_branch(q, k, v, dil, wsub):
    b, s, h, d = q.shape
    unit = dil * BLK
    s_pad = -(-s // unit) * unit
    n_sub = s_pad // dil
    nb = n_sub // BLK

    def to_sub(t):
        t = jnp.pad(t, ((0, 0), (0, s_pad - s), (0, 0), (0, 0)))
        return t.reshape(b, n_sub, dil, h, d).transpose(0, 2, 3, 1, 4).reshape(b, dil, h, nb, BLK, d)

    def with_prev(t):
        prev = jnp.pad(t[:, :, :, :-1], ((0, 0), (0, 0), (0, 0), (1, 0), (0, 0), (0, 0)))
        return jnp.concatenate([prev, t], axis=4)

    qs = to_sub(q)
    kk = with_prev(to_sub(k))
    vv = with_prev(to_sub(v))
    sc = jnp.einsum('brhnqd,brhnkd->brhnqk', qs, kk,
                    preferred_element_type=jnp.float32) * (d ** -0.5)
    qi = jnp.arange(BLK)[:, None] + BLK
    kj = jnp.arange(2 * BLK)[None, :]
    dist = qi - kj
    band = (dist >= 0) & (dist <= wsub)
    has_prev = (jnp.arange(nb)[:, None, None] > 0) | (kj[None] >= BLK)
    mask = band[None] & has_prev
    sc = jnp.where(mask, sc, NEG_INF)
    m = jnp.max(sc, axis=-1, keepdims=True)
    p = jnp.exp(sc - m)
    l = jnp.sum(p, axis=-1, keepdims=True)
    o = jnp.einsum('brhnqk,brhnkd->brhnqd', p.astype(v.dtype), vv,
                   preferred_element_type=jnp.float32) / l
    lse = (m + jnp.log(l))[..., 0]
    o = o.reshape(b, dil, h, n_sub, d).transpose(0, 3, 1, 2, 4).reshape(b, s_pad, h, d)[:, :s]
    lse = lse.reshape(b, dil, h, n_sub).transpose(0, 3, 1, 2).reshape(b, s_pad, h)[:, :s]
    return o, lse


def dsw_attention(q, k, v):
    outs, lses = [], []
    for window, dil in DSW_PATTERNS:
        o, lse = dilated_branch(q, k, v, dil, window // dil)
        outs.append(o)
        lses.append(lse)
    wts = jax.nn.softmax(jnp.stack(lses, axis=0), axis=0)
    return jnp.sum(wts[..., None] * jnp.stack(outs, axis=0), axis=0)


def diff_attention(q, k, v, lam):
    b, s, h, _, d = q.shape
    nblk = s // BLK
    qb = q.reshape(b, nblk, BLK, h, 2, d).transpose(1, 0, 4, 3, 2, 5)
    kt = k.transpose(0, 3, 2, 1, 4)
    vt = v.transpose(0, 2, 1, 3)
    kpos = jnp.arange(s)
    scale = d ** -0.5

    def one_block(args):
        qblk, i = args
        sc = jnp.einsum('bchqd,bchkd->bchqk', qblk, kt,
                        preferred_element_type=jnp.float32) * scale
        qpos = i * BLK + jnp.arange(BLK)
        sc = jnp.where(kpos[None, :] <= qpos[:, None], sc, NEG_INF)
        p = jax.nn.softmax(sc, axis=-1)
        a = p[:, 0] - lam * p[:, 1]
        return jnp.einsum('bhqk,bhkd->bhqd', a.astype(vt.dtype), vt,
                          preferred_element_type=jnp.float32)

    o = lax.map(one_block, (qb, jnp.arange(nblk)))
    return o.transpose(1, 0, 3, 2, 4).reshape(b, s, h, 2 * d)


def hybrid_attention_layer(x, norm_g, w_in, dsw_qg, dsw_kg, diff_qg, diff_kg,
                           lq1, lk1, lq2, lk2, subln_g, w_out, cos, sin, lam_init):
    b, s, _ = x.shape
    h = rms_norm(x, norm_g)
    proj = jnp.einsum('bsd,de->bse', h, w_in)
    offs = [int(o) for o in np.cumsum(ATT_SPLITS)[:-1]]
    qa, ka, va, ga, qd, kd, vd, gd = jnp.split(proj, offs, axis=-1)
    qa = partial_rope(rms_norm(qa.reshape(b, s, DSW_HEADS, HEAD_DIM), dsw_qg), cos, sin)
    ka = partial_rope(rms_norm(ka.reshape(b, s, DSW_HEADS, HEAD_DIM), dsw_kg), cos, sin)
    va = va.reshape(b, s, DSW_HEADS, HEAD_DIM)
    oa = dsw_attention(qa, ka, va).reshape(b, s, DSW_WIDTH)
    qd = partial_rope(rms_norm(qd.reshape(b, s, DIFF_HEADS, 2, HEAD_DIM), diff_qg), cos, sin)
    kd = partial_rope(rms_norm(kd.reshape(b, s, DIFF_HEADS, 2, HEAD_DIM), diff_kg), cos, sin)
    vd = vd.reshape(b, s, DIFF_HEADS, DIFF_V_DIM)
    f32 = jnp.float32
    lam = (jnp.exp(jnp.sum(lq1.astype(f32) * lk1.astype(f32)))
           - jnp.exp(jnp.sum(lq2.astype(f32) * lk2.astype(f32))) + lam_init)
    od = diff_attention(qd, kd, vd, lam)
    od = (rms_norm(od, subln_g) * (1.0 - lam_init)).reshape(b, s, DIFF_WIDTH)
    y = jnp.concatenate([oa.astype(x.dtype) * jax.nn.silu(ga),
                         od.astype(x.dtype) * jax.nn.silu(gd)], axis=-1)
    return x + jnp.einsum('bse,ed->bsd', y, w_out)


def sgu_layer(x, norm_g, w_in, ln_g, ln_b, w_s, b_s, w_out):
    b, s, _ = x.shape
    h = rms_norm(x, norm_g)
    proj = jnp.einsum('bsd,de->bse', h, w_in)
    u, v, g = jnp.split(proj, [SGU_WIDTH, 2 * SGU_WIDTH], axis=-1)
    u = jax.nn.gelu(u, approximate=False)
    v = layer_norm(jax.nn.gelu(v, approximate=False), ln_g, ln_b)
    nc = s // SGU_CHUNK
    vg = v.reshape(b, nc, SGU_CHUNK, SGU_GROUPS, SGU_WIDTH // SGU_GROUPS)
    causal = jnp.tril(jnp.ones((SGU_CHUNK, SGU_CHUNK), dtype=bool))
    ws = jnp.where(causal[None], w_s, 0)
    sp = jnp.einsum('gts,bnsgc->bntgc', ws, vg) + b_s.T[None, None, :, :, None]
    sp = sp.reshape(b, s, SGU_WIDTH)
    y = u * sp * jax.nn.silu(g)
    return x + jnp.einsum('bse,ed->bsd', y, w_out)


def setup_inputs(seed: int = 0) -> dict:
    key = jax.random.key(seed)
    ks = jax.random.split(key, 20)
    nrm = jax.random.normal
    f32 = jnp.float32
    D = D_MODEL
    return {
        "x": nrm(ks[0], (BATCH, SEQ, D), f32),
        "att_norm": 1.0 + 0.02 * nrm(ks[1], (N_EVEN, D), f32),
        "att_w_in": nrm(ks[2], (N_EVEN, D, ATT_IN), f32) * D ** -0.5,
        "dsw_q_norm": 1.0 + 0.02 * nrm(ks[3], (N_EVEN, HEAD_DIM), f32),
        "dsw_k_norm": 1.0 + 0.02 * nrm(ks[4], (N_EVEN, HEAD_DIM), f32),
        "diff_q_norm": 1.0 + 0.02 * nrm(ks[5], (N_EVEN, HEAD_DIM), f32),
        "diff_k_norm": 1.0 + 0.02 * nrm(ks[6], (N_EVEN, HEAD_DIM), f32),
        "diff_lam_q1": 0.1 * nrm(ks[7], (N_EVEN, HEAD_DIM), f32),
        "diff_lam_k1": 0.1 * nrm(ks[8], (N_EVEN, HEAD_DIM), f32),
        "diff_lam_q2": 0.1 * nrm(ks[9], (N_EVEN, HEAD_DIM), f32),
        "diff_lam_k2": 0.1 * nrm(ks[10], (N_EVEN, HEAD_DIM), f32),
        "diff_subln": 1.0 + 0.02 * nrm(ks[11], (N_EVEN, DIFF_V_DIM), f32),
        "att_w_out": nrm(ks[12], (N_EVEN, ATT_OUT_IN, D), f32) * ATT_OUT_IN ** -0.5,
        "sgu_norm": 1.0 + 0.02 * nrm(ks[13], (N_ODD, D), f32),
        "sgu_w_in": nrm(ks[14], (N_ODD, D, SGU_IN), f32) * D ** -0.5,
        "sgu_ln_g": 1.0 + 0.02 * nrm(ks[15], (N_ODD, SGU_WIDTH), f32),
        "sgu_ln_b": 0.02 * nrm(ks[16], (N_ODD, SGU_WIDTH), f32),
        "sgu_w_s": nrm(ks[17], (N_ODD, SGU_GROUPS, SGU_CHUNK, SGU_CHUNK), f32) * SGU_CHUNK ** -0.5,
        "sgu_b_s": 1.0 + 0.1 * nrm(ks[18], (N_ODD, SGU_GROUPS, SGU_CHUNK), f32),
        "sgu_w_out": nrm(ks[19], (N_ODD, SGU_WIDTH, D), f32) * SGU_WIDTH ** -0.5,
    }


def reference(x, att_norm, att_w_in, dsw_q_norm, dsw_k_norm, diff_q_norm, diff_k_norm,
              diff_lam_q1, diff_lam_k1, diff_lam_q2, diff_lam_k2, diff_subln, att_w_out,
              sgu_norm, sgu_w_in, sgu_ln_g, sgu_ln_b, sgu_w_s, sgu_b_s, sgu_w_out):
    cos, sin = rope_tables(x.shape[1])
    for i in range(DEPTH):
        j = i // 2
        if i % 2 == 0:
            lam_init = 0.8 - 0.6 * math.exp(-0.3 * i)
            x = hybrid_attention_layer(
                x, att_norm[j], att_w_in[j], dsw_q_norm[j], dsw_k_norm[j],
                diff_q_norm[j], diff_k_norm[j], diff_lam_q1[j], diff_lam_k1[j],
                diff_lam_q2[j], diff_lam_k2[j], diff_subln[j], att_w_out[j],
                cos, sin, lam_init)
        else:
            x = sgu_layer(x, sgu_norm[j], sgu_w_in[j], sgu_ln_g[j], sgu_ln_b[j],
                          sgu_w_s[j], sgu_b_s[j], sgu_w_out[j])
    return x
```

```python
import functools
import math

import jax
import jax.numpy as jnp
from jax import lax
from jax.experimental import pallas as pl
from jax.experimental.pallas import tpu as pltpu

F32 = jnp.float32
BF16 = jnp.bfloat16

HEAD_DIM = 64
ROT_HALF = 8
ROPE_THETA = 500000.0
BLK = 128
N_GROUPS64 = 8
SPLIT = N_GROUPS64 * HEAD_DIM
DSW_PATTERNS = ((128, 1), (512, 4), (2048, 16))
DIFF_HEADS = 4
DIFF_V_DIM = 128
SGU_GROUPS = 8
SGU_CHUNK = 128
RMS_EPS = 1e-6
LN_EPS = 1e-5
NEG_INF = -1e30
LAM_INIT = 0.8 - 0.6 * math.exp(-0.3 * 0)

V7X_VMEM_LIMIT_BYTES = 56 * 1024 * 1024

ROW_TILE = 512
DSW_BLOCKS_PER_STEP = 8
DIFF_TQ = 512
DIFF_TK = 512


def _params(*sem):
    return pltpu.CompilerParams(dimension_semantics=sem, vmem_limit_bytes=V7X_VMEM_LIMIT_BYTES)


def _const_spec(shape):
    nd = len(shape)
    return pl.BlockSpec(shape, lambda *_: (0,) * nd)


def _norm_rope_fm(p, gain, cos, sin, scale):
    t = p.shape[1]
    p3 = p.reshape(N_GROUPS64, HEAD_DIM, t)
    ms = jnp.mean(p3 * p3, axis=1, keepdims=True)
    y = p3 * lax.rsqrt(ms + RMS_EPS) * gain
    x1 = y[:, 0:ROT_HALF, :]
    x2 = y[:, ROT_HALF:2 * ROT_HALF, :]
    out = jnp.concatenate([x1 * cos - x2 * sin, x2 * cos + x1 * sin, y[:, 2 * ROT_HALF:, :]], axis=1)
    if scale != 1.0:
        out = out * scale
    return out.reshape(SPLIT, t)


def _att_in_kernel(x_ref, ng_ref, wtok_ref, wfm_ref, gains_ref, cos_ref, sin_ref,
                   va_ref, g_ref, qa_ref, ka_ref, qd_ref, kd_ref, vd_ref):
    x = x_ref[...]
    ms = jnp.mean(x * x, axis=-1, keepdims=True)
    h = (x * lax.rsqrt(ms + RMS_EPS) * ng_ref[...]).astype(BF16)
    tok = jnp.dot(h, wtok_ref[...], preferred_element_type=F32)
    va_ref[...] = tok[:, :SPLIT].astype(BF16)
    g_ref[...] = tok[:, SPLIT:]
    cos = cos_ref[...][None]
    sin = sin_ref[...][None]
    score_scale = HEAD_DIM ** -0.5
    outs = (qa_ref, ka_ref, qd_ref, kd_ref)
    for idx, o_ref in enumerate(outs):
        p = lax.dot_general(wfm_ref[idx * SPLIT:(idx + 1) * SPLIT, :], h, (((1,), (1,)), ((), ())),
                            preferred_element_type=F32)
        scale = score_scale if idx in (0, 2) else 1.0
        o_ref[...] = _norm_rope_fm(p, gains_ref[idx], cos, sin, scale).astype(BF16)
    p = lax.dot_general(wfm_ref[4 * SPLIT:5 * SPLIT, :], h, (((1,), (1,)), ((), ())),
                        preferred_element_type=F32)
    vd_ref[...] = p.astype(BF16)


def _att_in(x2d, norm_g, w_in, gains, cos_t, sin_t):
    s, d = x2d.shape
    tm = ROW_TILE
    w = w_in.astype(BF16)
    w_tok = jnp.concatenate([w[:, 2 * SPLIT:4 * SPLIT], w[:, 7 * SPLIT:8 * SPLIT]], axis=1)
    w_fm = jnp.concatenate([w[:, 0:2 * SPLIT], w[:, 4 * SPLIT:7 * SPLIT]], axis=1).T
    fm = jax.ShapeDtypeStruct((SPLIT, s), BF16)
    fm_spec = pl.BlockSpec((SPLIT, tm), lambda i: (0, i))
    return pl.pallas_call(
        _att_in_kernel,
        grid=(s // tm,),
        in_specs=[
            pl.BlockSpec((tm, d), lambda i: (i, 0)),
            _const_spec((1, d)),
            _const_spec(w_tok.shape),
            _const_spec(w_fm.shape),
            _const_spec(gains.shape),
            pl.BlockSpec((ROT_HALF, tm), lambda i: (0, i)),
            pl.BlockSpec((ROT_HALF, tm), lambda i: (0, i)),
        ],
        out_specs=[
            pl.BlockSpec((tm, SPLIT), lambda i: (i, 0)),
            pl.BlockSpec((tm, 2 * SPLIT), lambda i: (i, 0)),
            fm_spec, fm_spec, fm_spec, fm_spec, fm_spec,
        ],
        out_shape=[
            jax.ShapeDtypeStruct((s, SPLIT), BF16),
            jax.ShapeDtypeStruct((s, 2 * SPLIT), F32),
            fm, fm, fm, fm, fm,
        ],
        compiler_params=_params("parallel"),
        name="att_in_proj",
    )(x2d, norm_g.reshape(1, d), w_tok, w_fm, gains, cos_t, sin_t)


def _dsw_kernel(*refs, has_state, blocks_per_step):
    if has_state:
        (q_ref, kprev_ref, kcur_ref, vprev_ref, vcur_ref, oin_ref, lsein_ref,
         o_ref, lse_ref, kall, vall) = refs
    else:
        (q_ref, kprev_ref, kcur_ref, vprev_ref, vcur_ref, o_ref, lse_ref, kall, vall) = refs
        oin_ref = lsein_ref = None
    g_blocks = blocks_per_step
    step = pl.program_id(1)
    kall[0:BLK, :] = kprev_ref[0]
    vall[0:BLK, :] = vprev_ref[0]
    kall[BLK:, :] = kcur_ref[...].reshape(g_blocks * BLK, SPLIT)
    vall[BLK:, :] = vcur_ref[...].reshape(g_blocks * BLK, SPLIT)

    qi = lax.broadcasted_iota(jnp.int32, (BLK, 2 * BLK), 0) + BLK
    kj = lax.broadcasted_iota(jnp.int32, (BLK, 2 * BLK), 1)
    dist = qi - kj
    band = (dist >= 0) & (dist <= BLK)
    lane = lax.broadcasted_iota(jnp.int32, (BLK, 2 * HEAD_DIM), 1)
    low_half = lane < HEAD_DIM

    def block(g, carry):
        first_key = jnp.where((step * g_blocks + g) == 0, BLK, 0)
        mask = band & (kj >= first_key)
        row0 = pl.multiple_of(g * BLK, BLK)
        lse_tile = jnp.zeros((BLK, BLK), F32)
        lse_in = lsein_ref[g] if has_state else None
        for hp in range(N_GROUPS64 // 2):
            cols = slice(hp * 2 * HEAD_DIM, (hp + 1) * 2 * HEAD_DIM)
            q_pair = q_ref[g, :, cols]
            kk = kall[pl.ds(row0, 2 * BLK), cols]
            vv = vall[pl.ds(row0, 2 * BLK), cols]
            o_in = oin_ref[g, :, cols] if has_state else None
            halves = []
            for e in range(2):
                head = 2 * hp + e
                sel = low_half if e == 0 else jnp.logical_not(low_half)
                qm = jnp.where(sel, q_pair, jnp.zeros_like(q_pair))
                sc = lax.dot_general(qm, kk, (((1,), (1,)), ((), ())), preferred_element_type=F32)
                sc = jnp.where(mask, sc, NEG_INF)
                m_blk = jnp.max(sc, axis=-1, keepdims=True)
                if has_state:
                    m_old = lse_in[:, head:head + 1]
                    m_new = jnp.maximum(m_old, m_blk)
                    alpha = jnp.exp(m_old - m_new)
                else:
                    m_new = m_blk
                p = jnp.exp(sc - m_new)
                l = jnp.sum(p, axis=-1, keepdims=True)
                pv = jnp.dot(p.astype(BF16), vv, preferred_element_type=F32)
                if has_state:
                    l = l + alpha
                    pv = pv + alpha * o_in
                halves.append(pv / l)
                lse_tile = jnp.where(lane == head, m_new + jnp.log(l), lse_tile)
            o_ref[g, :, cols] = jnp.where(low_half, halves[0], halves[1])
        lse_ref[g] = lse_tile
        return carry

    lax.fori_loop(0, g_blocks, block, 0)


def _dsw_pattern(q, k, v, dil, state):
    s = q.shape[0]
    nb = s // (BLK * dil)
    g_blocks = min(DSW_BLOCKS_PER_STEP, nb)
    view = lambda a, width: a.reshape(nb, BLK, dil * width)
    cur_spec = lambda width: pl.BlockSpec((g_blocks, BLK, width), lambda r, j: (j, 0, r))
    prev_spec = pl.BlockSpec((1, BLK, SPLIT), lambda r, j: (jnp.maximum(j * g_blocks - 1, 0), 0, r))
    operands = [view(q, SPLIT), view(k, SPLIT), view(k, SPLIT), view(v, SPLIT), view(v, SPLIT)]
    in_specs = [cur_spec(SPLIT), prev_spec, cur_spec(SPLIT), prev_spec, cur_spec(SPLIT)]
    if state is not None:
        operands += [view(state[0], SPLIT), view(state[1], BLK)]
        in_specs += [cur_spec(SPLIT), cur_spec(BLK)]
    o, lse = pl.pallas_call(
        functools.partial(_dsw_kernel, has_state=state is not None, blocks_per_step=g_blocks),
        grid=(dil, nb // g_blocks),
        in_specs=in_specs,
        out_specs=[cur_spec(SPLIT), cur_spec(BLK)],
        out_shape=[jax.ShapeDtypeStruct((nb, BLK, dil * SPLIT), F32),
                   jax.ShapeDtypeStruct((nb, BLK, dil * BLK), F32)],
        scratch_shapes=[pltpu.VMEM(((g_blocks + 1) * BLK, SPLIT), BF16),
                        pltpu.VMEM(((g_blocks + 1) * BLK, SPLIT), BF16)],
        compiler_params=_params("parallel", "arbitrary"),
        name=f"dsw_attention_dil{dil}",
    )(*operands)
    return o.reshape(s, SPLIT), lse.reshape(s, BLK)


def _dsw_attention(q, k, v):
    state = None
    for _, dil in DSW_PATTERNS:
        state = _dsw_pattern(q, k, v, dil, state)
    return state[0]


def _diff_kernel(qt_ref, k_ref, vt_ref, lam_ref, sg_ref, o_ref, m_sc, l_sc, acc_sc, *, tq, tk):
    i = pl.program_id(1)
    qt = qt_ref[...]
    m_sc[...] = jnp.full(m_sc.shape, NEG_INF, F32)
    l_sc[...] = jnp.zeros(l_sc.shape, F32)
    acc_sc[...] = jnp.zeros(acc_sc.shape, F32)

    def chunk(j, masked):
        k0 = pl.multiple_of(j * tk, tk)
        vt = vt_ref[:, pl.ds(k0, tk)]
        if masked:
            kpos = k0 + lax.broadcasted_iota(jnp.int32, (tk, tq), 0)
            qpos = i * tq + lax.broadcasted_iota(jnp.int32, (tk, tq), 1)
            keep = kpos <= qpos
        for c in range(2):
            kc = k_ref[c, pl.ds(k0, tk), :]
            st = jnp.dot(kc, qt[c * HEAD_DIM:(c + 1) * HEAD_DIM, :], preferred_element_type=F32)
            if masked:
                st = jnp.where(keep, st, NEG_INF)
            m_old = m_sc[c]
            m_new = jnp.maximum(m_old, jnp.max(st, axis=0, keepdims=True))
            alpha = jnp.exp(m_old - m_new)
            p = jnp.exp(st - m_new)
            l_sc[c] = alpha * l_sc[c] + jnp.sum(p, axis=0, keepdims=True)
            acc_sc[c] = alpha * acc_sc[c] + jnp.dot(vt, p.astype(BF16), preferred_element_type=F32)
            m_sc[c] = m_new

    n_full = (i * tq) // tk

    def full_chunk(j, carry):
        chunk(j, False)
        return carry

    lax.fori_loop(0, n_full, full_chunk, 0)
    for jj in range(tq // tk):
        chunk(n_full + jj, True)

    lam_p = lam_ref[...]
    lam = (jnp.exp(jnp.sum(lam_p[0:1] * lam_p[1:2], axis=-1, keepdims=True))
           - jnp.exp(jnp.sum(lam_p[2:3] * lam_p[3:4], axis=-1, keepdims=True)) + LAM_INIT)
    od = acc_sc[0] / l_sc[0] - lam * (acc_sc[1] / l_sc[1])
    ms = jnp.mean(od * od, axis=0, keepdims=True)
    y = od * lax.rsqrt(ms + RMS_EPS) * sg_ref[...] * (1.0 - LAM_INIT)
    o_ref[...] = y.T


def _diff_attention(qd_t, kd_tok, vd_t, lam_params, subln_col):
    s = qd_t.shape[1]
    tq, tk = DIFF_TQ, DIFF_TK
    return pl.pallas_call(
        functools.partial(_diff_kernel, tq=tq, tk=tk),
        grid=(DIFF_HEADS, s // tq),
        in_specs=[
            pl.BlockSpec((DIFF_V_DIM, tq), lambda h, i: (h, i)),
            pl.BlockSpec((2, s, HEAD_DIM), lambda h, i: (h, 0, 0)),
            pl.BlockSpec((DIFF_V_DIM, s), lambda h, i: (h, 0)),
            _const_spec(lam_params.shape),
            _const_spec(subln_col.shape),
        ],
        out_specs=pl.BlockSpec((tq, DIFF_V_DIM), lambda h, i: (i, h)),
        out_shape=jax.ShapeDtypeStruct((s, DIFF_HEADS * DIFF_V_DIM), F32),
        scratch_shapes=[pltpu.VMEM((2, 1, tq), F32), pltpu.VMEM((2, 1, tq), F32),
                        pltpu.VMEM((2, DIFF_V_DIM, tq), F32)],
        compiler_params=_params("parallel", "arbitrary"),
        name="diff_attention",
    )(qd_t, kd_tok, vd_t, lam_params, subln_col)


def _att_out_kernel(x_ref, oa_ref, od_ref, g_ref, w_ref, o_ref):
    g = g_ref[...]
    y = jnp.concatenate([oa_ref[...], od_ref[...]], axis=-1) * (g * jax.nn.sigmoid(g))
    o_ref[...] = x_ref[...] + jnp.dot(y.astype(BF16), w_ref[...], preferred_element_type=F32)


def _att_out(x2d, oa, od, g, w_out):
    s, d = x2d.shape
    tm = ROW_TILE
    row = lambda width: pl.BlockSpec((tm, width), lambda i: (i, 0))
    return pl.pallas_call(
        _att_out_kernel,
        grid=(s // tm,),
        in_specs=[row(d), row(SPLIT), row(SPLIT), row(2 * SPLIT), _const_spec(w_out.shape)],
        out_specs=row(d),
        out_shape=jax.ShapeDtypeStruct((s, d), F32),
        compiler_params=_params("parallel"),
        name="att_out_proj",
    )(x2d, oa, od, g, w_out.astype(BF16))


def _gelu(x):
    return 0.5 * x * (1.0 + lax.erf(x * (2.0 ** -0.5)))


def _sgu_kernel(x_ref, ng_ref, win_ref, lng_ref, lnb_ref, ws_ref, bs_ref, wout_ref, o_ref, *, width):
    x = x_ref[...]
    tm = x.shape[0]
    ms = jnp.mean(x * x, axis=-1, keepdims=True)
    h = (x * lax.rsqrt(ms + RMS_EPS) * ng_ref[...]).astype(BF16)
    v = _gelu(jnp.dot(h, win_ref[:, width:2 * width], preferred_element_type=F32))
    mu = jnp.mean(v, axis=-1, keepdims=True)
    vc = v - mu
    vn = vc * lax.rsqrt(jnp.mean(vc * vc, axis=-1, keepdims=True) + LN_EPS)
    vn = (vn * lng_ref[...] + lnb_ref[...]).astype(BF16)
    gw = width // SGU_GROUPS
    row = lax.broadcasted_iota(jnp.int32, (SGU_CHUNK, SGU_CHUNK), 0)
    col = lax.broadcasted_iota(jnp.int32, (SGU_CHUNK, SGU_CHUNK), 1)
    causal = col <= row
    sp_groups = []
    for grp in range(SGU_GROUPS):
        ws = jnp.where(causal, ws_ref[grp], 0.0).astype(BF16)
        bias = bs_ref[grp]
        chunks = []
        for c in range(tm // SGU_CHUNK):
            vg = vn[c * SGU_CHUNK:(c + 1) * SGU_CHUNK, grp * gw:(grp + 1) * gw]
            chunks.append(jnp.dot(ws, vg, preferred_element_type=F32) + bias)
        sp_groups.append(jnp.concatenate(chunks, axis=0))
    sp = jnp.concatenate(sp_groups, axis=1)
    u = _gelu(jnp.dot(h, win_ref[:, 0:width], preferred_element_type=F32))
    g = jnp.dot(h, win_ref[:, 2 * width:3 * width], preferred_element_type=F32)
    y = u * sp * (g * jax.nn.sigmoid(g))
    o_ref[...] = x + jnp.dot(y.astype(BF16), wout_ref[...], preferred_element_type=F32)


def _sgu_layer(x2d, norm_g, w_in, ln_g, ln_b, w_s, b_s, w_out):
    s, d = x2d.shape
    width = w_out.shape[0]
    tm = ROW_TILE
    row = pl.BlockSpec((tm, d), lambda i: (i, 0))
    single = lambda shape: pl.BlockSpec(shape, lambda *_: (0,) * len(shape), pipeline_mode=pl.Buffered(1))
    return pl.pallas_call(
        functools.partial(_sgu_kernel, width=width),
        grid=(s // tm,),
        in_specs=[row, _const_spec((1, d)), single(w_in.shape), _const_spec((1, width)), _const_spec((1, width)),
                  _const_spec(w_s.shape), _const_spec((SGU_GROUPS, SGU_CHUNK, 1)), single(w_out.shape)],
        out_specs=row,
        out_shape=jax.ShapeDtypeStruct((s, d), F32),
        compiler_params=_params("parallel"),
        name="sgu_layer",
    )(x2d, norm_g.reshape(1, d), w_in.astype(BF16), ln_g.reshape(1, width), ln_b.reshape(1, width),
      w_s, b_s.reshape(SGU_GROUPS, SGU_CHUNK, 1), w_out.astype(BF16))


def _rope_tables_fm(seq_len):
    pos = jnp.arange(seq_len, dtype=F32)
    inv = 1.0 / (ROPE_THETA ** (jnp.arange(0, 2 * ROT_HALF, 2, dtype=F32) / (2 * ROT_HALF)))
    ang = inv[:, None] * pos[None, :]
    return jnp.cos(ang), jnp.sin(ang)


def kernel(x, att_norm, att_w_in, dsw_q_norm, dsw_k_norm, diff_q_norm, diff_k_norm, diff_lam_q1, diff_lam_k1,
           diff_lam_q2, diff_lam_k2, diff_subln, att_w_out, sgu_norm, sgu_w_in, sgu_ln_g, sgu_ln_b, sgu_w_s,
           sgu_b_s, sgu_w_out):
    b, s, d = x.shape
    assert b == 1 and s % (BLK * DSW_PATTERNS[-1][1]) == 0
    x2d = x.reshape(s, d)
    cos_t, sin_t = _rope_tables_fm(s)

    gains = jnp.stack([dsw_q_norm[0], dsw_k_norm[0], diff_q_norm[0], diff_k_norm[0]])
    gains = jnp.broadcast_to(gains[:, None, :, None], (4, N_GROUPS64, HEAD_DIM, 1))
    va, g, qa_t, ka_t, qd_t, kd_t, vd_t = _att_in(x2d, att_norm[0], att_w_in[0], gains, cos_t, sin_t)
    oa = _dsw_attention(qa_t.T, ka_t.T, va)
    kd_tok = kd_t.reshape(2 * DIFF_HEADS, HEAD_DIM, s).transpose(0, 2, 1)
    lam_params = jnp.stack([diff_lam_q1[0], diff_lam_k1[0], diff_lam_q2[0], diff_lam_k2[0]])
    od = _diff_attention(qd_t, kd_tok, vd_t, lam_params, diff_subln[0].reshape(DIFF_V_DIM, 1))
    x2d = _att_out(x2d, oa, od, g, att_w_out[0])

    x2d = _sgu_layer(x2d, sgu_norm[0], sgu_w_in[0], sgu_ln_g[0], sgu_ln_b[0], sgu_w_s[0], sgu_b_s[0], sgu_w_out[0])
    return x2d.reshape(b, s, d)
```

```python
import functools
import math

import jax
import jax.numpy as jnp
from jax import lax
from jax.experimental import pallas as pl
from jax.experimental.pallas import tpu as pltpu

F32 = jnp.float32
BF16 = jnp.bfloat16

HEAD_DIM = 64
ROT_HALF = 8
ROPE_THETA = 500000.0
BLK = 128
N_GROUPS64 = 8
SPLIT = N_GROUPS64 * HEAD_DIM
DSW_PATTERNS = ((128, 1), (512, 4), (2048, 16))
DIFF_HEADS = 4
DIFF_V_DIM = 128
SGU_GROUPS = 8
SGU_CHUNK = 128
RMS_EPS = 1e-6
LN_EPS = 1e-5
NEG_INF = -1e30
LAM_INIT = 0.8 - 0.6 * math.exp(-0.3 * 0)
LOG2_E = math.log2(math.e)
DIFF_UNSHIFTED_SCORE_BOUND = 96.0

V7X_VMEM_LIMIT_BYTES = 56 * 1024 * 1024

ROW_TILE = 512
DSW_BLOCKS_PER_STEP = 8
DIFF_TQ = 1024
DIFF_TK = 1024
DIFF_DIAG_BLOCK = 512


def _params(*sem):
    return pltpu.CompilerParams(dimension_semantics=sem, vmem_limit_bytes=V7X_VMEM_LIMIT_BYTES)


def _const_spec(shape):
    nd = len(shape)
    return pl.BlockSpec(shape, lambda *_: (0,) * nd)


def _norm_rope_fm(p, gain, cos, sin, scale):
    t = p.shape[1]
    p3 = p.reshape(N_GROUPS64, HEAD_DIM, t)
    ms = jnp.mean(p3 * p3, axis=1, keepdims=True)
    y = p3 * lax.rsqrt(ms + RMS_EPS) * gain
    x1 = y[:, 0:ROT_HALF, :]
    x2 = y[:, ROT_HALF:2 * ROT_HALF, :]
    out = jnp.concatenate([x1 * cos - x2 * sin, x2 * cos + x1 * sin, y[:, 2 * ROT_HALF:, :]], axis=1)
    if scale != 1.0:
        out = out * scale
    return out.reshape(SPLIT, t)


def _att_in_kernel(x_ref, ng_ref, wtok_ref, wfm_ref, gains_ref, cos_ref, sin_ref,
                   va_ref, g_ref, qa_ref, ka_ref, qd_ref, kd_ref, vd_ref):
    x = x_ref[...]
    ms = jnp.mean(x * x, axis=-1, keepdims=True)
    h = (x * lax.rsqrt(ms + RMS_EPS) * ng_ref[...]).astype(BF16)
    tok = jnp.dot(h, wtok_ref[...], preferred_element_type=F32)
    va_ref[...] = tok[:, :SPLIT].astype(BF16)
    g_ref[...] = tok[:, SPLIT:]
    cos = cos_ref[...][None]
    sin = sin_ref[...][None]
    scales = (HEAD_DIM ** -0.5, 1.0, HEAD_DIM ** -0.5 * LOG2_E, 1.0)
    outs = (qa_ref, ka_ref, qd_ref, kd_ref)
    for idx, o_ref in enumerate(outs):
        p = lax.dot_general(wfm_ref[idx * SPLIT:(idx + 1) * SPLIT, :], h, (((1,), (1,)), ((), ())),
                            preferred_element_type=F32)
        o_ref[...] = _norm_rope_fm(p, gains_ref[idx], cos, sin, scales[idx]).astype(BF16)
    p = lax.dot_general(wfm_ref[4 * SPLIT:5 * SPLIT, :], h, (((1,), (1,)), ((), ())),
                        preferred_element_type=F32)
    vd_ref[...] = p.astype(BF16)


def _att_in(x2d, norm_g, w_in, gains, cos_t, sin_t):
    s, d = x2d.shape
    tm = ROW_TILE
    w = w_in.astype(BF16)
    w_tok = jnp.concatenate([w[:, 2 * SPLIT:4 * SPLIT], w[:, 7 * SPLIT:8 * SPLIT]], axis=1)
    w_fm = jnp.concatenate([w[:, 0:2 * SPLIT], w[:, 4 * SPLIT:7 * SPLIT]], axis=1).T
    fm = jax.ShapeDtypeStruct((SPLIT, s), BF16)
    fm_spec = pl.BlockSpec((SPLIT, tm), lambda i: (0, i))
    return pl.pallas_call(
        _att_in_kernel,
        grid=(s // tm,),
        in_specs=[
            pl.BlockSpec((tm, d), lambda i: (i, 0)),
            _const_spec((1, d)),
            _const_spec(w_tok.shape),
            _const_spec(w_fm.shape),
            _const_spec(gains.shape),
            pl.BlockSpec((ROT_HALF, tm), lambda i: (0, i)),
            pl.BlockSpec((ROT_HALF, tm), lambda i: (0, i)),
        ],
        out_specs=[
            pl.BlockSpec((tm, SPLIT), lambda i: (i, 0)),
            pl.BlockSpec((tm, 2 * SPLIT), lambda i: (i, 0)),
            fm_spec, fm_spec, fm_spec, fm_spec, fm_spec,
        ],
        out_shape=[
            jax.ShapeDtypeStruct((s, SPLIT), BF16),
            jax.ShapeDtypeStruct((s, 2 * SPLIT), F32),
            fm, fm, fm, fm, fm,
        ],
        compiler_params=_params("parallel"),
        name="att_in_proj",
    )(x2d, norm_g.reshape(1, d), w_tok, w_fm, gains, cos_t, sin_t)


def _dsw_kernel(*refs, has_state, blocks_per_step):
    if has_state:
        (q_ref, kprev_ref, kcur_ref, vprev_ref, vcur_ref, oin_ref, lsein_ref,
         o_ref, lse_ref, kall, vall) = refs
    else:
        (q_ref, kprev_ref, kcur_ref, vprev_ref, vcur_ref, o_ref, lse_ref, kall, vall) = refs
        oin_ref = lsein_ref = None
    g_blocks = blocks_per_step
    step = pl.program_id(1)
    kall[0:BLK, :] = kprev_ref[0]
    vall[0:BLK, :] = vprev_ref[0]
    kall[BLK:, :] = kcur_ref[...].reshape(g_blocks * BLK, SPLIT)
    vall[BLK:, :] = vcur_ref[...].reshape(g_blocks * BLK, SPLIT)

    qi = lax.broadcasted_iota(jnp.int32, (BLK, 2 * BLK), 0) + BLK
    kj = lax.broadcasted_iota(jnp.int32, (BLK, 2 * BLK), 1)
    dist = qi - kj
    band = (dist >= 0) & (dist <= BLK)
    lane = lax.broadcasted_iota(jnp.int32, (BLK, 2 * HEAD_DIM), 1)
    low_half = lane < HEAD_DIM

    def block(g, carry):
        first_key = jnp.where((step * g_blocks + g) == 0, BLK, 0)
        mask = band & (kj >= first_key)
        row0 = pl.multiple_of(g * BLK, BLK)
        lse_tile = jnp.zeros((BLK, BLK), F32)
        lse_in = lsein_ref[g] if has_state else None
        for hp in range(N_GROUPS64 // 2):
            cols = slice(hp * 2 * HEAD_DIM, (hp + 1) * 2 * HEAD_DIM)
            q_pair = q_ref[g, :, cols]
            kk = kall[pl.ds(row0, 2 * BLK), cols]
            vv = vall[pl.ds(row0, 2 * BLK), cols]
            o_in = oin_ref[g, :, cols] if has_state else None
            halves = []
            for e in range(2):
                head = 2 * hp + e
                sel = low_half if e == 0 else jnp.logical_not(low_half)
                qm = jnp.where(sel, q_pair, jnp.zeros_like(q_pair))
                sc = lax.dot_general(qm, kk, (((1,), (1,)), ((), ())), preferred_element_type=F32)
                sc = jnp.where(mask, sc, NEG_INF)
                m_blk = jnp.max(sc, axis=-1, keepdims=True)
                if has_state:
                    m_old = lse_in[:, head:head + 1]
                    m_new = jnp.maximum(m_old, m_blk)
                    alpha = jnp.exp(m_old - m_new)
                else:
                    m_new = m_blk
                p = jnp.exp(sc - m_new)
                l = jnp.sum(p, axis=-1, keepdims=True)
                pv = jnp.dot(p.astype(BF16), vv, preferred_element_type=F32)
                if has_state:
                    l = l + alpha
                    pv = pv + alpha * o_in
                halves.append(pv / l)
                lse_tile = jnp.where(lane == head, m_new + jnp.log(l), lse_tile)
            o_ref[g, :, cols] = jnp.where(low_half, halves[0], halves[1])
        lse_ref[g] = lse_tile
        return carry

    lax.fori_loop(0, g_blocks, block, 0)


def _dsw_pattern(q, k, v, dil, state):
    s = q.shape[0]
    nb = s // (BLK * dil)
    g_blocks = min(DSW_BLOCKS_PER_STEP, nb)
    view = lambda a, width: a.reshape(nb, BLK, dil * width)
    cur_spec = lambda width: pl.BlockSpec((g_blocks, BLK, width), lambda r, j: (j, 0, r))
    prev_spec = pl.BlockSpec((1, BLK, SPLIT), lambda r, j: (jnp.maximum(j * g_blocks - 1, 0), 0, r))
    operands = [view(q, SPLIT), view(k, SPLIT), view(k, SPLIT), view(v, SPLIT), view(v, SPLIT)]
    in_specs = [cur_spec(SPLIT), prev_spec, cur_spec(SPLIT), prev_spec, cur_spec(SPLIT)]
    if state is not None:
        operands += [view(state[0], SPLIT), view(state[1], BLK)]
        in_specs += [cur_spec(SPLIT), cur_spec(BLK)]
    o, lse = pl.pallas_call(
        functools.partial(_dsw_kernel, has_state=state is not None, blocks_per_step=g_blocks),
        grid=(dil, nb // g_blocks),
        in_specs=in_specs,
        out_specs=[cur_spec(SPLIT), cur_spec(BLK)],
        out_shape=[jax.ShapeDtypeStruct((nb, BLK, dil * SPLIT), F32),
                   jax.ShapeDtypeStruct((nb, BLK, dil * BLK), F32)],
        scratch_shapes=[pltpu.VMEM(((g_blocks + 1) * BLK, SPLIT), BF16),
                        pltpu.VMEM(((g_blocks + 1) * BLK, SPLIT), BF16)],
        compiler_params=_params("parallel", "arbitrary"),
        name=f"dsw_attention_dil{dil}",
    )(*operands)
    return o.reshape(s, SPLIT), lse.reshape(s, BLK)


def _dsw_attention(q, k, v):
    state = None
    for _, dil in DSW_PATTERNS:
        state = _dsw_pattern(q, k, v, dil, state)
    return state[0]


def _diff_finalize(o0, o1, lam_ref, sg_ref, o_ref):
    lam_p = lam_ref[...]
    lam = (jnp.exp(jnp.sum(lam_p[0:1] * lam_p[1:2], axis=-1, keepdims=True))
           - jnp.exp(jnp.sum(lam_p[2:3] * lam_p[3:4], axis=-1, keepdims=True)) + LAM_INIT)
    od = o0 - lam * o1
    ms = jnp.mean(od * od, axis=0, keepdims=True)
    y = od * lax.rsqrt(ms + RMS_EPS) * sg_ref[...] * (1.0 - LAM_INIT)
    o_ref[...] = y.T


def _causal_keep(i, k0, tq, tk):
    kpos = k0 + lax.broadcasted_iota(jnp.int32, (tk, tq), 0)
    qpos = i * tq + lax.broadcasted_iota(jnp.int32, (tk, tq), 1)
    return kpos <= qpos


def _diff_unshifted_kernel(qt_ref, k_ref, vt_ref, lam_ref, sg_ref, o_ref, l_sc, acc_sc, *, tq, tk):
    i = pl.program_id(1)
    qt = qt_ref[...]
    l_sc[...] = jnp.zeros(l_sc.shape, F32)
    acc_sc[...] = jnp.zeros(acc_sc.shape, F32)

    def full_chunk(j, carry):
        k0 = pl.multiple_of(j * tk, tk)
        vt = vt_ref[:, pl.ds(k0, tk)]
        for c in range(2):
            kc = k_ref[c, pl.ds(k0, tk), :]
            st = jnp.dot(kc, qt[c * HEAD_DIM:(c + 1) * HEAD_DIM, :], preferred_element_type=F32)
            p = jnp.exp2(st)
            l_sc[c] += jnp.sum(p.reshape(tk // 8, 8, tq), axis=0)
            acc_sc[c] += jnp.dot(vt, p.astype(BF16), preferred_element_type=F32)
        return carry

    lax.fori_loop(0, i, full_chunk, 0)

    sb = DIFF_DIAG_BLOCK
    k0 = pl.multiple_of(i * tk, tk)
    tri = lax.broadcasted_iota(jnp.int32, (sb, sb), 0) <= lax.broadcasted_iota(jnp.int32, (sb, sb), 1)
    for qb in range(tq // sb):
        nk = (qb + 1) * sb
        lanes = slice(qb * sb, (qb + 1) * sb)
        vt = vt_ref[:, pl.ds(k0, nk)]
        for c in range(2):
            kc = k_ref[c, pl.ds(k0, nk), :]
            st = jnp.dot(kc, qt[c * HEAD_DIM:(c + 1) * HEAD_DIM, lanes], preferred_element_type=F32)
            p = jnp.exp2(st)
            p_tri = jnp.where(tri, p[qb * sb:, :], 0.0)
            p = jnp.concatenate([p[:qb * sb, :], p_tri], axis=0) if qb else p_tri
            l_sc[c, :, lanes] += jnp.sum(p.reshape(nk // 8, 8, sb), axis=0)
            acc_sc[c, :, lanes] += jnp.dot(vt, p.astype(BF16), preferred_element_type=F32)

    l0 = jnp.sum(l_sc[0], axis=0, keepdims=True)
    l1 = jnp.sum(l_sc[1], axis=0, keepdims=True)
    _diff_finalize(acc_sc[0] / l0, acc_sc[1] / l1, lam_ref, sg_ref, o_ref)


def _diff_kernel(qt_ref, k_ref, vt_ref, lam_ref, sg_ref, o_ref, m_sc, l_sc, acc_sc, *, tq, tk):
    i = pl.program_id(1)
    qt = qt_ref[...]
    m_sc[...] = jnp.full(m_sc.shape, NEG_INF, F32)
    l_sc[...] = jnp.zeros(l_sc.shape, F32)
    acc_sc[...] = jnp.zeros(acc_sc.shape, F32)

    def chunk(j, masked):
        k0 = pl.multiple_of(j * tk, tk)
        vt = vt_ref[:, pl.ds(k0, tk)]
        for c in range(2):
            kc = k_ref[c, pl.ds(k0, tk), :]
            st = jnp.dot(kc, qt[c * HEAD_DIM:(c + 1) * HEAD_DIM, :], preferred_element_type=F32)
            if masked:
                st = jnp.where(_causal_keep(i, k0, tq, tk), st, NEG_INF)
            m_old = m_sc[c]
            m_new = jnp.maximum(m_old, jnp.max(st, axis=0, keepdims=True))
            alpha = jnp.exp2(m_old - m_new)
            p = jnp.exp2(st - m_new)
            l_sc[c] = alpha * l_sc[c] + jnp.sum(p, axis=0, keepdims=True)
            acc_sc[c] = alpha * acc_sc[c] + jnp.dot(vt, p.astype(BF16), preferred_element_type=F32)
            m_sc[c] = m_new

    n_full = (i * tq) // tk

    def full_chunk(j, carry):
        chunk(j, False)
        return carry

    lax.fori_loop(0, n_full, full_chunk, 0)
    for jj in range(tq // tk):
        chunk(n_full + jj, True)

    _diff_finalize(acc_sc[0] / l_sc[0], acc_sc[1] / l_sc[1], lam_ref, sg_ref, o_ref)


def _diff_call(body, scratch, name, qd_t, kd_tok, vd_t, lam_params, subln_col):
    s = qd_t.shape[1]
    tq, tk = DIFF_TQ, DIFF_TK
    assert tq == tk and s % tq == 0 and tq % DIFF_DIAG_BLOCK == 0
    return pl.pallas_call(
        functools.partial(body, tq=tq, tk=tk),
        grid=(DIFF_HEADS, s // tq),
        in_specs=[
            pl.BlockSpec((DIFF_V_DIM, tq), lambda h, i: (h, i)),
            pl.BlockSpec((2, s, HEAD_DIM), lambda h, i: (h, 0, 0)),
            pl.BlockSpec((DIFF_V_DIM, s), lambda h, i: (h, 0)),
            _const_spec(lam_params.shape),
            _const_spec(subln_col.shape),
        ],
        out_specs=pl.BlockSpec((tq, DIFF_V_DIM), lambda h, i: (i, h)),
        out_shape=jax.ShapeDtypeStruct((s, DIFF_HEADS * DIFF_V_DIM), F32),
        scratch_shapes=scratch(tq),
        compiler_params=_params("parallel", "arbitrary"),
        name=name,
    )(qd_t, kd_tok, vd_t, lam_params, subln_col)


def _diff_attention(qd_t, kd_tok, vd_t, lam_params, subln_col, score_bound):
    acc = lambda tq: pltpu.VMEM((2, DIFF_V_DIM, tq), F32)
    unshifted = functools.partial(
        _diff_call, _diff_unshifted_kernel, lambda tq: [pltpu.VMEM((2, 8, tq), F32), acc(tq)],
        "diff_attention_unshifted")
    online = functools.partial(
        _diff_call, _diff_kernel, lambda tq: [pltpu.VMEM((2, 1, tq), F32), pltpu.VMEM((2, 1, tq), F32), acc(tq)],
        "diff_attention_online")
    return lax.cond(score_bound <= DIFF_UNSHIFTED_SCORE_BOUND, unshifted, online,
                    qd_t, kd_tok, vd_t, lam_params, subln_col)


def _att_out_kernel(x_ref, oa_ref, od_ref, g_ref, w_ref, o_ref):
    g = g_ref[...]
    y = jnp.concatenate([oa_ref[...], od_ref[...]], axis=-1) * (g * jax.nn.sigmoid(g))
    o_ref[...] = x_ref[...] + jnp.dot(y.astype(BF16), w_ref[...], preferred_element_type=F32)


def _att_out(x2d, oa, od, g, w_out):
    s, d = x2d.shape
    tm = ROW_TILE
    row = lambda width: pl.BlockSpec((tm, width), lambda i: (i, 0))
    return pl.pallas_call(
        _att_out_kernel,
        grid=(s // tm,),
        in_specs=[row(d), row(SPLIT), row(SPLIT), row(2 * SPLIT), _const_spec(w_out.shape)],
        out_specs=row(d),
        out_shape=jax.ShapeDtypeStruct((s, d), F32),
        compiler_params=_params("parallel"),
        name="att_out_proj",
    )(x2d, oa, od, g, w_out.astype(BF16))


def _gelu(x):
    return 0.5 * x * (1.0 + lax.erf(x * (2.0 ** -0.5)))


def _sgu_kernel(x_ref, ng_ref, win_ref, lng_ref, lnb_ref, ws_ref, bs_ref, wout_ref, o_ref, *, width):
    x = x_ref[...]
    tm = x.shape[0]
    ms = jnp.mean(x * x, axis=-1, keepdims=True)
    h = (x * lax.rsqrt(ms + RMS_EPS) * ng_ref[...]).astype(BF16)
    v = _gelu(jnp.dot(h, win_ref[:, width:2 * width], preferred_element_type=F32))
    mu = jnp.mean(v, axis=-1, keepdims=True)
    vc = v - mu
    vn = vc * lax.rsqrt(jnp.mean(vc * vc, axis=-1, keepdims=True) + LN_EPS)
    vn = (vn * lng_ref[...] + lnb_ref[...]).astype(BF16)
    gw = width // SGU_GROUPS
    row = lax.broadcasted_iota(jnp.int32, (SGU_CHUNK, SGU_CHUNK), 0)
    col = lax.broadcasted_iota(jnp.int32, (SGU_CHUNK, SGU_CHUNK), 1)
    causal = col <= row
    sp_groups = []
    for grp in range(SGU_GROUPS):
        ws = jnp.where(causal, ws_ref[grp], 0.0).astype(BF16)
        bias = bs_ref[grp]
        chunks = []
        for c in range(tm // SGU_CHUNK):
            vg = vn[c * SGU_CHUNK:(c + 1) * SGU_CHUNK, grp * gw:(grp + 1) * gw]
            chunks.append(jnp.dot(ws, vg, preferred_element_type=F32) + bias)
        sp_groups.append(jnp.concatenate(chunks, axis=0))
    sp = jnp.concatenate(sp_groups, axis=1)
    u = _gelu(jnp.dot(h, win_ref[:, 0:width], preferred_element_type=F32))
    g = jnp.dot(h, win_ref[:, 2 * width:3 * width], preferred_element_type=F32)
    y = u * sp * (g * jax.nn.sigmoid(g))
    o_ref[...] = x + jnp.dot(y.astype(BF16), wout_ref[...], preferred_element_type=F32)


def _sgu_layer(x2d, norm_g, w_in, ln_g, ln_b, w_s, b_s, w_out):
    s, d = x2d.shape
    width = w_out.shape[0]
    tm = ROW_TILE
    row = pl.BlockSpec((tm, d), lambda i: (i, 0))
    single = lambda shape: pl.BlockSpec(shape, lambda *_: (0,) * len(shape), pipeline_mode=pl.Buffered(1))
    return pl.pallas_call(
        functools.partial(_sgu_kernel, width=width),
        grid=(s // tm,),
        in_specs=[row, _const_spec((1, d)), single(w_in.shape), _const_spec((1, width)), _const_spec((1, width)),
                  _const_spec(w_s.shape), _const_spec((SGU_GROUPS, SGU_CHUNK, 1)), single(w_out.shape)],
        out_specs=row,
        out_shape=jax.ShapeDtypeStruct((s, d), F32),
        compiler_params=_params("parallel"),
        name="sgu_layer",
    )(x2d, norm_g.reshape(1, d), w_in.astype(BF16), ln_g.reshape(1, width), ln_b.reshape(1, width),
      w_s, b_s.reshape(SGU_GROUPS, SGU_CHUNK, 1), w_out.astype(BF16))


def _rope_tables_fm(seq_len):
    pos = jnp.arange(seq_len, dtype=F32)
    inv = 1.0 / (ROPE_THETA ** (jnp.arange(0, 2 * ROT_HALF, 2, dtype=F32) / (2 * ROT_HALF)))
    ang = inv[:, None] * pos[None, :]
    return jnp.cos(ang), jnp.sin(ang)


def kernel(x, att_norm, att_w_in, dsw_q_norm, dsw_k_norm, diff_q_norm, diff_k_norm, diff_lam_q1, diff_lam_k1,
           diff_lam_q2, diff_lam_k2, diff_subln, att_w_out, sgu_norm, sgu_w_in, sgu_ln_g, sgu_ln_b, sgu_w_s,
           sgu_b_s, sgu_w_out):
    b, s, d = x.shape
    assert b == 1 and s % (BLK * DSW_PATTERNS[-1][1]) == 0
    x2d = x.reshape(s, d)
    cos_t, sin_t = _rope_tables_fm(s)

    gains = jnp.stack([dsw_q_norm[0], dsw_k_norm[0], diff_q_norm[0], diff_k_norm[0]])
    gains = jnp.broadcast_to(gains[:, None, :, None], (4, N_GROUPS64, HEAD_DIM, 1))
    va, g, qa_t, ka_t, qd_t, kd_t, vd_t = _att_in(x2d, att_norm[0], att_w_in[0], gains, cos_t, sin_t)
    oa = _dsw_attention(qa_t.T, ka_t.T, va)
    kd_tok = kd_t.reshape(2 * DIFF_HEADS, HEAD_DIM, s).transpose(0, 2, 1)
    lam_params = jnp.stack([diff_lam_q1[0], diff_lam_k1[0], diff_lam_q2[0], diff_lam_k2[0]])
    score_bound = (1.02 * LOG2_E * HEAD_DIM ** 0.5
                   * jnp.max(jnp.abs(diff_q_norm[0])) * jnp.max(jnp.abs(diff_k_norm[0])))
    od = _diff_attention(qd_t, kd_tok, vd_t, lam_params, diff_subln[0].reshape(DIFF_V_DIM, 1), score_bound)
    x2d = _att_out(x2d, oa, od, g, att_w_out[0])

    x2d = _sgu_layer(x2d, sgu_norm[0], sgu_w_in[0], sgu_ln_g[0], sgu_ln_b[0], sgu_w_s[0], sgu_b_s[0], sgu_w_out[0])
    return x2d.reshape(b, s, d)
```

```python
import functools
import math

import numpy as np
import jax
import jax.numpy as jnp
from jax import lax
from jax.experimental import pallas as pl
from jax.experimental.pallas import tpu as pltpu

F32 = jnp.float32
BF16 = jnp.bfloat16

HEAD_DIM = 64
ROT_HALF = 8
ROPE_THETA = 500000.0
BLK = 128
N_GROUPS64 = 8
SPLIT = N_GROUPS64 * HEAD_DIM
DSW_PATTERNS = ((128, 1), (512, 4), (2048, 16))
DIFF_HEADS = 4
DIFF_V_DIM = 128
SGU_GROUPS = 8
SGU_CHUNK = 128
RMS_EPS = 1e-6
LN_EPS = 1e-5
NEG_INF = -1e30
LAM_INIT = 0.8 - 0.6 * math.exp(-0.3 * 0)
LOG2_E = math.log2(math.e)
DIFF_UNSHIFTED_SCORE_BOUND = 96.0
DSW_UNSHIFTED_SCORE_BOUND = 96.0

V7X_VMEM_LIMIT_BYTES = 56 * 1024 * 1024

ROW_TILE = 512
DSW_BLOCKS_PER_STEP = 8
DIFF_TQ = 1024
DIFF_TK = 1024
DIFF_DIAG_BLOCK = 512


def _params(*sem):
    return pltpu.CompilerParams(dimension_semantics=sem, vmem_limit_bytes=V7X_VMEM_LIMIT_BYTES)


def _const_spec(shape):
    nd = len(shape)
    return pl.BlockSpec(shape, lambda *_: (0,) * nd)


def _norm_rope_fm(p, gain, cos, sin, scale):
    t = p.shape[1]
    p3 = p.reshape(N_GROUPS64, HEAD_DIM, t)
    ms = jnp.mean(p3 * p3, axis=1, keepdims=True)
    y = p3 * lax.rsqrt(ms + RMS_EPS) * gain
    x1 = y[:, 0:ROT_HALF, :]
    x2 = y[:, ROT_HALF:2 * ROT_HALF, :]
    out = jnp.concatenate([x1 * cos - x2 * sin, x2 * cos + x1 * sin, y[:, 2 * ROT_HALF:, :]], axis=1)
    if scale != 1.0:
        out = out * scale
    return out.reshape(SPLIT, t)


def _att_in_kernel(x_ref, ng_ref, wtok_ref, wfm_ref, gains_ref, cos_ref, sin_ref,
                   va_ref, g_ref, qa_ref, ka_ref, qd_ref, kd_ref, vd_ref):
    x = x_ref[...]
    ms = jnp.mean(x * x, axis=-1, keepdims=True)
    h = (x * lax.rsqrt(ms + RMS_EPS) * ng_ref[...]).astype(BF16)
    tok = jnp.dot(h, wtok_ref[...], preferred_element_type=F32)
    va_ref[...] = tok[:, :SPLIT].astype(BF16)
    g_ref[...] = tok[:, SPLIT:]
    cos = cos_ref[...][None]
    sin = sin_ref[...][None]
    scales = (HEAD_DIM ** -0.5 * LOG2_E, 1.0, HEAD_DIM ** -0.5 * LOG2_E, 1.0)
    outs = (qa_ref, ka_ref, qd_ref, kd_ref)
    for idx, o_ref in enumerate(outs):
        p = lax.dot_general(wfm_ref[idx * SPLIT:(idx + 1) * SPLIT, :], h, (((1,), (1,)), ((), ())),
                            preferred_element_type=F32)
        o_ref[...] = _norm_rope_fm(p, gains_ref[idx], cos, sin, scales[idx]).astype(BF16)
    p = lax.dot_general(wfm_ref[4 * SPLIT:5 * SPLIT, :], h, (((1,), (1,)), ((), ())),
                        preferred_element_type=F32)
    vd_ref[...] = p.astype(BF16)


def _att_in(x2d, norm_g, w_in, gains, cos_t, sin_t):
    s, d = x2d.shape
    tm = ROW_TILE
    w = w_in.astype(BF16)
    w_tok = jnp.concatenate([w[:, 2 * SPLIT:4 * SPLIT], w[:, 7 * SPLIT:8 * SPLIT]], axis=1)
    w_fm = jnp.concatenate([w[:, 0:2 * SPLIT], w[:, 4 * SPLIT:7 * SPLIT]], axis=1).T
    fm = jax.ShapeDtypeStruct((SPLIT, s), BF16)
    fm_spec = pl.BlockSpec((SPLIT, tm), lambda i: (0, i))
    return pl.pallas_call(
        _att_in_kernel,
        grid=(s // tm,),
        in_specs=[
            pl.BlockSpec((tm, d), lambda i: (i, 0)),
            _const_spec((1, d)),
            _const_spec(w_tok.shape),
            _const_spec(w_fm.shape),
            _const_spec(gains.shape),
            pl.BlockSpec((ROT_HALF, tm), lambda i: (0, i)),
            pl.BlockSpec((ROT_HALF, tm), lambda i: (0, i)),
        ],
        out_specs=[
            pl.BlockSpec((tm, SPLIT), lambda i: (i, 0)),
            pl.BlockSpec((tm, 2 * SPLIT), lambda i: (i, 0)),
            fm_spec, fm_spec, fm_spec, fm_spec, fm_spec,
        ],
        out_shape=[
            jax.ShapeDtypeStruct((s, SPLIT), BF16),
            jax.ShapeDtypeStruct((s, 2 * SPLIT), F32),
            fm, fm, fm, fm, fm,
        ],
        compiler_params=_params("parallel"),
        name="att_in_proj",
    )(x2d, norm_g.reshape(1, d), w_tok, w_fm, gains, cos_t, sin_t)


def _dsw_kernel(*refs, has_state, blocks_per_step):
    if has_state:
        (q_ref, kprev_ref, kcur_ref, vprev_ref, vcur_ref, oin_ref, lsein_ref,
         o_ref, lse_ref, kall, vall) = refs
    else:
        (q_ref, kprev_ref, kcur_ref, vprev_ref, vcur_ref, o_ref, lse_ref, kall, vall) = refs
        oin_ref = lsein_ref = None
    g_blocks = blocks_per_step
    step = pl.program_id(1)
    kall[0:BLK, :] = kprev_ref[0]
    vall[0:BLK, :] = vprev_ref[0]
    kall[BLK:, :] = kcur_ref[...].reshape(g_blocks * BLK, SPLIT)
    vall[BLK:, :] = vcur_ref[...].reshape(g_blocks * BLK, SPLIT)

    qi = lax.broadcasted_iota(jnp.int32, (BLK, 2 * BLK), 0) + BLK
    kj = lax.broadcasted_iota(jnp.int32, (BLK, 2 * BLK), 1)
    dist = qi - kj
    band = (dist >= 0) & (dist <= BLK)
    lane = lax.broadcasted_iota(jnp.int32, (BLK, 2 * HEAD_DIM), 1)
    low_half = lane < HEAD_DIM

    def block(g, carry):
        first_key = jnp.where((step * g_blocks + g) == 0, BLK, 0)
        mask = band & (kj >= first_key)
        row0 = pl.multiple_of(g * BLK, BLK)
        lse_tile = jnp.zeros((BLK, BLK), F32)
        lse_in = lsein_ref[g] if has_state else None
        for hp in range(N_GROUPS64 // 2):
            cols = slice(hp * 2 * HEAD_DIM, (hp + 1) * 2 * HEAD_DIM)
            q_pair = q_ref[g, :, cols]
            kk = kall[pl.ds(row0, 2 * BLK), cols]
            vv = vall[pl.ds(row0, 2 * BLK), cols]
            o_in = oin_ref[g, :, cols] if has_state else None
            halves = []
            for e in range(2):
                head = 2 * hp + e
                sel = low_half if e == 0 else jnp.logical_not(low_half)
                qm = jnp.where(sel, q_pair, jnp.zeros_like(q_pair))
                sc = lax.dot_general(qm, kk, (((1,), (1,)), ((), ())), preferred_element_type=F32)
                sc = jnp.where(mask, sc, NEG_INF)
                m_blk = jnp.max(sc, axis=-1, keepdims=True)
                if has_state:
                    m_old = lse_in[:, head:head + 1]
                    m_new = jnp.maximum(m_old, m_blk)
                    alpha = jnp.exp2(m_old - m_new)
                else:
                    m_new = m_blk
                p = jnp.exp2(sc - m_new)
                l = jnp.sum(p, axis=-1, keepdims=True)
                pv = jnp.dot(p.astype(BF16), vv, preferred_element_type=F32)
                if has_state:
                    l = l + alpha
                    pv = pv + alpha * o_in
                halves.append(pv / l)
                lse_tile = jnp.where(lane == head, m_new + jnp.log2(l), lse_tile)
            o_ref[g, :, cols] = jnp.where(low_half, halves[0], halves[1])
        lse_ref[g] = lse_tile
        return carry

    lax.fori_loop(0, g_blocks, block, 0)


def _dsw_pattern(q, k, v, dil, state):
    s = q.shape[0]
    nb = s // (BLK * dil)
    g_blocks = min(DSW_BLOCKS_PER_STEP, nb)
    view = lambda a, width: a.reshape(nb, BLK, dil * width)
    cur_spec = lambda width: pl.BlockSpec((g_blocks, BLK, width), lambda r, j: (j, 0, r))
    prev_spec = pl.BlockSpec((1, BLK, SPLIT), lambda r, j: (jnp.maximum(j * g_blocks - 1, 0), 0, r))
    operands = [view(q, SPLIT), view(k, SPLIT), view(k, SPLIT), view(v, SPLIT), view(v, SPLIT)]
    in_specs = [cur_spec(SPLIT), prev_spec, cur_spec(SPLIT), prev_spec, cur_spec(SPLIT)]
    if state is not None:
        operands += [view(state[0], SPLIT), view(state[1], BLK)]
        in_specs += [cur_spec(SPLIT), cur_spec(BLK)]
    o, lse = pl.pallas_call(
        functools.partial(_dsw_kernel, has_state=state is not None, blocks_per_step=g_blocks),
        grid=(dil, nb // g_blocks),
        in_specs=in_specs,
        out_specs=[cur_spec(SPLIT), cur_spec(BLK)],
        out_shape=[jax.ShapeDtypeStruct((nb, BLK, dil * SPLIT), F32),
                   jax.ShapeDtypeStruct((nb, BLK, dil * BLK), F32)],
        scratch_shapes=[pltpu.VMEM(((g_blocks + 1) * BLK, SPLIT), BF16),
                        pltpu.VMEM(((g_blocks + 1) * BLK, SPLIT), BF16)],
        compiler_params=_params("parallel", "arbitrary"),
        name=f"dsw_attention_dil{dil}",
    )(*operands)
    return o.reshape(s, SPLIT), lse.reshape(s, BLK)


def _dsw_unshifted_kernel(*refs, has_state, last, blocks_per_step):
    refs = list(refs)
    q_ref, kprev_ref, kcur_ref, vprev_ref, vcur_ref, ehot_ref = refs[:6]
    pos = 6
    if last:
        xexp_ref = refs[pos]
        pos += 1
    if has_state:
        accin_ref, lin_ref = refs[pos:pos + 2]
        pos += 2
    if last:
        o_ref = refs[pos]
        pos += 1
    else:
        acc_ref, l_ref = refs[pos:pos + 2]
        pos += 2
    kall, vall = refs[pos:pos + 2]
    g_blocks = blocks_per_step
    step = pl.program_id(1)
    kall[0:BLK, :] = kprev_ref[0]
    vall[0:BLK, :] = vprev_ref[0]
    kall[BLK:, :] = kcur_ref[...].reshape(g_blocks * BLK, SPLIT)
    vall[BLK:, :] = vcur_ref[...].reshape(g_blocks * BLK, SPLIT)

    qi = (lax.broadcasted_iota(jnp.int32, (2 * BLK, 2 * BLK), 0) & (BLK - 1)) + BLK
    kj = lax.broadcasted_iota(jnp.int32, (2 * BLK, 2 * BLK), 1)
    dist = qi - kj
    band = (dist >= 0) & (dist <= BLK)
    q_low = lax.broadcasted_iota(jnp.int32, (BLK, 2 * HEAD_DIM), 1) < HEAD_DIM
    v_low = lax.broadcasted_iota(jnp.int32, (2 * BLK, 2 * HEAD_DIM), 1) < HEAD_DIM

    def block(g, carry):
        first_key = jnp.where((step * g_blocks + g) == 0, BLK, 0)
        mask = band & (kj >= first_key)
        row0 = pl.multiple_of(g * BLK, BLK)
        l_tile = lin_ref[g] if has_state else jnp.zeros((BLK, BLK), F32)
        accs = []
        for hp in range(N_GROUPS64 // 2):
            cols = slice(hp * 2 * HEAD_DIM, (hp + 1) * 2 * HEAD_DIM)
            q_pair = q_ref[g, :, cols]
            kk = kall[pl.ds(row0, 2 * BLK), cols]
            vv = vall[pl.ds(row0, 2 * BLK), cols]
            qz = jnp.zeros_like(q_pair)
            vz = jnp.zeros_like(vv)
            q2 = jnp.concatenate([jnp.where(q_low, q_pair, qz), jnp.where(q_low, qz, q_pair)], axis=0)
            sc = lax.dot_general(q2, kk, (((1,), (1,)), ((), ())), preferred_element_type=F32)
            p2 = jnp.where(mask, jnp.exp2(sc), 0.0).astype(BF16)
            p_cat = jnp.concatenate([p2[:BLK], p2[BLK:]], axis=1)
            v_split = jnp.concatenate([jnp.where(v_low, vv, vz), jnp.where(v_low, vz, vv)], axis=0)
            rhs = jnp.concatenate([v_split, ehot_ref[hp]], axis=1)
            out2 = jnp.dot(p_cat, rhs, preferred_element_type=F32)
            acc = out2[:, :2 * HEAD_DIM]
            if has_state:
                acc = acc + accin_ref[g, :, cols]
            l_tile = l_tile + out2[:, 2 * HEAD_DIM:]
            if last:
                accs.append(acc)
            else:
                acc_ref[g, :, cols] = acc
        if last:
            r = 1.0 / l_tile
            r_hi = r.astype(BF16)
            r_lo = (r - r_hi.astype(F32)).astype(BF16)
            r_exp = (jnp.dot(r_hi, xexp_ref[...], preferred_element_type=F32)
                     + jnp.dot(r_lo, xexp_ref[...], preferred_element_type=F32))
            for hp in range(N_GROUPS64 // 2):
                cols = slice(hp * 2 * HEAD_DIM, (hp + 1) * 2 * HEAD_DIM)
                o_ref[g, :, cols] = accs[hp] * r_exp[:, cols]
        else:
            l_ref[g] = l_tile
        return carry

    lax.fori_loop(0, g_blocks, block, 0)


def _dsw_unshifted_pattern(q, k, v, dil, state, last, ehot, xexp):
    s = q.shape[0]
    nb = s // (BLK * dil)
    g_blocks = min(DSW_BLOCKS_PER_STEP, nb)
    view = lambda a, width: a.reshape(nb, BLK, dil * width)
    cur_spec = lambda width: pl.BlockSpec((g_blocks, BLK, width), lambda r, j: (j, 0, r))
    prev_spec = pl.BlockSpec((1, BLK, SPLIT), lambda r, j: (jnp.maximum(j * g_blocks - 1, 0), 0, r))
    operands = [view(q, SPLIT), view(k, SPLIT), view(k, SPLIT), view(v, SPLIT), view(v, SPLIT), ehot]
    in_specs = [cur_spec(SPLIT), prev_spec, cur_spec(SPLIT), prev_spec, cur_spec(SPLIT), _const_spec(ehot.shape)]
    if last:
        operands.append(xexp)
        in_specs.append(_const_spec(xexp.shape))
    if state is not None:
        operands += [view(state[0], SPLIT), view(state[1], BLK)]
        in_specs += [cur_spec(SPLIT), cur_spec(BLK)]
    acc_shape = jax.ShapeDtypeStruct((nb, BLK, dil * SPLIT), F32)
    l_shape = jax.ShapeDtypeStruct((nb, BLK, dil * BLK), F32)
    out = pl.pallas_call(
        functools.partial(_dsw_unshifted_kernel, has_state=state is not None, last=last, blocks_per_step=g_blocks),
        grid=(dil, nb // g_blocks),
        in_specs=in_specs,
        out_specs=[cur_spec(SPLIT)] if last else [cur_spec(SPLIT), cur_spec(BLK)],
        out_shape=[acc_shape] if last else [acc_shape, l_shape],
        scratch_shapes=[pltpu.VMEM(((g_blocks + 1) * BLK, SPLIT), BF16),
                        pltpu.VMEM(((g_blocks + 1) * BLK, SPLIT), BF16)],
        compiler_params=_params("parallel", "arbitrary"),
        name=f"dsw_unshifted_dil{dil}",
    )(*operands)
    if last:
        return out[0].reshape(s, SPLIT)
    return out[0].reshape(s, SPLIT), out[1].reshape(s, BLK)


def _dsw_constants():
    lanes_per_head = BLK // N_GROUPS64
    ehot = np.zeros((N_GROUPS64 // 2, 4 * BLK, BLK), np.float32)
    xexp = np.zeros((BLK, SPLIT), np.float32)
    for head in range(N_GROUPS64):
        hp, e = divmod(head, 2)
        ehot[hp, e * 2 * BLK:(e + 1) * 2 * BLK, head * lanes_per_head:(head + 1) * lanes_per_head] = 1.0
        xexp[head * lanes_per_head, head * HEAD_DIM:(head + 1) * HEAD_DIM] = 1.0
    return jnp.asarray(ehot, BF16), jnp.asarray(xexp, BF16)


def _dsw_attention_unshifted(q, k, v):
    ehot, xexp = _dsw_constants()
    state = None
    n_pat = len(DSW_PATTERNS)
    for idx, (_, dil) in enumerate(DSW_PATTERNS):
        state = _dsw_unshifted_pattern(q, k, v, dil, state, idx == n_pat - 1, ehot, xexp)
    return state


def _dsw_attention_online(q, k, v):
    state = None
    for _, dil in DSW_PATTERNS:
        state = _dsw_pattern(q, k, v, dil, state)
    return state[0]


def _dsw_attention(q, k, v, score_bound):
    return lax.cond(score_bound <= DSW_UNSHIFTED_SCORE_BOUND, _dsw_attention_unshifted, _dsw_attention_online,
                    q, k, v)


def _diff_finalize(o0, o1, lam_ref, sg_ref, o_ref):
    lam_p = lam_ref[...]
    lam = (jnp.exp(jnp.sum(lam_p[0:1] * lam_p[1:2], axis=-1, keepdims=True))
           - jnp.exp(jnp.sum(lam_p[2:3] * lam_p[3:4], axis=-1, keepdims=True)) + LAM_INIT)
    od = o0 - lam * o1
    ms = jnp.mean(od * od, axis=0, keepdims=True)
    y = od * lax.rsqrt(ms + RMS_EPS) * sg_ref[...] * (1.0 - LAM_INIT)
    o_ref[...] = y.T


def _causal_keep(i, k0, tq, tk):
    kpos = k0 + lax.broadcasted_iota(jnp.int32, (tk, tq), 0)
    qpos = i * tq + lax.broadcasted_iota(jnp.int32, (tk, tq), 1)
    return kpos <= qpos


def _diff_unshifted_kernel(qt_ref, k_ref, vt_ref, lam_ref, sg_ref, o_ref, l_sc, acc_sc, *, tq, tk):
    i = pl.program_id(1)
    qt = qt_ref[...]
    l_sc[...] = jnp.zeros(l_sc.shape, F32)
    acc_sc[...] = jnp.zeros(acc_sc.shape, F32)

    def full_chunk(j, carry):
        k0 = pl.multiple_of(j * tk, tk)
        vt = vt_ref[:, pl.ds(k0, tk)]
        for c in range(2):
            kc = k_ref[c, pl.ds(k0, tk), :]
            st = jnp.dot(kc, qt[c * HEAD_DIM:(c + 1) * HEAD_DIM, :], preferred_element_type=F32)
            p = jnp.exp2(st)
            l_sc[c] += jnp.sum(p.reshape(tk // 8, 8, tq), axis=0)
            acc_sc[c] += jnp.dot(vt, p.astype(BF16), preferred_element_type=F32)
        return carry

    lax.fori_loop(0, i, full_chunk, 0)

    sb = DIFF_DIAG_BLOCK
    k0 = pl.multiple_of(i * tk, tk)
    tri = lax.broadcasted_iota(jnp.int32, (sb, sb), 0) <= lax.broadcasted_iota(jnp.int32, (sb, sb), 1)
    for qb in range(tq // sb):
        nk = (qb + 1) * sb
        lanes = slice(qb * sb, (qb + 1) * sb)
        vt = vt_ref[:, pl.ds(k0, nk)]
        for c in range(2):
            kc = k_ref[c, pl.ds(k0, nk), :]
            st = jnp.dot(kc, qt[c * HEAD_DIM:(c + 1) * HEAD_DIM, lanes], preferred_element_type=F32)
            p = jnp.exp2(st)
            p_tri = jnp.where(tri, p[qb * sb:, :], 0.0)
            p = jnp.concatenate([p[:qb * sb, :], p_tri], axis=0) if qb else p_tri
            l_sc[c, :, lanes] += jnp.sum(p.reshape(nk // 8, 8, sb), axis=0)
            acc_sc[c, :, lanes] += jnp.dot(vt, p.astype(BF16), preferred_element_type=F32)

    l0 = jnp.sum(l_sc[0], axis=0, keepdims=True)
    l1 = jnp.sum(l_sc[1], axis=0, keepdims=True)
    _diff_finalize(acc_sc[0] / l0, acc_sc[1] / l1, lam_ref, sg_ref, o_ref)


def _diff_kernel(qt_ref, k_ref, vt_ref, lam_ref, sg_ref, o_ref, m_sc, l_sc, acc_sc, *, tq, tk):
    i = pl.program_id(1)
    qt = qt_ref[...]
    m_sc[...] = jnp.full(m_sc.shape, NEG_INF, F32)
    l_sc[...] = jnp.zeros(l_sc.shape, F32)
    acc_sc[...] = jnp.zeros(acc_sc.shape, F32)

    def chunk(j, masked):
        k0 = pl.multiple_of(j * tk, tk)
        vt = vt_ref[:, pl.ds(k0, tk)]
        for c in range(2):
            kc = k_ref[c, pl.ds(k0, tk), :]
            st = jnp.dot(kc, qt[c * HEAD_DIM:(c + 1) * HEAD_DIM, :], preferred_element_type=F32)
            if masked:
                st = jnp.where(_causal_keep(i, k0, tq, tk), st, NEG_INF)
            m_old = m_sc[c]
            m_new = jnp.maximum(m_old, jnp.max(st, axis=0, keepdims=True))
            alpha = jnp.exp2(m_old - m_new)
            p = jnp.exp2(st - m_new)
            l_sc[c] = alpha * l_sc[c] + jnp.sum(p, axis=0, keepdims=True)
            acc_sc[c] = alpha * acc_sc[c] + jnp.dot(vt, p.astype(BF16), preferred_element_type=F32)
            m_sc[c] = m_new

    n_full = (i * tq) // tk

    def full_chunk(j, carry):
        chunk(j, False)
        return carry

    lax.fori_loop(0, n_full, full_chunk, 0)
    for jj in range(tq // tk):
        chunk(n_full + jj, True)

    _diff_finalize(acc_sc[0] / l_sc[0], acc_sc[1] / l_sc[1], lam_ref, sg_ref, o_ref)


def _diff_call(body, scratch, name, qd_t, kd_tok, vd_t, lam_params, subln_col):
    s = qd_t.shape[1]
    tq, tk = DIFF_TQ, DIFF_TK
    assert tq == tk and s % tq == 0 and tq % DIFF_DIAG_BLOCK == 0
    return pl.pallas_call(
        functools.partial(body, tq=tq, tk=tk),
        grid=(DIFF_HEADS, s // tq),
        in_specs=[
            pl.BlockSpec((DIFF_V_DIM, tq), lambda h, i: (h, i)),
            pl.BlockSpec((2, s, HEAD_DIM), lambda h, i: (h, 0, 0)),
            pl.BlockSpec((DIFF_V_DIM, s), lambda h, i: (h, 0)),
            _const_spec(lam_params.shape),
            _const_spec(subln_col.shape),
        ],
        out_specs=pl.BlockSpec((tq, DIFF_V_DIM), lambda h, i: (i, h)),
        out_shape=jax.ShapeDtypeStruct((s, DIFF_HEADS * DIFF_V_DIM), F32),
        scratch_shapes=scratch(tq),
        compiler_params=_params("parallel", "arbitrary"),
        name=name,
    )(qd_t, kd_tok, vd_t, lam_params, subln_col)


def _diff_attention(qd_t, kd_tok, vd_t, lam_params, subln_col, score_bound):
    acc = lambda tq: pltpu.VMEM((2, DIFF_V_DIM, tq), F32)
    unshifted = functools.partial(
        _diff_call, _diff_unshifted_kernel, lambda tq: [pltpu.VMEM((2, 8, tq), F32), acc(tq)],
        "diff_attention_unshifted")
    online = functools.partial(
        _diff_call, _diff_kernel, lambda tq: [pltpu.VMEM((2, 1, tq), F32), pltpu.VMEM((2, 1, tq), F32), acc(tq)],
        "diff_attention_online")
    return lax.cond(score_bound <= DIFF_UNSHIFTED_SCORE_BOUND, unshifted, online,
                    qd_t, kd_tok, vd_t, lam_params, subln_col)


def _att_out_kernel(x_ref, oa_ref, od_ref, g_ref, w_ref, o_ref):
    g = g_ref[...]
    y = jnp.concatenate([oa_ref[...], od_ref[...]], axis=-1) * (g * jax.nn.sigmoid(g))
    o_ref[...] = x_ref[...] + jnp.dot(y.astype(BF16), w_ref[...], preferred_element_type=F32)


def _att_out(x2d, oa, od, g, w_out):
    s, d = x2d.shape
    tm = ROW_TILE
    row = lambda width: pl.BlockSpec((tm, width), lambda i: (i, 0))
    return pl.pallas_call(
        _att_out_kernel,
        grid=(s // tm,),
        in_specs=[row(d), row(SPLIT), row(SPLIT), row(2 * SPLIT), _const_spec(w_out.shape)],
        out_specs=row(d),
        out_shape=jax.ShapeDtypeStruct((s, d), F32),
        compiler_params=_params("parallel"),
        name="att_out_proj",
    )(x2d, oa, od, g, w_out.astype(BF16))


def _gelu(x):
    return 0.5 * x * (1.0 + lax.erf(x * (2.0 ** -0.5)))


def _sgu_kernel(x_ref, ng_ref, win_ref, lng_ref, lnb_ref, ws_ref, bs_ref, wout_ref, o_ref, *, width):
    x = x_ref[...]
    tm = x.shape[0]
    ms = jnp.mean(x * x, axis=-1, keepdims=True)
    h = (x * lax.rsqrt(ms + RMS_EPS) * ng_ref[...]).astype(BF16)
    v = _gelu(jnp.dot(h, win_ref[:, width:2 * width], preferred_element_type=F32))
    mu = jnp.mean(v, axis=-1, keepdims=True)
    vc = v - mu
    vn = vc * lax.rsqrt(jnp.mean(vc * vc, axis=-1, keepdims=True) + LN_EPS)
    vn = (vn * lng_ref[...] + lnb_ref[...]).astype(BF16)
    gw = width // SGU_GROUPS
    row = lax.broadcasted_iota(jnp.int32, (SGU_CHUNK, SGU_CHUNK), 0)
    col = lax.broadcasted_iota(jnp.int32, (SGU_CHUNK, SGU_CHUNK), 1)
    causal = col <= row
    sp_groups = []
    for grp in range(SGU_GROUPS):
        ws = jnp.where(causal, ws_ref[grp], 0.0).astype(BF16)
        bias = bs_ref[grp]
        chunks = []
        for c in range(tm // SGU_CHUNK):
            vg = vn[c * SGU_CHUNK:(c + 1) * SGU_CHUNK, grp * gw:(grp + 1) * gw]
            chunks.append(jnp.dot(ws, vg, preferred_element_type=F32) + bias)
        sp_groups.append(jnp.concatenate(chunks, axis=0))
    sp = jnp.concatenate(sp_groups, axis=1)
    u = _gelu(jnp.dot(h, win_ref[:, 0:width], preferred_element_type=F32))
    g = jnp.dot(h, win_ref[:, 2 * width:3 * width], preferred_element_type=F32)
    y = u * sp * (g * jax.nn.sigmoid(g))
    o_ref[...] = x + jnp.dot(y.astype(BF16), wout_ref[...], preferred_element_type=F32)


def _sgu_layer(x2d, norm_g, w_in, ln_g, ln_b, w_s, b_s, w_out):
    s, d = x2d.shape
    width = w_out.shape[0]
    tm = ROW_TILE
    row = pl.BlockSpec((tm, d), lambda i: (i, 0))
    single = lambda shape: pl.BlockSpec(shape, lambda *_: (0,) * len(shape), pipeline_mode=pl.Buffered(1))
    return pl.pallas_call(
        functools.partial(_sgu_kernel, width=width),
        grid=(s // tm,),
        in_specs=[row, _const_spec((1, d)), single(w_in.shape), _const_spec((1, width)), _const_spec((1, width)),
                  _const_spec(w_s.shape), _const_spec((SGU_GROUPS, SGU_CHUNK, 1)), single(w_out.shape)],
        out_specs=row,
        out_shape=jax.ShapeDtypeStruct((s, d), F32),
        compiler_params=_params("parallel"),
        name="sgu_layer",
    )(x2d, norm_g.reshape(1, d), w_in.astype(BF16), ln_g.reshape(1, width), ln_b.reshape(1, width),
      w_s, b_s.reshape(SGU_GROUPS, SGU_CHUNK, 1), w_out.astype(BF16))


def _rope_tables_fm(seq_len):
    pos = jnp.arange(seq_len, dtype=F32)
    inv = 1.0 / (ROPE_THETA ** (jnp.arange(0, 2 * ROT_HALF, 2, dtype=F32) / (2 * ROT_HALF)))
    ang = inv[:, None] * pos[None, :]
    return jnp.cos(ang), jnp.sin(ang)


def kernel(x, att_norm, att_w_in, dsw_q_norm, dsw_k_norm, diff_q_norm, diff_k_norm, diff_lam_q1, diff_lam_k1,
           diff_lam_q2, diff_lam_k2, diff_subln, att_w_out, sgu_norm, sgu_w_in, sgu_ln_g, sgu_ln_b, sgu_w_s,
           sgu_b_s, sgu_w_out):
    b, s, d = x.shape
    assert b == 1 and s % (BLK * DSW_PATTERNS[-1][1]) == 0
    x2d = x.reshape(s, d)
    cos_t, sin_t = _rope_tables_fm(s)

    gains = jnp.stack([dsw_q_norm[0], dsw_k_norm[0], diff_q_norm[0], diff_k_norm[0]])
    gains = jnp.broadcast_to(gains[:, None, :, None], (4, N_GROUPS64, HEAD_DIM, 1))
    va, g, qa_t, ka_t, qd_t, kd_t, vd_t = _att_in(x2d, att_norm[0], att_w_in[0], gains, cos_t, sin_t)
    bound = lambda gq, gk: 1.02 * LOG2_E * HEAD_DIM ** 0.5 * jnp.max(jnp.abs(gq)) * jnp.max(jnp.abs(gk))
    oa = _dsw_attention(qa_t.T, ka_t.T, va, bound(dsw_q_norm[0], dsw_k_norm[0]))
    kd_tok = kd_t.reshape(2 * DIFF_HEADS, HEAD_DIM, s).transpose(0, 2, 1)
    lam_params = jnp.stack([diff_lam_q1[0], diff_lam_k1[0], diff_lam_q2[0], diff_lam_k2[0]])
    od = _diff_attention(qd_t, kd_tok, vd_t, lam_params, diff_subln[0].reshape(DIFF_V_DIM, 1),
                         bound(diff_q_norm[0], diff_k_norm[0]))
    x2d = _att_out(x2d, oa, od, g, att_w_out[0])

    x2d = _sgu_layer(x2d, sgu_norm[0], sgu_w_in[0], sgu_ln_g[0], sgu_ln_b[0], sgu_w_s[0], sgu_b_s[0], sgu_w_out[0])
    return x2d.reshape(b, s, d)
```

```python
import functools
import math

import numpy as np
import jax
import jax.numpy as jnp
from jax import lax
from jax.experimental import pallas as pl
from jax.experimental.pallas import tpu as pltpu

F32 = jnp.float32
BF16 = jnp.bfloat16

LANES = 128
HEAD_DIM = 64
ROT_HALF = 8
ROPE_THETA = 500000.0
BLK = 128
N_GROUPS64 = 8
SPLIT = N_GROUPS64 * HEAD_DIM
DSW_PATTERNS = ((128, 1), (512, 4), (2048, 16))
DIFF_HEADS = 4
DIFF_V_DIM = 128
SGU_GROUPS = 8
SGU_CHUNK = 128
RMS_EPS = 1e-6
LN_EPS = 1e-5
NEG_INF = -1e30
LAM_INIT = 0.8 - 0.6 * math.exp(-0.3 * 0)
LOG2_E = math.log2(math.e)
DIFF_UNSHIFTED_SCORE_BOUND = 96.0
DSW_UNSHIFTED_SCORE_BOUND = 96.0

V7X_VMEM_LIMIT_BYTES = 56 * 1024 * 1024

ROW_TILE = 512
DSW_BLOCKS_PER_STEP = 8
DIFF_TQ = 1024
DIFF_TK = 1024
DIFF_DIAG_BLOCK = 512


def _params(*sem):
    return pltpu.CompilerParams(dimension_semantics=sem, vmem_limit_bytes=V7X_VMEM_LIMIT_BYTES)


def _const_spec(shape):
    nd = len(shape)
    return pl.BlockSpec(shape, lambda *_: (0,) * nd)


def _norm_rope_fm(p, gain, cos, sin, scale):
    t = p.shape[1]
    p3 = p.reshape(N_GROUPS64, HEAD_DIM, t)
    ms = jnp.mean(p3 * p3, axis=1, keepdims=True)
    y = p3 * lax.rsqrt(ms + RMS_EPS) * gain
    x1 = y[:, 0:ROT_HALF, :]
    x2 = y[:, ROT_HALF:2 * ROT_HALF, :]
    out = jnp.concatenate([x1 * cos - x2 * sin, x2 * cos + x1 * sin, y[:, 2 * ROT_HALF:, :]], axis=1)
    if scale != 1.0:
        out = out * scale
    return out.reshape(SPLIT, t)


def _emit_pattern_views(tok, tok_sc, view_refs):
    t = tok.shape[0]
    n_lane_tiles = SPLIT // LANES
    for c in range(n_lane_tiles):
        tok_sc[c] = tok[:, c * LANES:(c + 1) * LANES]
    for (_, dil), ref in zip(DSW_PATTERNS, view_refs):
        if dil == 1:
            ref[...] = tok.astype(BF16)
            continue
        for r in range(dil):
            rows = jnp.concatenate([tok_sc[c, pl.ds(r, t // dil, stride=dil), :] for c in range(n_lane_tiles)],
                                   axis=1)
            ref[0, :, r * SPLIT:(r + 1) * SPLIT] = rows.astype(BF16)


def _att_in_kernel(x_ref, ng_ref, wtok_ref, wfm_ref, gains_ref, cos_ref, sin_ref, *refs):
    n_pat = len(DSW_PATTERNS)
    qa_views, ka_views, va_views = refs[0:n_pat], refs[n_pat:2 * n_pat], refs[2 * n_pat:3 * n_pat]
    g_ref, kd_ref, qd_ref, vd_ref, tok_sc = refs[3 * n_pat:]
    x = x_ref[...]
    ms = jnp.mean(x * x, axis=-1, keepdims=True)
    h = (x * lax.rsqrt(ms + RMS_EPS) * ng_ref[...]).astype(BF16)
    tok = jnp.dot(h, wtok_ref[...], preferred_element_type=F32)
    _emit_pattern_views(tok[:, :SPLIT], tok_sc, va_views)
    g_ref[...] = tok[:, SPLIT:]
    cos = cos_ref[...][None]
    sin = sin_ref[...][None]
    scales = (HEAD_DIM ** -0.5 * LOG2_E, 1.0, HEAD_DIM ** -0.5 * LOG2_E, 1.0)
    normed = []
    for idx in range(4):
        p = lax.dot_general(wfm_ref[idx * SPLIT:(idx + 1) * SPLIT, :], h, (((1,), (1,)), ((), ())),
                            preferred_element_type=F32)
        normed.append(_norm_rope_fm(p, gains_ref[idx], cos, sin, scales[idx]))
    _emit_pattern_views(normed[0].T, tok_sc, qa_views)
    _emit_pattern_views(normed[1].T, tok_sc, ka_views)
    qd_ref[...] = normed[2].astype(BF16)
    kd_ref[...] = normed[3].T.astype(BF16)
    p = lax.dot_general(wfm_ref[4 * SPLIT:5 * SPLIT, :], h, (((1,), (1,)), ((), ())),
                        preferred_element_type=F32)
    vd_ref[...] = p.astype(BF16)


def _att_in(x2d, norm_g, w_in, gains, cos_t, sin_t):
    s, d = x2d.shape
    tm = ROW_TILE
    w = w_in.astype(BF16)
    w_tok = jnp.concatenate([w[:, 2 * SPLIT:4 * SPLIT], w[:, 7 * SPLIT:8 * SPLIT]], axis=1)
    w_fm = jnp.concatenate([w[:, 0:2 * SPLIT], w[:, 4 * SPLIT:7 * SPLIT]], axis=1).T
    view_shapes, view_specs = [], []
    for _, dil in DSW_PATTERNS:
        if dil == 1:
            view_shapes.append(jax.ShapeDtypeStruct((s, SPLIT), BF16))
            view_specs.append(pl.BlockSpec((tm, SPLIT), lambda i: (i, 0)))
            continue
        rows = tm // dil
        assert tm % dil == 0 and BLK % rows == 0 and rows % 16 == 0
        per_blk = BLK // rows
        view_shapes.append(jax.ShapeDtypeStruct((s // (BLK * dil), BLK, dil * SPLIT), BF16))
        view_specs.append(pl.BlockSpec((1, rows, dil * SPLIT),
                                       functools.partial(lambda i, per_blk: (i // per_blk, i % per_blk, 0),
                                                         per_blk=per_blk)))
    fm = jax.ShapeDtypeStruct((SPLIT, s), BF16)
    fm_spec = pl.BlockSpec((SPLIT, tm), lambda i: (0, i))
    outs = pl.pallas_call(
        _att_in_kernel,
        grid=(s // tm,),
        in_specs=[
            pl.BlockSpec((tm, d), lambda i: (i, 0)),
            _const_spec((1, d)),
            _const_spec(w_tok.shape),
            _const_spec(w_fm.shape),
            _const_spec(gains.shape),
            pl.BlockSpec((ROT_HALF, tm), lambda i: (0, i)),
            pl.BlockSpec((ROT_HALF, tm), lambda i: (0, i)),
        ],
        out_specs=view_specs * 3 + [
            pl.BlockSpec((tm, 2 * SPLIT), lambda i: (i, 0)),
            pl.BlockSpec((tm, SPLIT), lambda i: (i, 0)),
            fm_spec, fm_spec,
        ],
        out_shape=view_shapes * 3 + [
            jax.ShapeDtypeStruct((s, 2 * SPLIT), F32),
            jax.ShapeDtypeStruct((s, SPLIT), BF16),
            fm, fm,
        ],
        scratch_shapes=[pltpu.VMEM((SPLIT // LANES, tm, LANES), F32)],
        compiler_params=_params("parallel"),
        name="att_in_proj",
    )(x2d, norm_g.reshape(1, d), w_tok, w_fm, gains, cos_t, sin_t)
    n_pat = len(DSW_PATTERNS)
    return (outs[0:n_pat], outs[n_pat:2 * n_pat], outs[2 * n_pat:3 * n_pat]) + tuple(outs[3 * n_pat:])


def _dsw_kernel(*refs, has_state, blocks_per_step):
    if has_state:
        (q_ref, kprev_ref, kcur_ref, vprev_ref, vcur_ref, oin_ref, lsein_ref,
         o_ref, lse_ref, kall, vall) = refs
    else:
        (q_ref, kprev_ref, kcur_ref, vprev_ref, vcur_ref, o_ref, lse_ref, kall, vall) = refs
        oin_ref = lsein_ref = None
    g_blocks = blocks_per_step
    step = pl.program_id(1)
    kall[0:BLK, :] = kprev_ref[0]
    vall[0:BLK, :] = vprev_ref[0]
    kall[BLK:, :] = kcur_ref[...].reshape(g_blocks * BLK, SPLIT)
    vall[BLK:, :] = vcur_ref[...].reshape(g_blocks * BLK, SPLIT)

    qi = lax.broadcasted_iota(jnp.int32, (BLK, 2 * BLK), 0) + BLK
    kj = lax.broadcasted_iota(jnp.int32, (BLK, 2 * BLK), 1)
    dist = qi - kj
    band = (dist >= 0) & (dist <= BLK)
    lane = lax.broadcasted_iota(jnp.int32, (BLK, 2 * HEAD_DIM), 1)
    low_half = lane < HEAD_DIM

    def block(g, carry):
        first_key = jnp.where((step * g_blocks + g) == 0, BLK, 0)
        mask = band & (kj >= first_key)
        row0 = pl.multiple_of(g * BLK, BLK)
        lse_tile = jnp.zeros((BLK, BLK), F32)
        lse_in = lsein_ref[g] if has_state else None
        for hp in range(N_GROUPS64 // 2):
            cols = slice(hp * 2 * HEAD_DIM, (hp + 1) * 2 * HEAD_DIM)
            q_pair = q_ref[g, :, cols]
            kk = kall[pl.ds(row0, 2 * BLK), cols]
            vv = vall[pl.ds(row0, 2 * BLK), cols]
            o_in = oin_ref[g, :, cols] if has_state else None
            halves = []
            for e in range(2):
                head = 2 * hp + e
                sel = low_half if e == 0 else jnp.logical_not(low_half)
                qm = jnp.where(sel, q_pair, jnp.zeros_like(q_pair))
                sc = lax.dot_general(qm, kk, (((1,), (1,)), ((), ())), preferred_element_type=F32)
                sc = jnp.where(mask, sc, NEG_INF)
                m_blk = jnp.max(sc, axis=-1, keepdims=True)
                if has_state:
                    m_old = lse_in[:, head:head + 1]
                    m_new = jnp.maximum(m_old, m_blk)
                    alpha = jnp.exp2(m_old - m_new)
                else:
                    m_new = m_blk
                p = jnp.exp2(sc - m_new)
                l = jnp.sum(p, axis=-1, keepdims=True)
                pv = jnp.dot(p.astype(BF16), vv, preferred_element_type=F32)
                if has_state:
                    l = l + alpha
                    pv = pv + alpha * o_in
                halves.append(pv / l)
                lse_tile = jnp.where(lane == head, m_new + jnp.log2(l), lse_tile)
            o_ref[g, :, cols] = jnp.where(low_half, halves[0], halves[1])
        lse_ref[g] = lse_tile
        return carry

    lax.fori_loop(0, g_blocks, block, 0)


def _dsw_pattern(q, k, v, dil, state):
    s = q.size // SPLIT
    nb = s // (BLK * dil)
    g_blocks = min(DSW_BLOCKS_PER_STEP, nb)
    view = lambda a, width: a.reshape(nb, BLK, dil * width)
    cur_spec = lambda width: pl.BlockSpec((g_blocks, BLK, width), lambda r, j: (j, 0, r))
    prev_spec = pl.BlockSpec((1, BLK, SPLIT), lambda r, j: (jnp.maximum(j * g_blocks - 1, 0), 0, r))
    operands = [view(q, SPLIT), view(k, SPLIT), view(k, SPLIT), view(v, SPLIT), view(v, SPLIT)]
    in_specs = [cur_spec(SPLIT), prev_spec, cur_spec(SPLIT), prev_spec, cur_spec(SPLIT)]
    if state is not None:
        operands += [view(state[0], SPLIT), view(state[1], BLK)]
        in_specs += [cur_spec(SPLIT), cur_spec(BLK)]
    o, lse = pl.pallas_call(
        functools.partial(_dsw_kernel, has_state=state is not None, blocks_per_step=g_blocks),
        grid=(dil, nb // g_blocks),
        in_specs=in_specs,
        out_specs=[cur_spec(SPLIT), cur_spec(BLK)],
        out_shape=[jax.ShapeDtypeStruct((nb, BLK, dil * SPLIT), F32),
                   jax.ShapeDtypeStruct((nb, BLK, dil * BLK), F32)],
        scratch_shapes=[pltpu.VMEM(((g_blocks + 1) * BLK, SPLIT), BF16),
                        pltpu.VMEM(((g_blocks + 1) * BLK, SPLIT), BF16)],
        compiler_params=_params("parallel", "arbitrary"),
        name=f"dsw_attention_dil{dil}",
    )(*operands)
    return o.reshape(s, SPLIT), lse.reshape(s, BLK)


def _dsw_unshifted_kernel(*refs, has_state, last, blocks_per_step):
    refs = list(refs)
    q_ref, kprev_ref, kcur_ref, vprev_ref, vcur_ref, ehot_ref = refs[:6]
    pos = 6
    if last:
        xexp_ref = refs[pos]
        pos += 1
    if has_state:
        accin_ref, lin_ref = refs[pos:pos + 2]
        pos += 2
    if last:
        o_ref = refs[pos]
        pos += 1
    else:
        acc_ref, l_ref = refs[pos:pos + 2]
        pos += 2
    kall, vall = refs[pos:pos + 2]
    g_blocks = blocks_per_step
    step = pl.program_id(1)
    kall[0:BLK, :] = kprev_ref[0]
    vall[0:BLK, :] = vprev_ref[0]
    kall[BLK:, :] = kcur_ref[...].reshape(g_blocks * BLK, SPLIT)
    vall[BLK:, :] = vcur_ref[...].reshape(g_blocks * BLK, SPLIT)

    qi = (lax.broadcasted_iota(jnp.int32, (2 * BLK, 2 * BLK), 0) & (BLK - 1)) + BLK
    kj = lax.broadcasted_iota(jnp.int32, (2 * BLK, 2 * BLK), 1)
    dist = qi - kj
    band = (dist >= 0) & (dist <= BLK)
    q_low = lax.broadcasted_iota(jnp.int32, (BLK, 2 * HEAD_DIM), 1) < HEAD_DIM
    v_low = lax.broadcasted_iota(jnp.int32, (2 * BLK, 2 * HEAD_DIM), 1) < HEAD_DIM

    def block(g, carry):
        first_key = jnp.where((step * g_blocks + g) == 0, BLK, 0)
        mask = band & (kj >= first_key)
        row0 = pl.multiple_of(g * BLK, BLK)
        l_tile = lin_ref[g] if has_state else jnp.zeros((BLK, BLK), F32)
        accs = []
        for hp in range(N_GROUPS64 // 2):
            cols = slice(hp * 2 * HEAD_DIM, (hp + 1) * 2 * HEAD_DIM)
            q_pair = q_ref[g, :, cols]
            kk = kall[pl.ds(row0, 2 * BLK), cols]
            vv = vall[pl.ds(row0, 2 * BLK), cols]
            qz = jnp.zeros_like(q_pair)
            vz = jnp.zeros_like(vv)
            q2 = jnp.concatenate([jnp.where(q_low, q_pair, qz), jnp.where(q_low, qz, q_pair)], axis=0)
            sc = lax.dot_general(q2, kk, (((1,), (1,)), ((), ())), preferred_element_type=F32)
            p2 = jnp.where(mask, jnp.exp2(sc), 0.0).astype(BF16)
            p_cat = jnp.concatenate([p2[:BLK], p2[BLK:]], axis=1)
            v_split = jnp.concatenate([jnp.where(v_low, vv, vz), jnp.where(v_low, vz, vv)], axis=0)
            rhs = jnp.concatenate([v_split, ehot_ref[hp]], axis=1)
            out2 = jnp.dot(p_cat, rhs, preferred_element_type=F32)
            acc = out2[:, :2 * HEAD_DIM]
            if has_state:
                acc = acc + accin_ref[g, :, cols]
            l_tile = l_tile + out2[:, 2 * HEAD_DIM:]
            if last:
                accs.append(acc)
            else:
                acc_ref[g, :, cols] = acc
        if last:
            r = 1.0 / l_tile
            r_hi = r.astype(BF16)
            r_lo = (r - r_hi.astype(F32)).astype(BF16)
            r_exp = (jnp.dot(r_hi, xexp_ref[...], preferred_element_type=F32)
                     + jnp.dot(r_lo, xexp_ref[...], preferred_element_type=F32))
            for hp in range(N_GROUPS64 // 2):
                cols = slice(hp * 2 * HEAD_DIM, (hp + 1) * 2 * HEAD_DIM)
                o_ref[g, :, cols] = accs[hp] * r_exp[:, cols]
        else:
            l_ref[g] = l_tile
        return carry

    lax.fori_loop(0, g_blocks, block, 0)


def _dsw_unshifted_pattern(q, k, v, dil, state, last, ehot, xexp):
    s = q.size // SPLIT
    nb = s // (BLK * dil)
    g_blocks = min(DSW_BLOCKS_PER_STEP, nb)
    view = lambda a, width: a.reshape(nb, BLK, dil * width)
    cur_spec = lambda width: pl.BlockSpec((g_blocks, BLK, width), lambda r, j: (j, 0, r))
    prev_spec = pl.BlockSpec((1, BLK, SPLIT), lambda r, j: (jnp.maximum(j * g_blocks - 1, 0), 0, r))
    operands = [view(q, SPLIT), view(k, SPLIT), view(k, SPLIT), view(v, SPLIT), view(v, SPLIT), ehot]
    in_specs = [cur_spec(SPLIT), prev_spec, cur_spec(SPLIT), prev_spec, cur_spec(SPLIT), _const_spec(ehot.shape)]
    if last:
        operands.append(xexp)
        in_specs.append(_const_spec(xexp.shape))
    if state is not None:
        operands += [view(state[0], SPLIT), view(state[1], BLK)]
        in_specs += [cur_spec(SPLIT), cur_spec(BLK)]
    acc_shape = jax.ShapeDtypeStruct((nb, BLK, dil * SPLIT), F32)
    l_shape = jax.ShapeDtypeStruct((nb, BLK, dil * BLK), F32)
    out = pl.pallas_call(
        functools.partial(_dsw_unshifted_kernel, has_state=state is not None, last=last, blocks_per_step=g_blocks),
        grid=(dil, nb // g_blocks),
        in_specs=in_specs,
        out_specs=[cur_spec(SPLIT)] if last else [cur_spec(SPLIT), cur_spec(BLK)],
        out_shape=[acc_shape] if last else [acc_shape, l_shape],
        scratch_shapes=[pltpu.VMEM(((g_blocks + 1) * BLK, SPLIT), BF16),
                        pltpu.VMEM(((g_blocks + 1) * BLK, SPLIT), BF16)],
        compiler_params=_params("parallel", "arbitrary"),
        name=f"dsw_unshifted_dil{dil}",
    )(*operands)
    if last:
        return out[0].reshape(s, SPLIT)
    return out[0].reshape(s, SPLIT), out[1].reshape(s, BLK)


def _dsw_constants():
    lanes_per_head = BLK // N_GROUPS64
    ehot = np.zeros((N_GROUPS64 // 2, 4 * BLK, BLK), np.float32)
    xexp = np.zeros((BLK, SPLIT), np.float32)
    for head in range(N_GROUPS64):
        hp, e = divmod(head, 2)
        ehot[hp, e * 2 * BLK:(e + 1) * 2 * BLK, head * lanes_per_head:(head + 1) * lanes_per_head] = 1.0
        xexp[head * lanes_per_head, head * HEAD_DIM:(head + 1) * HEAD_DIM] = 1.0
    return jnp.asarray(ehot, BF16), jnp.asarray(xexp, BF16)


def _dsw_attention_unshifted(q_views, k_views, v_views):
    ehot, xexp = _dsw_constants()
    state = None
    n_pat = len(DSW_PATTERNS)
    for idx, (_, dil) in enumerate(DSW_PATTERNS):
        state = _dsw_unshifted_pattern(q_views[idx], k_views[idx], v_views[idx], dil, state, idx == n_pat - 1,
                                       ehot, xexp)
    return state


def _dsw_attention_online(q_views, k_views, v_views):
    state = None
    for idx, (_, dil) in enumerate(DSW_PATTERNS):
        state = _dsw_pattern(q_views[idx], k_views[idx], v_views[idx], dil, state)
    return state[0]


def _dsw_attention(q_views, k_views, v_views, score_bound):
    return lax.cond(score_bound <= DSW_UNSHIFTED_SCORE_BOUND, _dsw_attention_unshifted, _dsw_attention_online,
                    q_views, k_views, v_views)


def _diff_finalize(o0, o1, lam_ref, sg_ref, o_ref):
    lam_p = lam_ref[...]
    lam = (jnp.exp(jnp.sum(lam_p[0:1] * lam_p[1:2], axis=-1, keepdims=True))
           - jnp.exp(jnp.sum(lam_p[2:3] * lam_p[3:4], axis=-1, keepdims=True)) + LAM_INIT)
    od = o0 - lam * o1
    ms = jnp.mean(od * od, axis=0, keepdims=True)
    y = od * lax.rsqrt(ms + RMS_EPS) * sg_ref[...] * (1.0 - LAM_INIT)
    o_ref[...] = y.T


def _component_queries(qt):
    zeros = jnp.zeros((HEAD_DIM, qt.shape[1]), qt.dtype)
    return (jnp.concatenate([qt[:HEAD_DIM], zeros], axis=0), jnp.concatenate([zeros, qt[HEAD_DIM:]], axis=0))


def _causal_keep(i, k0, tq, tk):
    kpos = k0 + lax.broadcasted_iota(jnp.int32, (tk, tq), 0)
    qpos = i * tq + lax.broadcasted_iota(jnp.int32, (tk, tq), 1)
    return kpos <= qpos


def _diff_unshifted_kernel(qt_ref, k_ref, vt_ref, lam_ref, sg_ref, o_ref, l_sc, acc_sc, *, tq, tk):
    i = pl.program_id(1)
    qc = _component_queries(qt_ref[...])
    l_sc[...] = jnp.zeros(l_sc.shape, F32)
    acc_sc[...] = jnp.zeros(acc_sc.shape, F32)

    def full_chunk(j, carry):
        k0 = pl.multiple_of(j * tk, tk)
        vt = vt_ref[:, pl.ds(k0, tk)]
        for c in range(2):
            st = jnp.dot(k_ref[pl.ds(k0, tk), :], qc[c], preferred_element_type=F32)
            p = jnp.exp2(st)
            l_sc[c] += jnp.sum(p.reshape(tk // 8, 8, tq), axis=0)
            acc_sc[c] += jnp.dot(vt, p.astype(BF16), preferred_element_type=F32)
        return carry

    lax.fori_loop(0, i, full_chunk, 0)

    sb = DIFF_DIAG_BLOCK
    k0 = pl.multiple_of(i * tk, tk)
    tri = lax.broadcasted_iota(jnp.int32, (sb, sb), 0) <= lax.broadcasted_iota(jnp.int32, (sb, sb), 1)
    for qb in range(tq // sb):
        nk = (qb + 1) * sb
        lanes = slice(qb * sb, (qb + 1) * sb)
        vt = vt_ref[:, pl.ds(k0, nk)]
        for c in range(2):
            st = jnp.dot(k_ref[pl.ds(k0, nk), :], qc[c][:, lanes], preferred_element_type=F32)
            p = jnp.exp2(st)
            p_tri = jnp.where(tri, p[qb * sb:, :], 0.0)
            p = jnp.concatenate([p[:qb * sb, :], p_tri], axis=0) if qb else p_tri
            l_sc[c, :, lanes] += jnp.sum(p.reshape(nk // 8, 8, sb), axis=0)
            acc_sc[c, :, lanes] += jnp.dot(vt, p.astype(BF16), preferred_element_type=F32)

    l0 = jnp.sum(l_sc[0], axis=0, keepdims=True)
    l1 = jnp.sum(l_sc[1], axis=0, keepdims=True)
    _diff_finalize(acc_sc[0] / l0, acc_sc[1] / l1, lam_ref, sg_ref, o_ref)


def _diff_kernel(qt_ref, k_ref, vt_ref, lam_ref, sg_ref, o_ref, m_sc, l_sc, acc_sc, *, tq, tk):
    i = pl.program_id(1)
    qc = _component_queries(qt_ref[...])
    m_sc[...] = jnp.full(m_sc.shape, NEG_INF, F32)
    l_sc[...] = jnp.zeros(l_sc.shape, F32)
    acc_sc[...] = jnp.zeros(acc_sc.shape, F32)

    def chunk(j, masked):
        k0 = pl.multiple_of(j * tk, tk)
        vt = vt_ref[:, pl.ds(k0, tk)]
        for c in range(2):
            st = jnp.dot(k_ref[pl.ds(k0, tk), :], qc[c], preferred_element_type=F32)
            if masked:
                st = jnp.where(_causal_keep(i, k0, tq, tk), st, NEG_INF)
            m_old = m_sc[c]
            m_new = jnp.maximum(m_old, jnp.max(st, axis=0, keepdims=True))
            alpha = jnp.exp2(m_old - m_new)
            p = jnp.exp2(st - m_new)
            l_sc[c] = alpha * l_sc[c] + jnp.sum(p, axis=0, keepdims=True)
            acc_sc[c] = alpha * acc_sc[c] + jnp.dot(vt, p.astype(BF16), preferred_element_type=F32)
            m_sc[c] = m_new

    n_full = (i * tq) // tk

    def full_chunk(j, carry):
        chunk(j, False)
        return carry

    lax.fori_loop(0, n_full, full_chunk, 0)
    for jj in range(tq // tk):
        chunk(n_full + jj, True)

    _diff_finalize(acc_sc[0] / l_sc[0], acc_sc[1] / l_sc[1], lam_ref, sg_ref, o_ref)


def _diff_call(body, scratch, name, qd_t, kd_tok, vd_t, lam_params, subln_col):
    s = qd_t.shape[1]
    tq, tk = DIFF_TQ, DIFF_TK
    assert tq == tk and s % tq == 0 and tq % DIFF_DIAG_BLOCK == 0
    return pl.pallas_call(
        functools.partial(body, tq=tq, tk=tk),
        grid=(DIFF_HEADS, s // tq),
        in_specs=[
            pl.BlockSpec((DIFF_V_DIM, tq), lambda h, i: (h, i)),
            pl.BlockSpec((s, 2 * HEAD_DIM), lambda h, i: (0, h)),
            pl.BlockSpec((DIFF_V_DIM, s), lambda h, i: (h, 0)),
            _const_spec(lam_params.shape),
            _const_spec(subln_col.shape),
        ],
        out_specs=pl.BlockSpec((tq, DIFF_V_DIM), lambda h, i: (i, h)),
        out_shape=jax.ShapeDtypeStruct((s, DIFF_HEADS * DIFF_V_DIM), F32),
        scratch_shapes=scratch(tq),
        compiler_params=_params("parallel", "arbitrary"),
        name=name,
    )(qd_t, kd_tok, vd_t, lam_params, subln_col)


def _diff_attention(qd_t, kd_tok, vd_t, lam_params, subln_col, score_bound):
    acc = lambda tq: pltpu.VMEM((2, DIFF_V_DIM, tq), F32)
    unshifted = functools.partial(
        _diff_call, _diff_unshifted_kernel, lambda tq: [pltpu.VMEM((2, 8, tq), F32), acc(tq)],
        "diff_attention_unshifted")
    online = functools.partial(
        _diff_call, _diff_kernel, lambda tq: [pltpu.VMEM((2, 1, tq), F32), pltpu.VMEM((2, 1, tq), F32), acc(tq)],
        "diff_attention_online")
    return lax.cond(score_bound <= DIFF_UNSHIFTED_SCORE_BOUND, unshifted, online,
                    qd_t, kd_tok, vd_t, lam_params, subln_col)


def _att_out_kernel(x_ref, oa_ref, od_ref, g_ref, w_ref, o_ref):
    g = g_ref[...]
    y = jnp.concatenate([oa_ref[...], od_ref[...]], axis=-1) * (g * jax.nn.sigmoid(g))
    o_ref[...] = x_ref[...] + jnp.dot(y.astype(BF16), w_ref[...], preferred_element_type=F32)


def _att_out(x2d, oa, od, g, w_out):
    s, d = x2d.shape
    tm = ROW_TILE
    row = lambda width: pl.BlockSpec((tm, width), lambda i: (i, 0))
    return pl.pallas_call(
        _att_out_kernel,
        grid=(s // tm,),
        in_specs=[row(d), row(SPLIT), row(SPLIT), row(2 * SPLIT), _const_spec(w_out.shape)],
        out_specs=row(d),
        out_shape=jax.ShapeDtypeStruct((s, d), F32),
        compiler_params=_params("parallel"),
        name="att_out_proj",
    )(x2d, oa, od, g, w_out.astype(BF16))


def _gelu(x):
    return 0.5 * x * (1.0 + lax.erf(x * (2.0 ** -0.5)))


def _sgu_kernel(x_ref, ng_ref, win_ref, lng_ref, lnb_ref, ws_ref, bs_ref, wout_ref, o_ref, *, width):
    x = x_ref[...]
    tm = x.shape[0]
    ms = jnp.mean(x * x, axis=-1, keepdims=True)
    h = (x * lax.rsqrt(ms + RMS_EPS) * ng_ref[...]).astype(BF16)
    v = _gelu(jnp.dot(h, win_ref[:, width:2 * width], preferred_element_type=F32))
    mu = jnp.mean(v, axis=-1, keepdims=True)
    vc = v - mu
    vn = vc * lax.rsqrt(jnp.mean(vc * vc, axis=-1, keepdims=True) + LN_EPS)
    vn = (vn * lng_ref[...] + lnb_ref[...]).astype(BF16)
    gw = width // SGU_GROUPS
    row = lax.broadcasted_iota(jnp.int32, (SGU_CHUNK, SGU_CHUNK), 0)
    col = lax.broadcasted_iota(jnp.int32, (SGU_CHUNK, SGU_CHUNK), 1)
    causal = col <= row
    sp_groups = []
    for grp in range(SGU_GROUPS):
        ws = jnp.where(causal, ws_ref[grp], 0.0).astype(BF16)
        bias = bs_ref[grp]
        chunks = []
        for c in range(tm // SGU_CHUNK):
            vg = vn[c * SGU_CHUNK:(c + 1) * SGU_CHUNK, grp * gw:(grp + 1) * gw]
            chunks.append(jnp.dot(ws, vg, preferred_element_type=F32) + bias)
        sp_groups.append(jnp.concatenate(chunks, axis=0))
    sp = jnp.concatenate(sp_groups, axis=1)
    u = _gelu(jnp.dot(h, win_ref[:, 0:width], preferred_element_type=F32))
    g = jnp.dot(h, win_ref[:, 2 * width:3 * width], preferred_element_type=F32)
    y = u * sp * (g * jax.nn.sigmoid(g))
    o_ref[...] = x + jnp.dot(y.astype(BF16), wout_ref[...], preferred_element_type=F32)


def _sgu_layer(x2d, norm_g, w_in, ln_g, ln_b, w_s, b_s, w_out):
    s, d = x2d.shape
    width = w_out.shape[0]
    tm = ROW_TILE
    row = pl.BlockSpec((tm, d), lambda i: (i, 0))
    single = lambda shape: pl.BlockSpec(shape, lambda *_: (0,) * len(shape), pipeline_mode=pl.Buffered(1))
    return pl.pallas_call(
        functools.partial(_sgu_kernel, width=width),
        grid=(s // tm,),
        in_specs=[row, _const_spec((1, d)), single(w_in.shape), _const_spec((1, width)), _const_spec((1, width)),
                  _const_spec(w_s.shape), _const_spec((SGU_GROUPS, SGU_CHUNK, 1)), single(w_out.shape)],
        out_specs=row,
        out_shape=jax.ShapeDtypeStruct((s, d), F32),
        compiler_params=_params("parallel"),
        name="sgu_layer",
    )(x2d, norm_g.reshape(1, d), w_in.astype(BF16), ln_g.reshape(1, width), ln_b.reshape(1, width),
      w_s, b_s.reshape(SGU_GROUPS, SGU_CHUNK, 1), w_out.astype(BF16))


def _rope_tables_fm(seq_len):
    pos = jnp.arange(seq_len, dtype=F32)
    inv = 1.0 / (ROPE_THETA ** (jnp.arange(0, 2 * ROT_HALF, 2, dtype=F32) / (2 * ROT_HALF)))
    ang = inv[:, None] * pos[None, :]
    return jnp.cos(ang), jnp.sin(ang)


def kernel(x, att_norm, att_w_in, dsw_q_norm, dsw_k_norm, diff_q_norm, diff_k_norm, diff_lam_q1, diff_lam_k1,
           diff_lam_q2, diff_lam_k2, diff_subln, att_w_out, sgu_norm, sgu_w_in, sgu_ln_g, sgu_ln_b, sgu_w_s,
           sgu_b_s, sgu_w_out):
    b, s, d = x.shape
    assert b == 1 and s % (BLK * DSW_PATTERNS[-1][1]) == 0
    x2d = x.reshape(s, d)
    cos_t, sin_t = _rope_tables_fm(s)

    gains = jnp.stack([dsw_q_norm[0], dsw_k_norm[0], diff_q_norm[0], diff_k_norm[0]])
    gains = jnp.broadcast_to(gains[:, None, :, None], (4, N_GROUPS64, HEAD_DIM, 1))
    qa_views, ka_views, va_views, g, kd_tok, qd_t, vd_t = _att_in(x2d, att_norm[0], att_w_in[0], gains, cos_t, sin_t)
    bound = lambda gq, gk: 1.02 * LOG2_E * HEAD_DIM ** 0.5 * jnp.max(jnp.abs(gq)) * jnp.max(jnp.abs(gk))
    oa = _dsw_attention(qa_views, ka_views, va_views, bound(dsw_q_norm[0], dsw_k_norm[0]))
    lam_params = jnp.stack([diff_lam_q1[0], diff_lam_k1[0], diff_lam_q2[0], diff_lam_k2[0]])
    od = _diff_attention(qd_t, kd_tok, vd_t, lam_params, diff_subln[0].reshape(DIFF_V_DIM, 1),
                         bound(diff_q_norm[0], diff_k_norm[0]))
    x2d = _att_out(x2d, oa, od, g, att_w_out[0])

    x2d = _sgu_layer(x2d, sgu_norm[0], sgu_w_in[0], sgu_ln_g[0], sgu_ln_b[0], sgu_w_s[0], sgu_b_s[0], sgu_w_out[0])
    return x2d.reshape(b, s, d)
```

```python
import functools
import math

import numpy as np
import jax
import jax.numpy as jnp
from jax import lax
from jax.experimental import pallas as pl
from jax.experimental.pallas import tpu as pltpu

F32 = jnp.float32
BF16 = jnp.bfloat16

LANES = 128
HEAD_DIM = 64
ROT_HALF = 8
ROPE_THETA = 500000.0
BLK = 128
N_GROUPS64 = 8
SPLIT = N_GROUPS64 * HEAD_DIM
DSW_PATTERNS = ((128, 1), (512, 4), (2048, 16))
DSW_DIL_RATIO = 4
DIFF_HEADS = 4
DIFF_V_DIM = 128
SGU_GROUPS = 8
SGU_CHUNK = 128
RMS_EPS = 1e-6
LN_EPS = 1e-5
NEG_INF = -1e30
LAM_INIT = 0.8 - 0.6 * math.exp(-0.3 * 0)
LOG2_E = math.log2(math.e)
DIFF_UNSHIFTED_SCORE_BOUND = 96.0
DSW_UNSHIFTED_SCORE_BOUND = 96.0

V7X_VMEM_LIMIT_BYTES = 56 * 1024 * 1024

ROW_TILE = 512
DSW_BLOCKS_PER_STEP = 8
DIFF_TQ = 1024
DIFF_TK = 1024
DIFF_DIAG_BLOCK = 512


def _params(*sem):
    return pltpu.CompilerParams(dimension_semantics=sem, vmem_limit_bytes=V7X_VMEM_LIMIT_BYTES)


def _const_spec(shape):
    nd = len(shape)
    return pl.BlockSpec(shape, lambda *_: (0,) * nd)


def _norm_rope_fm(p, gain, cos, sin, scale):
    t = p.shape[1]
    p3 = p.reshape(N_GROUPS64, HEAD_DIM, t)
    ms = jnp.mean(p3 * p3, axis=1, keepdims=True)
    y = p3 * lax.rsqrt(ms + RMS_EPS) * gain
    x1 = y[:, 0:ROT_HALF, :]
    x2 = y[:, ROT_HALF:2 * ROT_HALF, :]
    out = jnp.concatenate([x1 * cos - x2 * sin, x2 * cos + x1 * sin, y[:, 2 * ROT_HALF:, :]], axis=1)
    if scale != 1.0:
        out = out * scale
    return out.reshape(SPLIT, t)


def _emit_pattern_views(tok, tok_sc, view_refs):
    t = tok.shape[0]
    n_lane_tiles = SPLIT // LANES
    for c in range(n_lane_tiles):
        tok_sc[c] = tok[:, c * LANES:(c + 1) * LANES]
    for (_, dil), ref in zip(DSW_PATTERNS, view_refs):
        if dil == 1:
            ref[...] = tok.astype(BF16)
            continue
        for r in range(dil):
            rows = jnp.concatenate([tok_sc[c, pl.ds(r, t // dil, stride=dil), :] for c in range(n_lane_tiles)],
                                   axis=1)
            ref[0, :, r * SPLIT:(r + 1) * SPLIT] = rows.astype(BF16)


def _att_in_kernel(x_ref, ng_ref, wtok_ref, wfm_ref, gains_ref, cos_ref, sin_ref, *refs):
    n_pat = len(DSW_PATTERNS)
    qa_views, ka_views, va_views = refs[0:n_pat], refs[n_pat:2 * n_pat], refs[2 * n_pat:3 * n_pat]
    g_ref, kd_ref, qd_ref, vd_ref, tok_sc = refs[3 * n_pat:]
    x = x_ref[...]
    ms = jnp.mean(x * x, axis=-1, keepdims=True)
    h = (x * lax.rsqrt(ms + RMS_EPS) * ng_ref[...]).astype(BF16)
    tok = jnp.dot(h, wtok_ref[...], preferred_element_type=F32)
    _emit_pattern_views(tok[:, :SPLIT], tok_sc, va_views)
    g_ref[...] = tok[:, SPLIT:]
    cos = cos_ref[...][None]
    sin = sin_ref[...][None]
    scales = (HEAD_DIM ** -0.5 * LOG2_E, 1.0, HEAD_DIM ** -0.5 * LOG2_E, 1.0)
    normed = []
    for idx in range(4):
        p = lax.dot_general(wfm_ref[idx * SPLIT:(idx + 1) * SPLIT, :], h, (((1,), (1,)), ((), ())),
                            preferred_element_type=F32)
        normed.append(_norm_rope_fm(p, gains_ref[idx], cos, sin, scales[idx]))
    _emit_pattern_views(normed[0].T, tok_sc, qa_views)
    _emit_pattern_views(normed[1].T, tok_sc, ka_views)
    qd_ref[...] = normed[2].astype(BF16)
    kd_ref[...] = normed[3].T.astype(BF16)
    p = lax.dot_general(wfm_ref[4 * SPLIT:5 * SPLIT, :], h, (((1,), (1,)), ((), ())),
                        preferred_element_type=F32)
    vd_ref[...] = p.astype(BF16)


def _att_in(x2d, norm_g, w_in, gains, cos_t, sin_t):
    s, d = x2d.shape
    tm = ROW_TILE
    w = w_in.astype(BF16)
    w_tok = jnp.concatenate([w[:, 2 * SPLIT:4 * SPLIT], w[:, 7 * SPLIT:8 * SPLIT]], axis=1)
    w_fm = jnp.concatenate([w[:, 0:2 * SPLIT], w[:, 4 * SPLIT:7 * SPLIT]], axis=1).T
    view_shapes, view_specs = [], []
    for _, dil in DSW_PATTERNS:
        if dil == 1:
            view_shapes.append(jax.ShapeDtypeStruct((s, SPLIT), BF16))
            view_specs.append(pl.BlockSpec((tm, SPLIT), lambda i: (i, 0)))
            continue
        rows = tm // dil
        assert tm % dil == 0 and BLK % rows == 0 and rows % 16 == 0
        per_blk = BLK // rows
        view_shapes.append(jax.ShapeDtypeStruct((s // (BLK * dil), BLK, dil * SPLIT), BF16))
        view_specs.append(pl.BlockSpec((1, rows, dil * SPLIT),
                                       functools.partial(lambda i, per_blk: (i // per_blk, i % per_blk, 0),
                                                         per_blk=per_blk)))
    fm = jax.ShapeDtypeStruct((SPLIT, s), BF16)
    fm_spec = pl.BlockSpec((SPLIT, tm), lambda i: (0, i))
    outs = pl.pallas_call(
        _att_in_kernel,
        grid=(s // tm,),
        in_specs=[
            pl.BlockSpec((tm, d), lambda i: (i, 0)),
            _const_spec((1, d)),
            _const_spec(w_tok.shape),
            _const_spec(w_fm.shape),
            _const_spec(gains.shape),
            pl.BlockSpec((ROT_HALF, tm), lambda i: (0, i)),
            pl.BlockSpec((ROT_HALF, tm), lambda i: (0, i)),
        ],
        out_specs=view_specs * 3 + [
            pl.BlockSpec((tm, 2 * SPLIT), lambda i: (i, 0)),
            pl.BlockSpec((tm, SPLIT), lambda i: (i, 0)),
            fm_spec, fm_spec,
        ],
        out_shape=view_shapes * 3 + [
            jax.ShapeDtypeStruct((s, 2 * SPLIT), F32),
            jax.ShapeDtypeStruct((s, SPLIT), BF16),
            fm, fm,
        ],
        scratch_shapes=[pltpu.VMEM((SPLIT // LANES, tm, LANES), F32)],
        compiler_params=_params("parallel"),
        name="att_in_proj",
    )(x2d, norm_g.reshape(1, d), w_tok, w_fm, gains, cos_t, sin_t)
    n_pat = len(DSW_PATTERNS)
    return (outs[0:n_pat], outs[n_pat:2 * n_pat], outs[2 * n_pat:3 * n_pat]) + tuple(outs[3 * n_pat:])


def _dsw_kernel(*refs, has_state, blocks_per_step):
    if has_state:
        (q_ref, kprev_ref, kcur_ref, vprev_ref, vcur_ref, oin_ref, lsein_ref,
         o_ref, lse_ref, kall, vall) = refs
    else:
        (q_ref, kprev_ref, kcur_ref, vprev_ref, vcur_ref, o_ref, lse_ref, kall, vall) = refs
        oin_ref = lsein_ref = None
    g_blocks = blocks_per_step
    step = pl.program_id(1)
    kall[0:BLK, :] = kprev_ref[0]
    vall[0:BLK, :] = vprev_ref[0]
    kall[BLK:, :] = kcur_ref[...].reshape(g_blocks * BLK, SPLIT)
    vall[BLK:, :] = vcur_ref[...].reshape(g_blocks * BLK, SPLIT)

    qi = lax.broadcasted_iota(jnp.int32, (BLK, 2 * BLK), 0) + BLK
    kj = lax.broadcasted_iota(jnp.int32, (BLK, 2 * BLK), 1)
    dist = qi - kj
    band = (dist >= 0) & (dist <= BLK)
    lane = lax.broadcasted_iota(jnp.int32, (BLK, 2 * HEAD_DIM), 1)
    low_half = lane < HEAD_DIM

    def block(g, carry):
        first_key = jnp.where((step * g_blocks + g) == 0, BLK, 0)
        mask = band & (kj >= first_key)
        row0 = pl.multiple_of(g * BLK, BLK)
        lse_tile = jnp.zeros((BLK, BLK), F32)
        lse_in = lsein_ref[g] if has_state else None
        for hp in range(N_GROUPS64 // 2):
            cols = slice(hp * 2 * HEAD_DIM, (hp + 1) * 2 * HEAD_DIM)
            q_pair = q_ref[g, :, cols]
            kk = kall[pl.ds(row0, 2 * BLK), cols]
            vv = vall[pl.ds(row0, 2 * BLK), cols]
            o_in = oin_ref[g, :, cols] if has_state else None
            halves = []
            for e in range(2):
                head = 2 * hp + e
                sel = low_half if e == 0 else jnp.logical_not(low_half)
                qm = jnp.where(sel, q_pair, jnp.zeros_like(q_pair))
                sc = lax.dot_general(qm, kk, (((1,), (1,)), ((), ())), preferred_element_type=F32)
                sc = jnp.where(mask, sc, NEG_INF)
                m_blk = jnp.max(sc, axis=-1, keepdims=True)
                if has_state:
                    m_old = lse_in[:, head:head + 1]
                    m_new = jnp.maximum(m_old, m_blk)
                    alpha = jnp.exp2(m_old - m_new)
                else:
                    m_new = m_blk
                p = jnp.exp2(sc - m_new)
                l = jnp.sum(p, axis=-1, keepdims=True)
                pv = jnp.dot(p.astype(BF16), vv, preferred_element_type=F32)
                if has_state:
                    l = l + alpha
                    pv = pv + alpha * o_in
                halves.append(pv / l)
                lse_tile = jnp.where(lane == head, m_new + jnp.log2(l), lse_tile)
            o_ref[g, :, cols] = jnp.where(low_half, halves[0], halves[1])
        lse_ref[g] = lse_tile
        return carry

    lax.fori_loop(0, g_blocks, block, 0)


def _dsw_pattern(q, k, v, dil, state):
    s = q.size // SPLIT
    nb = s // (BLK * dil)
    g_blocks = min(DSW_BLOCKS_PER_STEP, nb)
    view = lambda a, width: a.reshape(nb, BLK, dil * width)
    cur_spec = lambda width: pl.BlockSpec((g_blocks, BLK, width), lambda r, j: (j, 0, r))
    prev_spec = pl.BlockSpec((1, BLK, SPLIT), lambda r, j: (jnp.maximum(j * g_blocks - 1, 0), 0, r))
    operands = [view(q, SPLIT), view(k, SPLIT), view(k, SPLIT), view(v, SPLIT), view(v, SPLIT)]
    in_specs = [cur_spec(SPLIT), prev_spec, cur_spec(SPLIT), prev_spec, cur_spec(SPLIT)]
    if state is not None:
        operands += [view(state[0], SPLIT), view(state[1], BLK)]
        in_specs += [cur_spec(SPLIT), cur_spec(BLK)]
    o, lse = pl.pallas_call(
        functools.partial(_dsw_kernel, has_state=state is not None, blocks_per_step=g_blocks),
        grid=(dil, nb // g_blocks),
        in_specs=in_specs,
        out_specs=[cur_spec(SPLIT), cur_spec(BLK)],
        out_shape=[jax.ShapeDtypeStruct((nb, BLK, dil * SPLIT), F32),
                   jax.ShapeDtypeStruct((nb, BLK, dil * BLK), F32)],
        scratch_shapes=[pltpu.VMEM(((g_blocks + 1) * BLK, SPLIT), BF16),
                        pltpu.VMEM(((g_blocks + 1) * BLK, SPLIT), BF16)],
        compiler_params=_params("parallel", "arbitrary"),
        name=f"dsw_attention_dil{dil}",
    )(*operands)
    return o.reshape(s, SPLIT), lse.reshape(s, BLK)


def _dsw_unshifted_kernel(*refs, has_state, last, blocks_per_step):
    refs = list(refs)
    q_ref, kprev_ref, kcur_ref, vprev_ref, vcur_ref, ehot_ref = refs[:6]
    pos = 6
    if last:
        xexp_ref = refs[pos]
        pos += 1
    if has_state:
        accin_ref, lin_ref = refs[pos:pos + 2]
        pos += 2
    if last:
        o_ref = refs[pos]
        pos += 1
    else:
        acc_ref, l_ref = refs[pos:pos + 2]
        pos += 2
    kall, vall = refs[pos:pos + 2]
    perm_sc = None if last else refs[pos + 2]
    g_blocks = blocks_per_step
    step = pl.program_id(1)
    kall[0:BLK, :] = kprev_ref[0]
    vall[0:BLK, :] = vprev_ref[0]
    kall[BLK:, :] = kcur_ref[...].reshape(g_blocks * BLK, SPLIT)
    vall[BLK:, :] = vcur_ref[...].reshape(g_blocks * BLK, SPLIT)

    qi = (lax.broadcasted_iota(jnp.int32, (2 * BLK, 2 * BLK), 0) & (BLK - 1)) + BLK
    kj = lax.broadcasted_iota(jnp.int32, (2 * BLK, 2 * BLK), 1)
    dist = qi - kj
    band = (dist >= 0) & (dist <= BLK)
    q_low = lax.broadcasted_iota(jnp.int32, (BLK, 2 * HEAD_DIM), 1) < HEAD_DIM
    v_low = lax.broadcasted_iota(jnp.int32, (2 * BLK, 2 * HEAD_DIM), 1) < HEAD_DIM

    def block(g, carry):
        first_key = jnp.where((step * g_blocks + g) == 0, BLK, 0)
        mask = band & (kj >= first_key)
        row0 = pl.multiple_of(g * BLK, BLK)
        l_tile = lin_ref[g] if has_state else jnp.zeros((BLK, BLK), F32)
        accs = []
        for hp in range(N_GROUPS64 // 2):
            cols = slice(hp * 2 * HEAD_DIM, (hp + 1) * 2 * HEAD_DIM)
            q_pair = q_ref[g, :, cols]
            kk = kall[pl.ds(row0, 2 * BLK), cols]
            vv = vall[pl.ds(row0, 2 * BLK), cols]
            qz = jnp.zeros_like(q_pair)
            vz = jnp.zeros_like(vv)
            q2 = jnp.concatenate([jnp.where(q_low, q_pair, qz), jnp.where(q_low, qz, q_pair)], axis=0)
            sc = lax.dot_general(q2, kk, (((1,), (1,)), ((), ())), preferred_element_type=F32)
            p2 = jnp.where(mask, jnp.exp2(sc), 0.0).astype(BF16)
            p_cat = jnp.concatenate([p2[:BLK], p2[BLK:]], axis=1)
            v_split = jnp.concatenate([jnp.where(v_low, vv, vz), jnp.where(v_low, vz, vv)], axis=0)
            rhs = jnp.concatenate([v_split, ehot_ref[hp]], axis=1)
            out2 = jnp.dot(p_cat, rhs, preferred_element_type=F32)
            acc = out2[:, :2 * HEAD_DIM]
            if has_state:
                acc = acc + accin_ref[g, :, cols]
            l_tile = l_tile + out2[:, 2 * HEAD_DIM:]
            if last:
                accs.append(acc)
            else:
                perm_sc[hp] = acc
        if last:
            r = 1.0 / l_tile
            r_hi = r.astype(BF16)
            r_lo = (r - r_hi.astype(F32)).astype(BF16)
            r_exp = (jnp.dot(r_hi, xexp_ref[...], preferred_element_type=F32)
                     + jnp.dot(r_lo, xexp_ref[...], preferred_element_type=F32))
            for hp in range(N_GROUPS64 // 2):
                cols = slice(hp * 2 * HEAD_DIM, (hp + 1) * 2 * HEAD_DIM)
                o_ref[g, :, cols] = accs[hp] * r_exp[:, cols]
        else:
            perm_sc[N_GROUPS64 // 2] = l_tile
            ratio = DSW_DIL_RATIO
            rows = BLK // ratio
            out_blk = lax.shift_right_logical(g, ratio.bit_length() - 1)
            row_dst = pl.ds(pl.multiple_of((g & (ratio - 1)) * rows, rows), rows)
            for m in range(ratio):
                src = pl.ds(m, rows, stride=ratio)
                for hp in range(N_GROUPS64 // 2):
                    col0 = m * SPLIT + hp * 2 * HEAD_DIM
                    acc_ref[out_blk, row_dst, col0:col0 + 2 * HEAD_DIM] = perm_sc[hp, src, :]
                l_ref[out_blk, row_dst, m * BLK:(m + 1) * BLK] = perm_sc[N_GROUPS64 // 2, src, :]
        return carry

    lax.fori_loop(0, g_blocks, block, 0)


def _dsw_residue_of(hop):
    if hop == 0:
        return lambda g: g
    prev = _dsw_residue_of(hop - 1)
    prev_dil = DSW_PATTERNS[hop - 1][1]
    return lambda g: prev_dil * (g % DSW_DIL_RATIO) + prev(g // DSW_DIL_RATIO)


def _dsw_unshifted_pattern(q, k, v, hop, state, ehot, xexp):
    dil = DSW_PATTERNS[hop][1]
    last = hop == len(DSW_PATTERNS) - 1
    s = q.size // SPLIT
    nb = s // (BLK * dil)
    g_blocks = min(DSW_BLOCKS_PER_STEP, nb)
    ratio = DSW_DIL_RATIO
    residue = _dsw_residue_of(hop)
    view = lambda a: a.reshape(nb, BLK, dil * SPLIT)
    qkv_spec = pl.BlockSpec((g_blocks, BLK, SPLIT), lambda r, j: (j, 0, residue(r)))
    prev_spec = pl.BlockSpec((1, BLK, SPLIT), lambda r, j: (jnp.maximum(j * g_blocks - 1, 0), 0, residue(r)))
    state_spec = lambda width: pl.BlockSpec((g_blocks, BLK, width), lambda r, j: (j, 0, r))
    operands = [view(q), view(k), view(k), view(v), view(v), ehot]
    in_specs = [qkv_spec, prev_spec, qkv_spec, prev_spec, qkv_spec, _const_spec(ehot.shape)]
    if last:
        operands.append(xexp)
        in_specs.append(_const_spec(xexp.shape))
    if state is not None:
        operands += list(state)
        in_specs += [state_spec(SPLIT), state_spec(BLK)]
    scratch = [pltpu.VMEM(((g_blocks + 1) * BLK, SPLIT), BF16), pltpu.VMEM(((g_blocks + 1) * BLK, SPLIT), BF16)]
    if last:
        out_specs = [qkv_spec]
        out_shape = [jax.ShapeDtypeStruct((nb, BLK, dil * SPLIT), F32)]
    else:
        assert DSW_PATTERNS[hop + 1][1] == dil * ratio and g_blocks % ratio == 0
        next_spec = lambda width: pl.BlockSpec((g_blocks // ratio, BLK, ratio * width), lambda r, j: (j, 0, r))
        out_specs = [next_spec(SPLIT), next_spec(BLK)]
        out_shape = [jax.ShapeDtypeStruct((nb // ratio, BLK, dil * ratio * SPLIT), F32),
                     jax.ShapeDtypeStruct((nb // ratio, BLK, dil * ratio * BLK), F32)]
        scratch.append(pltpu.VMEM((N_GROUPS64 // 2 + 1, BLK, 2 * HEAD_DIM), F32))
    out = pl.pallas_call(
        functools.partial(_dsw_unshifted_kernel, has_state=state is not None, last=last, blocks_per_step=g_blocks),
        grid=(dil, nb // g_blocks),
        in_specs=in_specs,
        out_specs=out_specs,
        out_shape=out_shape,
        scratch_shapes=scratch,
        compiler_params=_params("parallel", "arbitrary"),
        name=f"dsw_unshifted_dil{dil}",
    )(*operands)
    return out[0] if last else tuple(out)


def _dsw_constants():
    lanes_per_head = BLK // N_GROUPS64
    ehot = np.zeros((N_GROUPS64 // 2, 4 * BLK, BLK), np.float32)
    xexp = np.zeros((BLK, SPLIT), np.float32)
    for head in range(N_GROUPS64):
        hp, e = divmod(head, 2)
        ehot[hp, e * 2 * BLK:(e + 1) * 2 * BLK, head * lanes_per_head:(head + 1) * lanes_per_head] = 1.0
        xexp[head * lanes_per_head, head * HEAD_DIM:(head + 1) * HEAD_DIM] = 1.0
    return jnp.asarray(ehot, BF16), jnp.asarray(xexp, BF16)


def _dsw_attention_unshifted(q_views, k_views, v_views):
    ehot, xexp = _dsw_constants()
    state = None
    for hop in range(len(DSW_PATTERNS)):
        state = _dsw_unshifted_pattern(q_views[hop], k_views[hop], v_views[hop], hop, state, ehot, xexp)
    return state


def _dsw_attention_online(q_views, k_views, v_views):
    state = None
    for idx, (_, dil) in enumerate(DSW_PATTERNS):
        state = _dsw_pattern(q_views[idx], k_views[idx], v_views[idx], dil, state)
    return state[0].reshape(q_views[-1].shape)


def _dsw_attention(q_views, k_views, v_views, score_bound):
    return lax.cond(score_bound <= DSW_UNSHIFTED_SCORE_BOUND, _dsw_attention_unshifted, _dsw_attention_online,
                    q_views, k_views, v_views)


def _diff_finalize(o0, o1, lam_ref, sg_ref, o_ref):
    lam_p = lam_ref[...]
    lam = (jnp.exp(jnp.sum(lam_p[0:1] * lam_p[1:2], axis=-1, keepdims=True))
           - jnp.exp(jnp.sum(lam_p[2:3] * lam_p[3:4], axis=-1, keepdims=True)) + LAM_INIT)
    od = o0 - lam * o1
    ms = jnp.mean(od * od, axis=0, keepdims=True)
    y = od * lax.rsqrt(ms + RMS_EPS) * sg_ref[...] * (1.0 - LAM_INIT)
    o_ref[...] = y.T


def _component_queries(qt):
    zeros = jnp.zeros((HEAD_DIM, qt.shape[1]), qt.dtype)
    return (jnp.concatenate([qt[:HEAD_DIM], zeros], axis=0), jnp.concatenate([zeros, qt[HEAD_DIM:]], axis=0))


def _causal_keep(i, k0, tq, tk):
    kpos = k0 + lax.broadcasted_iota(jnp.int32, (tk, tq), 0)
    qpos = i * tq + lax.broadcasted_iota(jnp.int32, (tk, tq), 1)
    return kpos <= qpos


def _diff_unshifted_kernel(qt_ref, k_ref, vt_ref, lam_ref, sg_ref, o_ref, l_sc, acc_sc, *, tq, tk):
    i = pl.program_id(1)
    qc = _component_queries(qt_ref[...])
    l_sc[...] = jnp.zeros(l_sc.shape, F32)
    acc_sc[...] = jnp.zeros(acc_sc.shape, F32)

    def full_chunk(j, carry):
        k0 = pl.multiple_of(j * tk, tk)
        vt = vt_ref[:, pl.ds(k0, tk)]
        for c in range(2):
            st = jnp.dot(k_ref[pl.ds(k0, tk), :], qc[c], preferred_element_type=F32)
            p = jnp.exp2(st)
            l_sc[c] += jnp.sum(p.reshape(tk // 8, 8, tq), axis=0)
            acc_sc[c] += jnp.dot(vt, p.astype(BF16), preferred_element_type=F32)
        return carry

    lax.fori_loop(0, i, full_chunk, 0)

    sb = DIFF_DIAG_BLOCK
    k0 = pl.multiple_of(i * tk, tk)
    tri = lax.broadcasted_iota(jnp.int32, (sb, sb), 0) <= lax.broadcasted_iota(jnp.int32, (sb, sb), 1)
    for qb in range(tq // sb):
        nk = (qb + 1) * sb
        lanes = slice(qb * sb, (qb + 1) * sb)
        vt = vt_ref[:, pl.ds(k0, nk)]
        for c in range(2):
            st = jnp.dot(k_ref[pl.ds(k0, nk), :], qc[c][:, lanes], preferred_element_type=F32)
            p = jnp.exp2(st)
            p_tri = jnp.where(tri, p[qb * sb:, :], 0.0)
            p = jnp.concatenate([p[:qb * sb, :], p_tri], axis=0) if qb else p_tri
            l_sc[c, :, lanes] += jnp.sum(p.reshape(nk // 8, 8, sb), axis=0)
            acc_sc[c, :, lanes] += jnp.dot(vt, p.astype(BF16), preferred_element_type=F32)

    l0 = jnp.sum(l_sc[0], axis=0, keepdims=True)
    l1 = jnp.sum(l_sc[1], axis=0, keepdims=True)
    _diff_finalize(acc_sc[0] / l0, acc_sc[1] / l1, lam_ref, sg_ref, o_ref)


def _diff_kernel(qt_ref, k_ref, vt_ref, lam_ref, sg_ref, o_ref, m_sc, l_sc, acc_sc, *, tq, tk):
    i = pl.program_id(1)
    qc = _component_queries(qt_ref[...])
    m_sc[...] = jnp.full(m_sc.shape, NEG_INF, F32)
    l_sc[...] = jnp.zeros(l_sc.shape, F32)
    acc_sc[...] = jnp.zeros(acc_sc.shape, F32)

    def chunk(j, masked):
        k0 = pl.multiple_of(j * tk, tk)
        vt = vt_ref[:, pl.ds(k0, tk)]
        for c in range(2):
            st = jnp.dot(k_ref[pl.ds(k0, tk), :], qc[c], preferred_element_type=F32)
            if masked:
                st = jnp.where(_causal_keep(i, k0, tq, tk), st, NEG_INF)
            m_old = m_sc[c]
            m_new = jnp.maximum(m_old, jnp.max(st, axis=0, keepdims=True))
            alpha = jnp.exp2(m_old - m_new)
            p = jnp.exp2(st - m_new)
            l_sc[c] = alpha * l_sc[c] + jnp.sum(p, axis=0, keepdims=True)
            acc_sc[c] = alpha * acc_sc[c] + jnp.dot(vt, p.astype(BF16), preferred_element_type=F32)
            m_sc[c] = m_new

    n_full = (i * tq) // tk

    def full_chunk(j, carry):
        chunk(j, False)
        return carry

    lax.fori_loop(0, n_full, full_chunk, 0)
    for jj in range(tq // tk):
        chunk(n_full + jj, True)

    _diff_finalize(acc_sc[0] / l_sc[0], acc_sc[1] / l_sc[1], lam_ref, sg_ref, o_ref)


def _diff_call(body, scratch, name, qd_t, kd_tok, vd_t, lam_params, subln_col):
    s = qd_t.shape[1]
    tq, tk = DIFF_TQ, DIFF_TK
    assert tq == tk and s % tq == 0 and tq % DIFF_DIAG_BLOCK == 0
    return pl.pallas_call(
        functools.partial(body, tq=tq, tk=tk),
        grid=(DIFF_HEADS, s // tq),
        in_specs=[
            pl.BlockSpec((DIFF_V_DIM, tq), lambda h, i: (h, i)),
            pl.BlockSpec((s, 2 * HEAD_DIM), lambda h, i: (0, h)),
            pl.BlockSpec((DIFF_V_DIM, s), lambda h, i: (h, 0)),
            _const_spec(lam_params.shape),
            _const_spec(subln_col.shape),
        ],
        out_specs=pl.BlockSpec((tq, DIFF_V_DIM), lambda h, i: (i, h)),
        out_shape=jax.ShapeDtypeStruct((s, DIFF_HEADS * DIFF_V_DIM), F32),
        scratch_shapes=scratch(tq),
        compiler_params=_params("parallel", "arbitrary"),
        name=name,
    )(qd_t, kd_tok, vd_t, lam_params, subln_col)


def _diff_attention(qd_t, kd_tok, vd_t, lam_params, subln_col, score_bound):
    acc = lambda tq: pltpu.VMEM((2, DIFF_V_DIM, tq), F32)
    unshifted = functools.partial(
        _diff_call, _diff_unshifted_kernel, lambda tq: [pltpu.VMEM((2, 8, tq), F32), acc(tq)],
        "diff_attention_unshifted")
    online = functools.partial(
        _diff_call, _diff_kernel, lambda tq: [pltpu.VMEM((2, 1, tq), F32), pltpu.VMEM((2, 1, tq), F32), acc(tq)],
        "diff_attention_online")
    return lax.cond(score_bound <= DIFF_UNSHIFTED_SCORE_BOUND, unshifted, online,
                    qd_t, kd_tok, vd_t, lam_params, subln_col)


def _att_out_kernel(x_ref, oa_ref, od_ref, g_ref, w_ref, o_ref, tok_sc, *, dil):
    n_lane_tiles = SPLIT // LANES
    rows = tok_sc.shape[1] // dil
    for r in range(dil):
        for c in range(n_lane_tiles):
            col0 = r * SPLIT + c * LANES
            tok_sc[c, pl.ds(r, rows, stride=dil), :] = oa_ref[0, :, col0:col0 + LANES]
    oa = jnp.concatenate([tok_sc[c] for c in range(n_lane_tiles)], axis=1)
    g = g_ref[...]
    y = jnp.concatenate([oa, od_ref[...]], axis=-1) * (g * jax.nn.sigmoid(g))
    o_ref[...] = x_ref[...] + jnp.dot(y.astype(BF16), w_ref[...], preferred_element_type=F32)


def _att_out(x2d, oa_view, od, g, w_out):
    s, d = x2d.shape
    tm = ROW_TILE
    dil = DSW_PATTERNS[-1][1]
    rows = tm // dil
    per_blk = BLK // rows
    assert oa_view.shape == (s // (BLK * dil), BLK, dil * SPLIT) and tm % dil == 0 and BLK % rows == 0
    row = lambda width: pl.BlockSpec((tm, width), lambda i: (i, 0))
    oa_spec = pl.BlockSpec((1, rows, dil * SPLIT), lambda i: (i // per_blk, i % per_blk, 0))
    return pl.pallas_call(
        functools.partial(_att_out_kernel, dil=dil),
        grid=(s // tm,),
        in_specs=[row(d), oa_spec, row(SPLIT), row(2 * SPLIT), _const_spec(w_out.shape)],
        out_specs=row(d),
        out_shape=jax.ShapeDtypeStruct((s, d), F32),
        scratch_shapes=[pltpu.VMEM((SPLIT // LANES, tm, LANES), F32)],
        compiler_params=_params("parallel"),
        name="att_out_proj",
    )(x2d, oa_view, od, g, w_out.astype(BF16))


def _gelu(x):
    return 0.5 * x * (1.0 + lax.erf(x * (2.0 ** -0.5)))


def _sgu_kernel(x_ref, ng_ref, win_ref, lng_ref, lnb_ref, ws_ref, bs_ref, wout_ref, o_ref, *, width):
    x = x_ref[...]
    tm = x.shape[0]
    ms = jnp.mean(x * x, axis=-1, keepdims=True)
    h = (x * lax.rsqrt(ms + RMS_EPS) * ng_ref[...]).astype(BF16)
    v = _gelu(jnp.dot(h, win_ref[:, width:2 * width], preferred_element_type=F32))
    mu = jnp.mean(v, axis=-1, keepdims=True)
    vc = v - mu
    vn = vc * lax.rsqrt(jnp.mean(vc * vc, axis=-1, keepdims=True) + LN_EPS)
    vn = (vn * lng_ref[...] + lnb_ref[...]).astype(BF16)
    gw = width // SGU_GROUPS
    row = lax.broadcasted_iota(jnp.int32, (SGU_CHUNK, SGU_CHUNK), 0)
    col = lax.broadcasted_iota(jnp.int32, (SGU_CHUNK, SGU_CHUNK), 1)
    causal = col <= row
    sp_groups = []
    for grp in range(SGU_GROUPS):
        ws = jnp.where(causal, ws_ref[grp], 0.0).astype(BF16)
        bias = bs_ref[grp]
        chunks = []
        for c in range(tm // SGU_CHUNK):
            vg = vn[c * SGU_CHUNK:(c + 1) * SGU_CHUNK, grp * gw:(grp + 1) * gw]
            chunks.append(jnp.dot(ws, vg, preferred_element_type=F32) + bias)
        sp_groups.append(jnp.concatenate(chunks, axis=0))
    sp = jnp.concatenate(sp_groups, axis=1)
    u = _gelu(jnp.dot(h, win_ref[:, 0:width], preferred_element_type=F32))
    g = jnp.dot(h, win_ref[:, 2 * width:3 * width], preferred_element_type=F32)
    y = u * sp * (g * jax.nn.sigmoid(g))
    o_ref[...] = x + jnp.dot(y.astype(BF16), wout_ref[...], preferred_element_type=F32)


def _sgu_layer(x2d, norm_g, w_in, ln_g, ln_b, w_s, b_s, w_out):
    s, d = x2d.shape
    width = w_out.shape[0]
    tm = ROW_TILE
    row = pl.BlockSpec((tm, d), lambda i: (i, 0))
    single = lambda shape: pl.BlockSpec(shape, lambda *_: (0,) * len(shape), pipeline_mode=pl.Buffered(1))
    return pl.pallas_call(
        functools.partial(_sgu_kernel, width=width),
        grid=(s // tm,),
        in_specs=[row, _const_spec((1, d)), single(w_in.shape), _const_spec((1, width)), _const_spec((1, width)),
                  _const_spec(w_s.shape), _const_spec((SGU_GROUPS, SGU_CHUNK, 1)), single(w_out.shape)],
        out_specs=row,
        out_shape=jax.ShapeDtypeStruct((s, d), F32),
        compiler_params=_params("parallel"),
        name="sgu_layer",
    )(x2d, norm_g.reshape(1, d), w_in.astype(BF16), ln_g.reshape(1, width), ln_b.reshape(1, width),
      w_s, b_s.reshape(SGU_GROUPS, SGU_CHUNK, 1), w_out.astype(BF16))


def _rope_tables_fm(seq_len):
    pos = jnp.arange(seq_len, dtype=F32)
    inv = 1.0 / (ROPE_THETA ** (jnp.arange(0, 2 * ROT_HALF, 2, dtype=F32) / (2 * ROT_HALF)))
    ang = inv[:, None] * pos[None, :]
    return jnp.cos(ang), jnp.sin(ang)


def kernel(x, att_norm, att_w_in, dsw_q_norm, dsw_k_norm, diff_q_norm, diff_k_norm, diff_lam_q1, diff_lam_k1,
           diff_lam_q2, diff_lam_k2, diff_subln, att_w_out, sgu_norm, sgu_w_in, sgu_ln_g, sgu_ln_b, sgu_w_s,
           sgu_b_s, sgu_w_out):
    b, s, d = x.shape
    assert b == 1 and s % (BLK * DSW_PATTERNS[-1][1]) == 0
    x2d = x.reshape(s, d)
    cos_t, sin_t = _rope_tables_fm(s)

    gains = jnp.stack([dsw_q_norm[0], dsw_k_norm[0], diff_q_norm[0], diff_k_norm[0]])
    gains = jnp.broadcast_to(gains[:, None, :, None], (4, N_GROUPS64, HEAD_DIM, 1))
    qa_views, ka_views, va_views, g, kd_tok, qd_t, vd_t = _att_in(x2d, att_norm[0], att_w_in[0], gains, cos_t, sin_t)
    bound = lambda gq, gk: 1.02 * LOG2_E * HEAD_DIM ** 0.5 * jnp.max(jnp.abs(gq)) * jnp.max(jnp.abs(gk))
    oa = _dsw_attention(qa_views, ka_views, va_views, bound(dsw_q_norm[0], dsw_k_norm[0]))
    lam_params = jnp.stack([diff_lam_q1[0], diff_lam_k1[0], diff_lam_q2[0], diff_lam_k2[0]])
    od = _diff_attention(qd_t, kd_tok, vd_t, lam_params, diff_subln[0].reshape(DIFF_V_DIM, 1),
                         bound(diff_q_norm[0], diff_k_norm[0]))
    x2d = _att_out(x2d, oa, od, g, att_w_out[0])

    x2d = _sgu_layer(x2d, sgu_norm[0], sgu_w_in[0], sgu_ln_g[0], sgu_ln_b[0], sgu_w_s[0], sgu_b_s[0], sgu_w_out[0])
    return x2d.reshape(b, s, d)
```

```python
import functools
import math

import numpy as np
import jax
import jax.numpy as jnp
from jax import lax
from jax.experimental import pallas as pl
from jax.experimental.pallas import tpu as pltpu

F32 = jnp.float32
BF16 = jnp.bfloat16

LANES = 128
HEAD_DIM = 64
ROT_HALF = 8
ROPE_THETA = 500000.0
BLK = 128
N_GROUPS64 = 8
SPLIT = N_GROUPS64 * HEAD_DIM
DSW_PATTERNS = ((128, 1), (512, 4), (2048, 16))
DSW_DIL_RATIO = 4
DIFF_HEADS = 4
DIFF_V_DIM = 128
SGU_GROUPS = 8
SGU_CHUNK = 128
RMS_EPS = 1e-6
LN_EPS = 1e-5
NEG_INF = -1e30
LAM_INIT = 0.8 - 0.6 * math.exp(-0.3 * 0)
LOG2_E = math.log2(math.e)
DIFF_UNSHIFTED_SCORE_BOUND = 96.0
DSW_UNSHIFTED_SCORE_BOUND = 96.0

V7X_VMEM_LIMIT_BYTES = 56 * 1024 * 1024

ROW_TILE = 512
DSW_BLOCKS_PER_STEP = 8
DIFF_TQ = 1024
DIFF_TK = 1024
DIFF_DIAG_BLOCK = 512


def _params(*sem):
    return pltpu.CompilerParams(dimension_semantics=sem, vmem_limit_bytes=V7X_VMEM_LIMIT_BYTES)


def _const_spec(shape):
    nd = len(shape)
    return pl.BlockSpec(shape, lambda *_: (0,) * nd)


def _norm_rope_fm(p, gain, cos, sin, scale):
    t = p.shape[1]
    p3 = p.reshape(N_GROUPS64, HEAD_DIM, t)
    ms = jnp.mean(p3 * p3, axis=1, keepdims=True)
    y = p3 * lax.rsqrt(ms + RMS_EPS) * gain
    x1 = y[:, 0:ROT_HALF, :]
    x2 = y[:, ROT_HALF:2 * ROT_HALF, :]
    out = jnp.concatenate([x1 * cos - x2 * sin, x2 * cos + x1 * sin, y[:, 2 * ROT_HALF:, :]], axis=1)
    if scale != 1.0:
        out = out * scale
    return out.reshape(SPLIT, t)


def _emit_pattern_views(tok, tok_sc, view_refs):
    t = tok.shape[0]
    n_lane_tiles = SPLIT // LANES
    for c in range(n_lane_tiles):
        tok_sc[c] = tok[:, c * LANES:(c + 1) * LANES]
    for (_, dil), ref in zip(DSW_PATTERNS, view_refs):
        if dil == 1:
            ref[...] = tok.astype(BF16)
            continue
        for r in range(dil):
            rows = jnp.concatenate([tok_sc[c, pl.ds(r, t // dil, stride=dil), :] for c in range(n_lane_tiles)],
                                   axis=1)
            ref[0, :, r * SPLIT:(r + 1) * SPLIT] = rows.astype(BF16)


def _att_in_kernel(x_ref, ng_ref, wtok_ref, wfm_ref, gains_ref, cos_ref, sin_ref, *refs):
    n_pat = len(DSW_PATTERNS)
    qa_views, ka_views, va_views = refs[0:n_pat], refs[n_pat:2 * n_pat], refs[2 * n_pat:3 * n_pat]
    g_ref, kd_ref, qd_ref, vd_ref, tok_sc = refs[3 * n_pat:]
    x = x_ref[...]
    ms = jnp.mean(x * x, axis=-1, keepdims=True)
    h = (x * lax.rsqrt(ms + RMS_EPS) * ng_ref[...]).astype(BF16)
    tok = jnp.dot(h, wtok_ref[...], preferred_element_type=F32)
    _emit_pattern_views(tok[:, :SPLIT], tok_sc, va_views)
    g_ref[...] = tok[:, SPLIT:]
    cos = cos_ref[...][None]
    sin = sin_ref[...][None]
    scales = (HEAD_DIM ** -0.5 * LOG2_E, 1.0, HEAD_DIM ** -0.5 * LOG2_E, 1.0)
    normed = []
    for idx in range(4):
        p = lax.dot_general(wfm_ref[idx * SPLIT:(idx + 1) * SPLIT, :], h, (((1,), (1,)), ((), ())),
                            preferred_element_type=F32)
        normed.append(_norm_rope_fm(p, gains_ref[idx], cos, sin, scales[idx]))
    _emit_pattern_views(normed[0].T, tok_sc, qa_views)
    _emit_pattern_views(normed[1].T, tok_sc, ka_views)
    qd_ref[...] = normed[2].astype(BF16)
    kd_ref[...] = normed[3].T.astype(BF16)
    p = lax.dot_general(wfm_ref[4 * SPLIT:5 * SPLIT, :], h, (((1,), (1,)), ((), ())),
                        preferred_element_type=F32)
    vd_ref[...] = p.astype(BF16)


def _att_in(x2d, norm_g, w_in, gains, cos_t, sin_t):
    s, d = x2d.shape
    tm = ROW_TILE
    w = w_in.astype(BF16)
    w_tok = jnp.concatenate([w[:, 2 * SPLIT:4 * SPLIT], w[:, 7 * SPLIT:8 * SPLIT]], axis=1)
    w_fm = jnp.concatenate([w[:, 0:2 * SPLIT], w[:, 4 * SPLIT:7 * SPLIT]], axis=1).T
    view_shapes, view_specs = [], []
    for _, dil in DSW_PATTERNS:
        if dil == 1:
            view_shapes.append(jax.ShapeDtypeStruct((s, SPLIT), BF16))
            view_specs.append(pl.BlockSpec((tm, SPLIT), lambda i: (i, 0)))
            continue
        rows = tm // dil
        assert tm % dil == 0 and BLK % rows == 0 and rows % 16 == 0
        per_blk = BLK // rows
        view_shapes.append(jax.ShapeDtypeStruct((s // (BLK * dil), BLK, dil * SPLIT), BF16))
        view_specs.append(pl.BlockSpec((1, rows, dil * SPLIT),
                                       functools.partial(lambda i, per_blk: (i // per_blk, i % per_blk, 0),
                                                         per_blk=per_blk)))
    fm = jax.ShapeDtypeStruct((SPLIT, s), BF16)
    fm_spec = pl.BlockSpec((SPLIT, tm), lambda i: (0, i))
    outs = pl.pallas_call(
        _att_in_kernel,
        grid=(s // tm,),
        in_specs=[
            pl.BlockSpec((tm, d), lambda i: (i, 0)),
            _const_spec((1, d)),
            _const_spec(w_tok.shape),
            _const_spec(w_fm.shape),
            _const_spec(gains.shape),
            pl.BlockSpec((ROT_HALF, tm), lambda i: (0, i)),
            pl.BlockSpec((ROT_HALF, tm), lambda i: (0, i)),
        ],
        out_specs=view_specs * 3 + [
            pl.BlockSpec((tm, 2 * SPLIT), lambda i: (i, 0)),
            pl.BlockSpec((tm, SPLIT), lambda i: (i, 0)),
            fm_spec, fm_spec,
        ],
        out_shape=view_shapes * 3 + [
            jax.ShapeDtypeStruct((s, 2 * SPLIT), F32),
            jax.ShapeDtypeStruct((s, SPLIT), BF16),
            fm, fm,
        ],
        scratch_shapes=[pltpu.VMEM((SPLIT // LANES, tm, LANES), F32)],
        compiler_params=_params("parallel"),
        name="att_in_proj",
    )(x2d, norm_g.reshape(1, d), w_tok, w_fm, gains, cos_t, sin_t)
    n_pat = len(DSW_PATTERNS)
    return (outs[0:n_pat], outs[n_pat:2 * n_pat], outs[2 * n_pat:3 * n_pat]) + tuple(outs[3 * n_pat:])


def _dsw_kernel(*refs, has_state, blocks_per_step):
    if has_state:
        (q_ref, kprev_ref, kcur_ref, vprev_ref, vcur_ref, oin_ref, lsein_ref,
         o_ref, lse_ref, kall, vall) = refs
    else:
        (q_ref, kprev_ref, kcur_ref, vprev_ref, vcur_ref, o_ref, lse_ref, kall, vall) = refs
        oin_ref = lsein_ref = None
    g_blocks = blocks_per_step
    step = pl.program_id(1)
    kall[0:BLK, :] = kprev_ref[0]
    vall[0:BLK, :] = vprev_ref[0]
    kall[BLK:, :] = kcur_ref[...].reshape(g_blocks * BLK, SPLIT)
    vall[BLK:, :] = vcur_ref[...].reshape(g_blocks * BLK, SPLIT)

    qi = lax.broadcasted_iota(jnp.int32, (BLK, 2 * BLK), 0) + BLK
    kj = lax.broadcasted_iota(jnp.int32, (BLK, 2 * BLK), 1)
    dist = qi - kj
    band = (dist >= 0) & (dist <= BLK)
    lane = lax.broadcasted_iota(jnp.int32, (BLK, 2 * HEAD_DIM), 1)
    low_half = lane < HEAD_DIM

    def block(g, carry):
        first_key = jnp.where((step * g_blocks + g) == 0, BLK, 0)
        mask = band & (kj >= first_key)
        row0 = pl.multiple_of(g * BLK, BLK)
        lse_tile = jnp.zeros((BLK, BLK), F32)
        lse_in = lsein_ref[g] if has_state else None
        for hp in range(N_GROUPS64 // 2):
            cols = slice(hp * 2 * HEAD_DIM, (hp + 1) * 2 * HEAD_DIM)
            q_pair = q_ref[g, :, cols]
            kk = kall[pl.ds(row0, 2 * BLK), cols]
            vv = vall[pl.ds(row0, 2 * BLK), cols]
            o_in = oin_ref[g, :, cols] if has_state else None
            halves = []
            for e in range(2):
                head = 2 * hp + e
                sel = low_half if e == 0 else jnp.logical_not(low_half)
                qm = jnp.where(sel, q_pair, jnp.zeros_like(q_pair))
                sc = lax.dot_general(qm, kk, (((1,), (1,)), ((), ())), preferred_element_type=F32)
                sc = jnp.where(mask, sc, NEG_INF)
                m_blk = jnp.max(sc, axis=-1, keepdims=True)
                if has_state:
                    m_old = lse_in[:, head:head + 1]
                    m_new = jnp.maximum(m_old, m_blk)
                    alpha = jnp.exp2(m_old - m_new)
                else:
                    m_new = m_blk
                p = jnp.exp2(sc - m_new)
                l = jnp.sum(p, axis=-1, keepdims=True)
                pv = jnp.dot(p.astype(BF16), vv, preferred_element_type=F32)
                if has_state:
                    l = l + alpha
                    pv = pv + alpha * o_in
                halves.append(pv / l)
                lse_tile = jnp.where(lane == head, m_new + jnp.log2(l), lse_tile)
            o_ref[g, :, cols] = jnp.where(low_half, halves[0], halves[1])
        lse_ref[g] = lse_tile
        return carry

    lax.fori_loop(0, g_blocks, block, 0)


def _dsw_pattern(q, k, v, dil, state):
    s = q.size // SPLIT
    nb = s // (BLK * dil)
    g_blocks = min(DSW_BLOCKS_PER_STEP, nb)
    view = lambda a, width: a.reshape(nb, BLK, dil * width)
    cur_spec = lambda width: pl.BlockSpec((g_blocks, BLK, width), lambda r, j: (j, 0, r))
    prev_spec = pl.BlockSpec((1, BLK, SPLIT), lambda r, j: (jnp.maximum(j * g_blocks - 1, 0), 0, r))
    operands = [view(q, SPLIT), view(k, SPLIT), view(k, SPLIT), view(v, SPLIT), view(v, SPLIT)]
    in_specs = [cur_spec(SPLIT), prev_spec, cur_spec(SPLIT), prev_spec, cur_spec(SPLIT)]
    if state is not None:
        operands += [view(state[0], SPLIT), view(state[1], BLK)]
        in_specs += [cur_spec(SPLIT), cur_spec(BLK)]
    o, lse = pl.pallas_call(
        functools.partial(_dsw_kernel, has_state=state is not None, blocks_per_step=g_blocks),
        grid=(dil, nb // g_blocks),
        in_specs=in_specs,
        out_specs=[cur_spec(SPLIT), cur_spec(BLK)],
        out_shape=[jax.ShapeDtypeStruct((nb, BLK, dil * SPLIT), F32),
                   jax.ShapeDtypeStruct((nb, BLK, dil * BLK), F32)],
        scratch_shapes=[pltpu.VMEM(((g_blocks + 1) * BLK, SPLIT), BF16),
                        pltpu.VMEM(((g_blocks + 1) * BLK, SPLIT), BF16)],
        compiler_params=_params("parallel", "arbitrary"),
        name=f"dsw_attention_dil{dil}",
    )(*operands)
    return o.reshape(s, SPLIT), lse.reshape(s, BLK)


def _dsw_unshifted_kernel(*refs, has_state, last, blocks_per_step):
    refs = list(refs)
    q_ref, kprev_ref, kcur_ref, vprev_ref, vcur_ref, ehot_ref = refs[:6]
    pos = 6
    if has_state:
        accin_ref, lin_ref = refs[pos:pos + 2]
        pos += 2
    acc_ref, l_ref, kall, vall = refs[pos:pos + 4]
    perm_sc = None if last else refs[pos + 4]
    g_blocks = blocks_per_step
    step = pl.program_id(1)
    kall[0:BLK, :] = kprev_ref[0]
    vall[0:BLK, :] = vprev_ref[0]
    kall[BLK:, :] = kcur_ref[...].reshape(g_blocks * BLK, SPLIT)
    vall[BLK:, :] = vcur_ref[...].reshape(g_blocks * BLK, SPLIT)

    qi = (lax.broadcasted_iota(jnp.int32, (2 * BLK, 2 * BLK), 0) & (BLK - 1)) + BLK
    kj = lax.broadcasted_iota(jnp.int32, (2 * BLK, 2 * BLK), 1)
    dist = qi - kj
    band = (dist >= 0) & (dist <= BLK)
    q_low = lax.broadcasted_iota(jnp.int32, (BLK, 2 * HEAD_DIM), 1) < HEAD_DIM
    v_low = lax.broadcasted_iota(jnp.int32, (2 * BLK, 2 * HEAD_DIM), 1) < HEAD_DIM

    def block(g, carry):
        first_key = jnp.where((step * g_blocks + g) == 0, BLK, 0)
        mask = band & (kj >= first_key)
        row0 = pl.multiple_of(g * BLK, BLK)
        l_tile = lin_ref[g] if has_state else jnp.zeros((BLK, BLK), F32)
        for hp in range(N_GROUPS64 // 2):
            cols = slice(hp * 2 * HEAD_DIM, (hp + 1) * 2 * HEAD_DIM)
            q_pair = q_ref[g, :, cols]
            kk = kall[pl.ds(row0, 2 * BLK), cols]
            vv = vall[pl.ds(row0, 2 * BLK), cols]
            qz = jnp.zeros_like(q_pair)
            vz = jnp.zeros_like(vv)
            q2 = jnp.concatenate([jnp.where(q_low, q_pair, qz), jnp.where(q_low, qz, q_pair)], axis=0)
            sc = lax.dot_general(q2, kk, (((1,), (1,)), ((), ())), preferred_element_type=F32)
            p2 = jnp.where(mask, jnp.exp2(sc), 0.0).astype(BF16)
            p_cat = jnp.concatenate([p2[:BLK], p2[BLK:]], axis=1)
            v_split = jnp.concatenate([jnp.where(v_low, vv, vz), jnp.where(v_low, vz, vv)], axis=0)
            rhs = jnp.concatenate([v_split, ehot_ref[hp]], axis=1)
            out2 = jnp.dot(p_cat, rhs, preferred_element_type=F32)
            acc = out2[:, :2 * HEAD_DIM]
            if has_state:
                acc = acc + accin_ref[g, :, cols]
            l_tile = l_tile + out2[:, 2 * HEAD_DIM:]
            if last:
                acc_ref[g, :, cols] = acc
            else:
                perm_sc[hp] = acc
        if last:
            l_ref[g] = l_tile
        else:
            perm_sc[N_GROUPS64 // 2] = l_tile
            ratio = DSW_DIL_RATIO
            rows = BLK // ratio
            out_blk = lax.shift_right_logical(g, ratio.bit_length() - 1)
            row_dst = pl.ds(pl.multiple_of((g & (ratio - 1)) * rows, rows), rows)
            for m in range(ratio):
                src = pl.ds(m, rows, stride=ratio)
                for hp in range(N_GROUPS64 // 2):
                    col0 = m * SPLIT + hp * 2 * HEAD_DIM
                    acc_ref[out_blk, row_dst, col0:col0 + 2 * HEAD_DIM] = perm_sc[hp, src, :]
                l_ref[out_blk, row_dst, m * BLK:(m + 1) * BLK] = perm_sc[N_GROUPS64 // 2, src, :]
        return carry

    lax.fori_loop(0, g_blocks, block, 0, unroll=True)


def _dsw_residue_of(hop):
    if hop == 0:
        return lambda g: g
    prev = _dsw_residue_of(hop - 1)
    prev_dil = DSW_PATTERNS[hop - 1][1]
    return lambda g: prev_dil * (g % DSW_DIL_RATIO) + prev(g // DSW_DIL_RATIO)


def _dsw_unshifted_pattern(q, k, v, hop, state, ehot):
    dil = DSW_PATTERNS[hop][1]
    last = hop == len(DSW_PATTERNS) - 1
    s = q.size // SPLIT
    nb = s // (BLK * dil)
    g_blocks = min(DSW_BLOCKS_PER_STEP, nb)
    ratio = DSW_DIL_RATIO
    residue = _dsw_residue_of(hop)
    view = lambda a: a.reshape(nb, BLK, dil * SPLIT)
    qkv_spec = pl.BlockSpec((g_blocks, BLK, SPLIT), lambda r, j: (j, 0, residue(r)))
    prev_spec = pl.BlockSpec((1, BLK, SPLIT), lambda r, j: (jnp.maximum(j * g_blocks - 1, 0), 0, residue(r)))
    state_spec = lambda width: pl.BlockSpec((g_blocks, BLK, width), lambda r, j: (j, 0, r))
    operands = [view(q), view(k), view(k), view(v), view(v), ehot]
    in_specs = [qkv_spec, prev_spec, qkv_spec, prev_spec, qkv_spec, _const_spec(ehot.shape)]
    if state is not None:
        operands += list(state)
        in_specs += [state_spec(SPLIT), state_spec(BLK)]
    scratch = [pltpu.VMEM(((g_blocks + 1) * BLK, SPLIT), BF16), pltpu.VMEM(((g_blocks + 1) * BLK, SPLIT), BF16)]
    if last:
        own_spec = lambda width: pl.BlockSpec((g_blocks, BLK, width), lambda r, j: (j, 0, residue(r)))
        out_specs = [own_spec(SPLIT), own_spec(BLK)]
        out_shape = [jax.ShapeDtypeStruct((nb, BLK, dil * SPLIT), F32),
                     jax.ShapeDtypeStruct((nb, BLK, dil * BLK), F32)]
    else:
        assert DSW_PATTERNS[hop + 1][1] == dil * ratio and g_blocks % ratio == 0
        next_spec = lambda width: pl.BlockSpec((g_blocks // ratio, BLK, ratio * width), lambda r, j: (j, 0, r))
        out_specs = [next_spec(SPLIT), next_spec(BLK)]
        out_shape = [jax.ShapeDtypeStruct((nb // ratio, BLK, dil * ratio * SPLIT), F32),
                     jax.ShapeDtypeStruct((nb // ratio, BLK, dil * ratio * BLK), F32)]
        scratch.append(pltpu.VMEM((N_GROUPS64 // 2 + 1, BLK, 2 * HEAD_DIM), F32))
    out = pl.pallas_call(
        functools.partial(_dsw_unshifted_kernel, has_state=state is not None, last=last, blocks_per_step=g_blocks),
        grid=(dil, nb // g_blocks),
        in_specs=in_specs,
        out_specs=out_specs,
        out_shape=out_shape,
        scratch_shapes=scratch,
        compiler_params=_params("parallel", "arbitrary"),
        name=f"dsw_unshifted_dil{dil}",
    )(*operands)
    return tuple(out)


DSW_L_LANES_PER_HEAD = BLK // N_GROUPS64


def _dsw_constants():
    ehot = np.zeros((N_GROUPS64 // 2, 4 * BLK, BLK), np.float32)
    xexp = np.zeros((BLK, SPLIT), np.float32)
    for head in range(N_GROUPS64):
        hp, e = divmod(head, 2)
        lanes = slice(head * DSW_L_LANES_PER_HEAD, (head + 1) * DSW_L_LANES_PER_HEAD)
        ehot[hp, e * 2 * BLK:(e + 1) * 2 * BLK, lanes] = 1.0
        xexp[head * DSW_L_LANES_PER_HEAD, head * HEAD_DIM:(head + 1) * HEAD_DIM] = 1.0
    return jnp.asarray(ehot, BF16), jnp.asarray(xexp, BF16)


def _dsw_attention_unshifted(q_views, k_views, v_views):
    ehot, _ = _dsw_constants()
    state = None
    for hop in range(len(DSW_PATTERNS)):
        state = _dsw_unshifted_pattern(q_views[hop], k_views[hop], v_views[hop], hop, state, ehot)
    return state


def _dsw_attention_online(q_views, k_views, v_views):
    state = None
    for idx, (_, dil) in enumerate(DSW_PATTERNS):
        state = _dsw_pattern(q_views[idx], k_views[idx], v_views[idx], dil, state)
    o_view = state[0].reshape(q_views[-1].shape)
    return o_view, jnp.ones(o_view.shape[:2] + (o_view.shape[2] // SPLIT * BLK,), F32)


def _dsw_attention(q_views, k_views, v_views, score_bound):
    return lax.cond(score_bound <= DSW_UNSHIFTED_SCORE_BOUND, _dsw_attention_unshifted, _dsw_attention_online,
                    q_views, k_views, v_views)


def _diff_finalize(o0, o1, lam_ref, sg_ref, o_ref):
    lam_p = lam_ref[...]
    lam = (jnp.exp(jnp.sum(lam_p[0:1] * lam_p[1:2], axis=-1, keepdims=True))
           - jnp.exp(jnp.sum(lam_p[2:3] * lam_p[3:4], axis=-1, keepdims=True)) + LAM_INIT)
    od = o0 - lam * o1
    ms = jnp.mean(od * od, axis=0, keepdims=True)
    y = od * lax.rsqrt(ms + RMS_EPS) * sg_ref[...] * (1.0 - LAM_INIT)
    o_ref[...] = y.T


def _component_queries(qt):
    zeros = jnp.zeros((HEAD_DIM, qt.shape[1]), qt.dtype)
    return (jnp.concatenate([qt[:HEAD_DIM], zeros], axis=0), jnp.concatenate([zeros, qt[HEAD_DIM:]], axis=0))


def _causal_keep(i, k0, tq, tk):
    kpos = k0 + lax.broadcasted_iota(jnp.int32, (tk, tq), 0)
    qpos = i * tq + lax.broadcasted_iota(jnp.int32, (tk, tq), 1)
    return kpos <= qpos


def _diff_unshifted_kernel(qt_ref, k_ref, vt_ref, lam_ref, sg_ref, o_ref, l_sc, acc_sc, *, tq, tk):
    i = pl.program_id(1)
    qc = _component_queries(qt_ref[...])
    l_sc[...] = jnp.zeros(l_sc.shape, F32)
    acc_sc[...] = jnp.zeros(acc_sc.shape, F32)

    def full_chunk(j, carry):
        k0 = pl.multiple_of(j * tk, tk)
        vt = vt_ref[:, pl.ds(k0, tk)]
        for c in range(2):
            st = jnp.dot(k_ref[pl.ds(k0, tk), :], qc[c], preferred_element_type=F32)
            p = jnp.exp2(st)
            l_sc[c] += jnp.sum(p.reshape(tk // 8, 8, tq), axis=0)
            acc_sc[c] += jnp.dot(vt, p.astype(BF16), preferred_element_type=F32)
        return carry

    lax.fori_loop(0, i, full_chunk, 0)

    sb = DIFF_DIAG_BLOCK
    k0 = pl.multiple_of(i * tk, tk)
    tri = lax.broadcasted_iota(jnp.int32, (sb, sb), 0) <= lax.broadcasted_iota(jnp.int32, (sb, sb), 1)
    for qb in range(tq // sb):
        nk = (qb + 1) * sb
        lanes = slice(qb * sb, (qb + 1) * sb)
        vt = vt_ref[:, pl.ds(k0, nk)]
        for c in range(2):
            st = jnp.dot(k_ref[pl.ds(k0, nk), :], qc[c][:, lanes], preferred_element_type=F32)
            p = jnp.exp2(st)
            p_tri = jnp.where(tri, p[qb * sb:, :], 0.0)
            p = jnp.concatenate([p[:qb * sb, :], p_tri], axis=0) if qb else p_tri
            l_sc[c, :, lanes] += jnp.sum(p.reshape(nk // 8, 8, sb), axis=0)
            acc_sc[c, :, lanes] += jnp.dot(vt, p.astype(BF16), preferred_element_type=F32)

    l0 = jnp.sum(l_sc[0], axis=0, keepdims=True)
    l1 = jnp.sum(l_sc[1], axis=0, keepdims=True)
    _diff_finalize(acc_sc[0] / l0, acc_sc[1] / l1, lam_ref, sg_ref, o_ref)


def _diff_kernel(qt_ref, k_ref, vt_ref, lam_ref, sg_ref, o_ref, m_sc, l_sc, acc_sc, *, tq, tk):
    i = pl.program_id(1)
    qc = _component_queries(qt_ref[...])
    m_sc[...] = jnp.full(m_sc.shape, NEG_INF, F32)
    l_sc[...] = jnp.zeros(l_sc.shape, F32)
    acc_sc[...] = jnp.zeros(acc_sc.shape, F32)

    def chunk(j, masked):
        k0 = pl.multiple_of(j * tk, tk)
        vt = vt_ref[:, pl.ds(k0, tk)]
        for c in range(2):
            st = jnp.dot(k_ref[pl.ds(k0, tk), :], qc[c], preferred_element_type=F32)
            if masked:
                st = jnp.where(_causal_keep(i, k0, tq, tk), st, NEG_INF)
            m_old = m_sc[c]
            m_new = jnp.maximum(m_old, jnp.max(st, axis=0, keepdims=True))
            alpha = jnp.exp2(m_old - m_new)
            p = jnp.exp2(st - m_new)
            l_sc[c] = alpha * l_sc[c] + jnp.sum(p, axis=0, keepdims=True)
            acc_sc[c] = alpha * acc_sc[c] + jnp.dot(vt, p.astype(BF16), preferred_element_type=F32)
            m_sc[c] = m_new

    n_full = (i * tq) // tk

    def full_chunk(j, carry):
        chunk(j, False)
        return carry

    lax.fori_loop(0, n_full, full_chunk, 0)
    for jj in range(tq // tk):
        chunk(n_full + jj, True)

    _diff_finalize(acc_sc[0] / l_sc[0], acc_sc[1] / l_sc[1], lam_ref, sg_ref, o_ref)


def _diff_call(body, scratch, name, qd_t, kd_tok, vd_t, lam_params, subln_col):
    s = qd_t.shape[1]
    tq, tk = DIFF_TQ, DIFF_TK
    assert tq == tk and s % tq == 0 and tq % DIFF_DIAG_BLOCK == 0
    return pl.pallas_call(
        functools.partial(body, tq=tq, tk=tk),
        grid=(DIFF_HEADS, s // tq),
        in_specs=[
            pl.BlockSpec((DIFF_V_DIM, tq), lambda h, i: (h, i)),
            pl.BlockSpec((s, 2 * HEAD_DIM), lambda h, i: (0, h)),
            pl.BlockSpec((DIFF_V_DIM, s), lambda h, i: (h, 0)),
            _const_spec(lam_params.shape),
            _const_spec(subln_col.shape),
        ],
        out_specs=pl.BlockSpec((tq, DIFF_V_DIM), lambda h, i: (i, h)),
        out_shape=jax.ShapeDtypeStruct((s, DIFF_HEADS * DIFF_V_DIM), F32),
        scratch_shapes=scratch(tq),
        compiler_params=_params("parallel", "arbitrary"),
        name=name,
    )(qd_t, kd_tok, vd_t, lam_params, subln_col)


def _diff_attention(qd_t, kd_tok, vd_t, lam_params, subln_col, score_bound):
    acc = lambda tq: pltpu.VMEM((2, DIFF_V_DIM, tq), F32)
    unshifted = functools.partial(
        _diff_call, _diff_unshifted_kernel, lambda tq: [pltpu.VMEM((2, 8, tq), F32), acc(tq)],
        "diff_attention_unshifted")
    online = functools.partial(
        _diff_call, _diff_kernel, lambda tq: [pltpu.VMEM((2, 1, tq), F32), pltpu.VMEM((2, 1, tq), F32), acc(tq)],
        "diff_attention_online")
    return lax.cond(score_bound <= DIFF_UNSHIFTED_SCORE_BOUND, unshifted, online,
                    qd_t, kd_tok, vd_t, lam_params, subln_col)


def _att_out_kernel(x_ref, acc_ref, l_ref, xexp_ref, od_ref, g_ref, w_ref, o_ref, tok_sc, *, dil):
    n_lane_tiles = SPLIT // LANES
    rows = tok_sc.shape[1] // dil
    for r in range(dil):
        dst = pl.ds(r, rows, stride=dil)
        for c in range(n_lane_tiles):
            col0 = r * SPLIT + c * LANES
            tok_sc[c, dst, :] = acc_ref[0, :, col0:col0 + LANES]
        tok_sc[n_lane_tiles, dst, :] = l_ref[0, :, r * LANES:(r + 1) * LANES]
    acc = jnp.concatenate([tok_sc[c] for c in range(n_lane_tiles)], axis=1)
    r = 1.0 / tok_sc[n_lane_tiles]
    r_hi = r.astype(BF16)
    r_lo = (r - r_hi.astype(F32)).astype(BF16)
    r_exp = (jnp.dot(r_hi, xexp_ref[...], preferred_element_type=F32)
             + jnp.dot(r_lo, xexp_ref[...], preferred_element_type=F32))
    g = g_ref[...]
    y = jnp.concatenate([acc * r_exp, od_ref[...]], axis=-1) * (g * jax.nn.sigmoid(g))
    o_ref[...] = x_ref[...] + jnp.dot(y.astype(BF16), w_ref[...], preferred_element_type=F32)


def _att_out(x2d, acc_view, l_view, od, g, w_out):
    s, d = x2d.shape
    tm = ROW_TILE
    dil = DSW_PATTERNS[-1][1]
    rows = tm // dil
    per_blk = BLK // rows
    assert acc_view.shape == (s // (BLK * dil), BLK, dil * SPLIT) and tm % dil == 0 and BLK % rows == 0
    assert l_view.shape == (s // (BLK * dil), BLK, dil * LANES)
    _, xexp = _dsw_constants()
    row = lambda width: pl.BlockSpec((tm, width), lambda i: (i, 0))
    view_spec = lambda width: pl.BlockSpec((1, rows, dil * width), lambda i: (i // per_blk, i % per_blk, 0))
    return pl.pallas_call(
        functools.partial(_att_out_kernel, dil=dil),
        grid=(s // tm,),
        in_specs=[row(d), view_spec(SPLIT), view_spec(LANES), _const_spec(xexp.shape), row(SPLIT), row(2 * SPLIT),
                  _const_spec(w_out.shape)],
        out_specs=row(d),
        out_shape=jax.ShapeDtypeStruct((s, d), F32),
        scratch_shapes=[pltpu.VMEM((SPLIT // LANES + 1, tm, LANES), F32)],
        compiler_params=_params("parallel"),
        name="att_out_proj",
    )(x2d, acc_view, l_view, xexp, od, g, w_out.astype(BF16))


def _gelu(x):
    return 0.5 * x * (1.0 + lax.erf(x * (2.0 ** -0.5)))


def _sgu_kernel(x_ref, ng_ref, win_ref, lng_ref, lnb_ref, ws_ref, bs_ref, wout_ref, o_ref, *, width):
    x = x_ref[...]
    tm = x.shape[0]
    ms = jnp.mean(x * x, axis=-1, keepdims=True)
    h = (x * lax.rsqrt(ms + RMS_EPS) * ng_ref[...]).astype(BF16)
    v = _gelu(jnp.dot(h, win_ref[:, width:2 * width], preferred_element_type=F32))
    mu = jnp.mean(v, axis=-1, keepdims=True)
    vc = v - mu
    vn = vc * lax.rsqrt(jnp.mean(vc * vc, axis=-1, keepdims=True) + LN_EPS)
    vn = (vn * lng_ref[...] + lnb_ref[...]).astype(BF16)
    gw = width // SGU_GROUPS
    row = lax.broadcasted_iota(jnp.int32, (SGU_CHUNK, SGU_CHUNK), 0)
    col = lax.broadcasted_iota(jnp.int32, (SGU_CHUNK, SGU_CHUNK), 1)
    causal = col <= row
    sp_groups = []
    for grp in range(SGU_GROUPS):
        ws = jnp.where(causal, ws_ref[grp], 0.0).astype(BF16)
        bias = bs_ref[grp]
        chunks = []
        for c in range(tm // SGU_CHUNK):
            vg = vn[c * SGU_CHUNK:(c + 1) * SGU_CHUNK, grp * gw:(grp + 1) * gw]
            chunks.append(jnp.dot(ws, vg, preferred_element_type=F32) + bias)
        sp_groups.append(jnp.concatenate(chunks, axis=0))
    sp = jnp.concatenate(sp_groups, axis=1)
    u = _gelu(jnp.dot(h, win_ref[:, 0:width], preferred_element_type=F32))
    g = jnp.dot(h, win_ref[:, 2 * width:3 * width], preferred_element_type=F32)
    y = u * sp * (g * jax.nn.sigmoid(g))
    o_ref[...] = x + jnp.dot(y.astype(BF16), wout_ref[...], preferred_element_type=F32)


def _sgu_layer(x2d, norm_g, w_in, ln_g, ln_b, w_s, b_s, w_out):
    s, d = x2d.shape
    width = w_out.shape[0]
    tm = ROW_TILE
    row = pl.BlockSpec((tm, d), lambda i: (i, 0))
    single = lambda shape: pl.BlockSpec(shape, lambda *_: (0,) * len(shape), pipeline_mode=pl.Buffered(1))
    return pl.pallas_call(
        functools.partial(_sgu_kernel, width=width),
        grid=(s // tm,),
        in_specs=[row, _const_spec((1, d)), single(w_in.shape), _const_spec((1, width)), _const_spec((1, width)),
                  _const_spec(w_s.shape), _const_spec((SGU_GROUPS, SGU_CHUNK, 1)), single(w_out.shape)],
        out_specs=row,
        out_shape=jax.ShapeDtypeStruct((s, d), F32),
        compiler_params=_params("parallel"),
        name="sgu_layer",
    )(x2d, norm_g.reshape(1, d), w_in.astype(BF16), ln_g.reshape(1, width), ln_b.reshape(1, width),
      w_s, b_s.reshape(SGU_GROUPS, SGU_CHUNK, 1), w_out.astype(BF16))


def _rope_tables_fm(seq_len):
    pos = jnp.arange(seq_len, dtype=F32)
    inv = 1.0 / (ROPE_THETA ** (jnp.arange(0, 2 * ROT_HALF, 2, dtype=F32) / (2 * ROT_HALF)))
    ang = inv[:, None] * pos[None, :]
    return jnp.cos(ang), jnp.sin(ang)


def kernel(x, att_norm, att_w_in, dsw_q_norm, dsw_k_norm, diff_q_norm, diff_k_norm, diff_lam_q1, diff_lam_k1,
           diff_lam_q2, diff_lam_k2, diff_subln, att_w_out, sgu_norm, sgu_w_in, sgu_ln_g, sgu_ln_b, sgu_w_s,
           sgu_b_s, sgu_w_out):
    b, s, d = x.shape
    assert b == 1 and s % (BLK * DSW_PATTERNS[-1][1]) == 0
    x2d = x.reshape(s, d)
    cos_t, sin_t = _rope_tables_fm(s)

    gains = jnp.stack([dsw_q_norm[0], dsw_k_norm[0], diff_q_norm[0], diff_k_norm[0]])
    gains = jnp.broadcast_to(gains[:, None, :, None], (4, N_GROUPS64, HEAD_DIM, 1))
    qa_views, ka_views, va_views, g, kd_tok, qd_t, vd_t = _att_in(x2d, att_norm[0], att_w_in[0], gains, cos_t, sin_t)
    bound = lambda gq, gk: 1.02 * LOG2_E * HEAD_DIM ** 0.5 * jnp.max(jnp.abs(gq)) * jnp.max(jnp.abs(gk))
    oa_acc, oa_l = _dsw_attention(qa_views, ka_views, va_views, bound(dsw_q_norm[0], dsw_k_norm[0]))
    lam_params = jnp.stack([diff_lam_q1[0], diff_lam_k1[0], diff_lam_q2[0], diff_lam_k2[0]])
    od = _diff_attention(qd_t, kd_tok, vd_t, lam_params, diff_subln[0].reshape(DIFF_V_DIM, 1),
                         bound(diff_q_norm[0], diff_k_norm[0]))
    x2d = _att_out(x2d, oa_acc, oa_l, od, g, att_w_out[0])

    x2d = _sgu_layer(x2d, sgu_norm[0], sgu_w_in[0], sgu_ln_g[0], sgu_ln_b[0], sgu_w_s[0], sgu_b_s[0], sgu_w_out[0])
    return x2d.reshape(b, s, d)
```

```python
import functools
import math

import numpy as np
import jax
import jax.numpy as jnp
from jax import lax
from jax.experimental import pallas as pl
from jax.experimental.pallas import tpu as pltpu

F32 = jnp.float32
BF16 = jnp.bfloat16

LANES = 128
HEAD_DIM = 64
ROT_HALF = 8
ROPE_THETA = 500000.0
BLK = 128
N_GROUPS64 = 8
SPLIT = N_GROUPS64 * HEAD_DIM
DSW_PATTERNS = ((128, 1), (512, 4), (2048, 16))
DSW_DIL_RATIO = 4
DIFF_HEADS = 4
DIFF_V_DIM = 128
SGU_GROUPS = 8
SGU_CHUNK = 128
RMS_EPS = 1e-6
LN_EPS = 1e-5
NEG_INF = -1e30
LAM_INIT = 0.8 - 0.6 * math.exp(-0.3 * 0)
LOG2_E = math.log2(math.e)
DIFF_UNSHIFTED_SCORE_BOUND = 96.0
DSW_UNSHIFTED_SCORE_BOUND = 96.0

V7X_VMEM_LIMIT_BYTES = 56 * 1024 * 1024

ROW_TILE = 512
DSW_BLOCKS_PER_STEP = 8
DIFF_TQ = 1024
DIFF_TK = 1024
DIFF_DIAG_BLOCK = 512


def _params(*sem):
    return pltpu.CompilerParams(dimension_semantics=sem, vmem_limit_bytes=V7X_VMEM_LIMIT_BYTES)


def _const_spec(shape):
    nd = len(shape)
    return pl.BlockSpec(shape, lambda *_: (0,) * nd)


def _norm_rope_fm(p, gain, cos, sin, scale):
    t = p.shape[1]
    p3 = p.reshape(N_GROUPS64, HEAD_DIM, t)
    ms = jnp.mean(p3 * p3, axis=1, keepdims=True)
    y = p3 * lax.rsqrt(ms + RMS_EPS) * gain
    x1 = y[:, 0:ROT_HALF, :]
    x2 = y[:, ROT_HALF:2 * ROT_HALF, :]
    out = jnp.concatenate([x1 * cos - x2 * sin, x2 * cos + x1 * sin, y[:, 2 * ROT_HALF:, :]], axis=1)
    if scale != 1.0:
        out = out * scale
    return out.reshape(SPLIT, t)


def _emit_pattern_views(tok, tok_sc, view_refs):
    t = tok.shape[0]
    n_lane_tiles = SPLIT // LANES
    for c in range(n_lane_tiles):
        tok_sc[c] = tok[:, c * LANES:(c + 1) * LANES]
    for (_, dil), ref in zip(DSW_PATTERNS, view_refs):
        if dil == 1:
            ref[...] = tok.astype(BF16)
            continue
        for r in range(dil):
            rows = jnp.concatenate([tok_sc[c, pl.ds(r, t // dil, stride=dil), :] for c in range(n_lane_tiles)],
                                   axis=1)
            ref[0, :, r * SPLIT:(r + 1) * SPLIT] = rows.astype(BF16)


def _att_in_kernel(x_ref, ng_ref, wtok_ref, wfm_ref, gains_ref, cos_ref, sin_ref, *refs):
    n_pat = len(DSW_PATTERNS)
    qa_views, ka_views, va_views = refs[0:n_pat], refs[n_pat:2 * n_pat], refs[2 * n_pat:3 * n_pat]
    g_ref, kd_ref, qd_ref, vd_ref, tok_sc = refs[3 * n_pat:]
    x = x_ref[...]
    ms = jnp.mean(x * x, axis=-1, keepdims=True)
    h = (x * lax.rsqrt(ms + RMS_EPS) * ng_ref[...]).astype(BF16)
    tok = jnp.dot(h, wtok_ref[...], preferred_element_type=F32)
    _emit_pattern_views(tok[:, :SPLIT], tok_sc, va_views)
    g_ref[...] = tok[:, SPLIT:]
    cos = cos_ref[...][None]
    sin = sin_ref[...][None]
    scales = (HEAD_DIM ** -0.5 * LOG2_E, 1.0, HEAD_DIM ** -0.5 * LOG2_E, 1.0)
    normed = []
    for idx in range(4):
        p = lax.dot_general(wfm_ref[idx * SPLIT:(idx + 1) * SPLIT, :], h, (((1,), (1,)), ((), ())),
                            preferred_element_type=F32)
        normed.append(_norm_rope_fm(p, gains_ref[idx], cos, sin, scales[idx]))
    _emit_pattern_views(normed[0].T, tok_sc, qa_views)
    _emit_pattern_views(normed[1].T, tok_sc, ka_views)
    qd_ref[...] = normed[2].astype(BF16)
    kd_ref[...] = normed[3].T.astype(BF16)
    p = lax.dot_general(wfm_ref[4 * SPLIT:5 * SPLIT, :], h, (((1,), (1,)), ((), ())),
                        preferred_element_type=F32)
    vd_ref[...] = p.astype(BF16)


def _att_in(x2d, norm_g, w_in, gains, cos_t, sin_t):
    s, d = x2d.shape
    tm = ROW_TILE
    w = w_in.astype(BF16)
    w_tok = jnp.concatenate([w[:, 2 * SPLIT:4 * SPLIT], w[:, 7 * SPLIT:8 * SPLIT]], axis=1)
    w_fm = jnp.concatenate([w[:, 0:2 * SPLIT], w[:, 4 * SPLIT:7 * SPLIT]], axis=1).T
    view_shapes, view_specs = [], []
    for _, dil in DSW_PATTERNS:
        if dil == 1:
            view_shapes.append(jax.ShapeDtypeStruct((s, SPLIT), BF16))
            view_specs.append(pl.BlockSpec((tm, SPLIT), lambda i: (i, 0)))
            continue
        rows = tm // dil
        assert tm % dil == 0 and BLK % rows == 0 and rows % 16 == 0
        per_blk = BLK // rows
        view_shapes.append(jax.ShapeDtypeStruct((s // (BLK * dil), BLK, dil * SPLIT), BF16))
        view_specs.append(pl.BlockSpec((1, rows, dil * SPLIT),
                                       functools.partial(lambda i, per_blk: (i // per_blk, i % per_blk, 0),
                                                         per_blk=per_blk)))
    fm = jax.ShapeDtypeStruct((SPLIT, s), BF16)
    fm_spec = pl.BlockSpec((SPLIT, tm), lambda i: (0, i))
    outs = pl.pallas_call(
        _att_in_kernel,
        grid=(s // tm,),
        in_specs=[
            pl.BlockSpec((tm, d), lambda i: (i, 0)),
            _const_spec((1, d)),
            _const_spec(w_tok.shape),
            _const_spec(w_fm.shape),
            _const_spec(gains.shape),
            pl.BlockSpec((ROT_HALF, tm), lambda i: (0, i)),
            pl.BlockSpec((ROT_HALF, tm), lambda i: (0, i)),
        ],
        out_specs=view_specs * 3 + [
            pl.BlockSpec((tm, 2 * SPLIT), lambda i: (i, 0)),
            pl.BlockSpec((tm, SPLIT), lambda i: (i, 0)),
            fm_spec, fm_spec,
        ],
        out_shape=view_shapes * 3 + [
            jax.ShapeDtypeStruct((s, 2 * SPLIT), F32),
            jax.ShapeDtypeStruct((s, SPLIT), BF16),
            fm, fm,
        ],
        scratch_shapes=[pltpu.VMEM((SPLIT // LANES, tm, LANES), F32)],
        compiler_params=_params("parallel"),
        name="att_in_proj",
    )(x2d, norm_g.reshape(1, d), w_tok, w_fm, gains, cos_t, sin_t)
    n_pat = len(DSW_PATTERNS)
    return (outs[0:n_pat], outs[n_pat:2 * n_pat], outs[2 * n_pat:3 * n_pat]) + tuple(outs[3 * n_pat:])


def _dsw_kernel(*refs, has_state, blocks_per_step):
    if has_state:
        (q_ref, kprev_ref, kcur_ref, vprev_ref, vcur_ref, oin_ref, lsein_ref,
         o_ref, lse_ref, kall, vall) = refs
    else:
        (q_ref, kprev_ref, kcur_ref, vprev_ref, vcur_ref, o_ref, lse_ref, kall, vall) = refs
        oin_ref = lsein_ref = None
    g_blocks = blocks_per_step
    step = pl.program_id(1)
    kall[0:BLK, :] = kprev_ref[0]
    vall[0:BLK, :] = vprev_ref[0]
    kall[BLK:, :] = kcur_ref[...].reshape(g_blocks * BLK, SPLIT)
    vall[BLK:, :] = vcur_ref[...].reshape(g_blocks * BLK, SPLIT)

    qi = lax.broadcasted_iota(jnp.int32, (BLK, 2 * BLK), 0) + BLK
    kj = lax.broadcasted_iota(jnp.int32, (BLK, 2 * BLK), 1)
    dist = qi - kj
    band = (dist >= 0) & (dist <= BLK)
    lane = lax.broadcasted_iota(jnp.int32, (BLK, 2 * HEAD_DIM), 1)
    low_half = lane < HEAD_DIM

    def block(g, carry):
        first_key = jnp.where((step * g_blocks + g) == 0, BLK, 0)
        mask = band & (kj >= first_key)
        row0 = pl.multiple_of(g * BLK, BLK)
        lse_tile = jnp.zeros((BLK, BLK), F32)
        lse_in = lsein_ref[g] if has_state else None
        for hp in range(N_GROUPS64 // 2):
            cols = slice(hp * 2 * HEAD_DIM, (hp + 1) * 2 * HEAD_DIM)
            q_pair = q_ref[g, :, cols]
            kk = kall[pl.ds(row0, 2 * BLK), cols]
            vv = vall[pl.ds(row0, 2 * BLK), cols]
            o_in = oin_ref[g, :, cols] if has_state else None
            halves = []
            for e in range(2):
                head = 2 * hp + e
                sel = low_half if e == 0 else jnp.logical_not(low_half)
                qm = jnp.where(sel, q_pair, jnp.zeros_like(q_pair))
                sc = lax.dot_general(qm, kk, (((1,), (1,)), ((), ())), preferred_element_type=F32)
                sc = jnp.where(mask, sc, NEG_INF)
                m_blk = jnp.max(sc, axis=-1, keepdims=True)
                if has_state:
                    m_old = lse_in[:, head:head + 1]
                    m_new = jnp.maximum(m_old, m_blk)
                    alpha = jnp.exp2(m_old - m_new)
                else:
                    m_new = m_blk
                p = jnp.exp2(sc - m_new)
                l = jnp.sum(p, axis=-1, keepdims=True)
                pv = jnp.dot(p.astype(BF16), vv, preferred_element_type=F32)
                if has_state:
                    l = l + alpha
                    pv = pv + alpha * o_in
                halves.append(pv / l)
                lse_tile = jnp.where(lane == head, m_new + jnp.log2(l), lse_tile)
            o_ref[g, :, cols] = jnp.where(low_half, halves[0], halves[1])
        lse_ref[g] = lse_tile
        return carry

    lax.fori_loop(0, g_blocks, block, 0)


def _dsw_pattern(q, k, v, dil, state):
    s = q.size // SPLIT
    nb = s // (BLK * dil)
    g_blocks = min(DSW_BLOCKS_PER_STEP, nb)
    view = lambda a, width: a.reshape(nb, BLK, dil * width)
    cur_spec = lambda width: pl.BlockSpec((g_blocks, BLK, width), lambda r, j: (j, 0, r))
    prev_spec = pl.BlockSpec((1, BLK, SPLIT), lambda r, j: (jnp.maximum(j * g_blocks - 1, 0), 0, r))
    operands = [view(q, SPLIT), view(k, SPLIT), view(k, SPLIT), view(v, SPLIT), view(v, SPLIT)]
    in_specs = [cur_spec(SPLIT), prev_spec, cur_spec(SPLIT), prev_spec, cur_spec(SPLIT)]
    if state is not None:
        operands += [view(state[0], SPLIT), view(state[1], BLK)]
        in_specs += [cur_spec(SPLIT), cur_spec(BLK)]
    o, lse = pl.pallas_call(
        functools.partial(_dsw_kernel, has_state=state is not None, blocks_per_step=g_blocks),
        grid=(dil, nb // g_blocks),
        in_specs=in_specs,
        out_specs=[cur_spec(SPLIT), cur_spec(BLK)],
        out_shape=[jax.ShapeDtypeStruct((nb, BLK, dil * SPLIT), F32),
                   jax.ShapeDtypeStruct((nb, BLK, dil * BLK), F32)],
        scratch_shapes=[pltpu.VMEM(((g_blocks + 1) * BLK, SPLIT), BF16),
                        pltpu.VMEM(((g_blocks + 1) * BLK, SPLIT), BF16)],
        compiler_params=_params("parallel", "arbitrary"),
        name=f"dsw_attention_dil{dil}",
    )(*operands)
    return o.reshape(s, SPLIT), lse.reshape(s, BLK)


def _dsw_unshifted_kernel(*refs, has_state, last, blocks_per_step):
    refs = list(refs)
    q_ref, kprev_ref, kcur_ref, vprev_ref, vcur_ref, ehot_ref = refs[:6]
    pos = 6
    if has_state:
        accin_ref, lin_ref = refs[pos:pos + 2]
        pos += 2
    acc_ref, l_ref, kall, vall = refs[pos:pos + 4]
    perm_sc = None if last else refs[pos + 4]
    g_blocks = blocks_per_step
    step = pl.program_id(1)
    kall[0:BLK, :] = kprev_ref[0]
    vall[0:BLK, :] = vprev_ref[0]
    kall[BLK:, :] = kcur_ref[...].reshape(g_blocks * BLK, SPLIT)
    vall[BLK:, :] = vcur_ref[...].reshape(g_blocks * BLK, SPLIT)

    qi = (lax.broadcasted_iota(jnp.int32, (2 * BLK, 2 * BLK), 0) & (BLK - 1)) + BLK
    kj = lax.broadcasted_iota(jnp.int32, (2 * BLK, 2 * BLK), 1)
    dist = qi - kj
    band = (dist >= 0) & (dist <= BLK)
    q_low = lax.broadcasted_iota(jnp.int32, (BLK, 2 * HEAD_DIM), 1) < HEAD_DIM
    v_low = lax.broadcasted_iota(jnp.int32, (2 * BLK, 2 * HEAD_DIM), 1) < HEAD_DIM

    def block(g, carry):
        first_key = jnp.where((step * g_blocks + g) == 0, BLK, 0)
        mask = band & (kj >= first_key)
        row0 = pl.multiple_of(g * BLK, BLK)
        l_tile = lin_ref[g] if has_state else jnp.zeros((BLK, BLK), F32)
        for hp in range(N_GROUPS64 // 2):
            cols = slice(hp * 2 * HEAD_DIM, (hp + 1) * 2 * HEAD_DIM)
            q_pair = q_ref[g, :, cols]
            kk = kall[pl.ds(row0, 2 * BLK), cols]
            vv = vall[pl.ds(row0, 2 * BLK), cols]
            qz = jnp.zeros_like(q_pair)
            vz = jnp.zeros_like(vv)
            q2 = jnp.concatenate([jnp.where(q_low, q_pair, qz), jnp.where(q_low, qz, q_pair)], axis=0)
            sc = lax.dot_general(q2, kk, (((1,), (1,)), ((), ())), preferred_element_type=F32)
            p2 = jnp.where(mask, jnp.exp2(sc), 0.0).astype(BF16)
            p_cat = jnp.concatenate([p2[:BLK], p2[BLK:]], axis=1)
            v_split = jnp.concatenate([jnp.where(v_low, vv, vz), jnp.where(v_low, vz, vv)], axis=0)
            rhs = jnp.concatenate([v_split, ehot_ref[hp]], axis=1)
            out2 = jnp.dot(p_cat, rhs, preferred_element_type=F32)
            acc = out2[:, :2 * HEAD_DIM]
            if has_state:
                acc = acc + accin_ref[g, :, cols]
            l_tile = l_tile + out2[:, 2 * HEAD_DIM:]
            if last:
                acc_ref[g, :, cols] = acc
            else:
                perm_sc[hp] = acc
        if last:
            l_ref[g] = l_tile
        else:
            perm_sc[N_GROUPS64 // 2] = l_tile
            ratio = DSW_DIL_RATIO
            rows = BLK // ratio
            out_blk = lax.shift_right_logical(g, ratio.bit_length() - 1)
            row_dst = pl.ds(pl.multiple_of((g & (ratio - 1)) * rows, rows), rows)
            for m in range(ratio):
                src = pl.ds(m, rows, stride=ratio)
                for hp in range(N_GROUPS64 // 2):
                    col0 = m * SPLIT + hp * 2 * HEAD_DIM
                    acc_ref[out_blk, row_dst, col0:col0 + 2 * HEAD_DIM] = perm_sc[hp, src, :]
                l_ref[out_blk, row_dst, m * BLK:(m + 1) * BLK] = perm_sc[N_GROUPS64 // 2, src, :]
        return carry

    lax.fori_loop(0, g_blocks, block, 0, unroll=True)


def _dsw_residue_of(hop):
    if hop == 0:
        return lambda g: g
    prev = _dsw_residue_of(hop - 1)
    prev_dil = DSW_PATTERNS[hop - 1][1]
    return lambda g: prev_dil * (g % DSW_DIL_RATIO) + prev(g // DSW_DIL_RATIO)


def _dsw_unshifted_pattern(q, k, v, hop, state, ehot):
    dil = DSW_PATTERNS[hop][1]
    last = hop == len(DSW_PATTERNS) - 1
    s = q.size // SPLIT
    nb = s // (BLK * dil)
    g_blocks = min(DSW_BLOCKS_PER_STEP, nb)
    ratio = DSW_DIL_RATIO
    residue = _dsw_residue_of(hop)
    view = lambda a: a.reshape(nb, BLK, dil * SPLIT)
    qkv_spec = pl.BlockSpec((g_blocks, BLK, SPLIT), lambda r, j: (j, 0, residue(r)))
    prev_spec = pl.BlockSpec((1, BLK, SPLIT), lambda r, j: (jnp.maximum(j * g_blocks - 1, 0), 0, residue(r)))
    state_spec = lambda width: pl.BlockSpec((g_blocks, BLK, width), lambda r, j: (j, 0, r))
    operands = [view(q), view(k), view(k), view(v), view(v), ehot]
    in_specs = [qkv_spec, prev_spec, qkv_spec, prev_spec, qkv_spec, _const_spec(ehot.shape)]
    if state is not None:
        operands += list(state)
        in_specs += [state_spec(SPLIT), state_spec(BLK)]
    scratch = [pltpu.VMEM(((g_blocks + 1) * BLK, SPLIT), BF16), pltpu.VMEM(((g_blocks + 1) * BLK, SPLIT), BF16)]
    if last:
        own_spec = lambda width: pl.BlockSpec((g_blocks, BLK, width), lambda r, j: (j, 0, residue(r)))
        out_specs = [own_spec(SPLIT), own_spec(BLK)]
        out_shape = [jax.ShapeDtypeStruct((nb, BLK, dil * SPLIT), F32),
                     jax.ShapeDtypeStruct((nb, BLK, dil * BLK), F32)]
    else:
        assert DSW_PATTERNS[hop + 1][1] == dil * ratio and g_blocks % ratio == 0
        next_spec = lambda width: pl.BlockSpec((g_blocks // ratio, BLK, ratio * width), lambda r, j: (j, 0, r))
        out_specs = [next_spec(SPLIT), next_spec(BLK)]
        out_shape = [jax.ShapeDtypeStruct((nb // ratio, BLK, dil * ratio * SPLIT), F32),
                     jax.ShapeDtypeStruct((nb // ratio, BLK, dil * ratio * BLK), F32)]
        scratch.append(pltpu.VMEM((N_GROUPS64 // 2 + 1, BLK, 2 * HEAD_DIM), F32))
    out = pl.pallas_call(
        functools.partial(_dsw_unshifted_kernel, has_state=state is not None, last=last, blocks_per_step=g_blocks),
        grid=(dil, nb // g_blocks),
        in_specs=in_specs,
        out_specs=out_specs,
        out_shape=out_shape,
        scratch_shapes=scratch,
        compiler_params=_params("parallel", "arbitrary"),
        name=f"dsw_unshifted_dil{dil}",
    )(*operands)
    return tuple(out)


DSW_L_LANES_PER_HEAD = BLK // N_GROUPS64


def _dsw_constants():
    ehot = np.zeros((N_GROUPS64 // 2, 4 * BLK, BLK), np.float32)
    xexp = np.zeros((BLK, SPLIT), np.float32)
    for head in range(N_GROUPS64):
        hp, e = divmod(head, 2)
        lanes = slice(head * DSW_L_LANES_PER_HEAD, (head + 1) * DSW_L_LANES_PER_HEAD)
        ehot[hp, e * 2 * BLK:(e + 1) * 2 * BLK, lanes] = 1.0
        xexp[head * DSW_L_LANES_PER_HEAD, head * HEAD_DIM:(head + 1) * HEAD_DIM] = 1.0
    return jnp.asarray(ehot, BF16), jnp.asarray(xexp, BF16)


def _dsw_attention_unshifted(q_views, k_views, v_views):
    ehot, _ = _dsw_constants()
    state = None
    for hop in range(len(DSW_PATTERNS)):
        state = _dsw_unshifted_pattern(q_views[hop], k_views[hop], v_views[hop], hop, state, ehot)
    return state


def _dsw_attention_online(q_views, k_views, v_views):
    state = None
    for idx, (_, dil) in enumerate(DSW_PATTERNS):
        state = _dsw_pattern(q_views[idx], k_views[idx], v_views[idx], dil, state)
    o_view = state[0].reshape(q_views[-1].shape)
    return o_view, jnp.ones(o_view.shape[:2] + (o_view.shape[2] // SPLIT * BLK,), F32)


def _dsw_attention(q_views, k_views, v_views, score_bound):
    return lax.cond(score_bound <= DSW_UNSHIFTED_SCORE_BOUND, _dsw_attention_unshifted, _dsw_attention_online,
                    q_views, k_views, v_views)


def _diff_finalize(o0, o1, lam_ref, sg_ref, o_ref):
    lam_p = lam_ref[...]
    lam = (jnp.exp(jnp.sum(lam_p[0:1] * lam_p[1:2], axis=-1, keepdims=True))
           - jnp.exp(jnp.sum(lam_p[2:3] * lam_p[3:4], axis=-1, keepdims=True)) + LAM_INIT)
    od = o0 - lam * o1
    ms = jnp.mean(od * od, axis=0, keepdims=True)
    y = od * lax.rsqrt(ms + RMS_EPS) * sg_ref[...] * (1.0 - LAM_INIT)
    o_ref[...] = y.T


def _component_queries(qt):
    zeros = jnp.zeros((HEAD_DIM, qt.shape[1]), qt.dtype)
    return (jnp.concatenate([qt[:HEAD_DIM], zeros], axis=0), jnp.concatenate([zeros, qt[HEAD_DIM:]], axis=0))


def _causal_keep(i, k0, tq, tk):
    kpos = k0 + lax.broadcasted_iota(jnp.int32, (tk, tq), 0)
    qpos = i * tq + lax.broadcasted_iota(jnp.int32, (tk, tq), 1)
    return kpos <= qpos


def _diff_unshifted_kernel(qt_ref, k_ref, vt_ref, lam_ref, sg_ref, o_ref, l_sc, acc_sc, *, tq, tk):
    i = pl.program_id(1)
    qc = _component_queries(qt_ref[...])
    l_sc[...] = jnp.zeros(l_sc.shape, F32)
    acc_sc[...] = jnp.zeros(acc_sc.shape, F32)

    def full_chunk(j):
        k0 = pl.multiple_of(j * tk, tk)
        vt = vt_ref[:, pl.ds(k0, tk)]
        for c in range(2):
            st = jnp.dot(k_ref[pl.ds(k0, tk), :], qc[c], preferred_element_type=F32)
            p = jnp.exp2(st)
            l_sc[c] += jnp.sum(p.reshape(tk // 8, 8, tq), axis=0)
            acc_sc[c] += jnp.dot(vt, p.astype(BF16), preferred_element_type=F32)

    def chunk_pair(jp, carry):
        full_chunk(2 * jp)
        full_chunk(2 * jp + 1)
        return carry

    lax.fori_loop(0, lax.shift_right_logical(i, 1), chunk_pair, 0)

    @pl.when((i & 1) == 1)
    def _():
        full_chunk(i - 1)

    sb = DIFF_DIAG_BLOCK
    k0 = pl.multiple_of(i * tk, tk)
    tri = lax.broadcasted_iota(jnp.int32, (sb, sb), 0) <= lax.broadcasted_iota(jnp.int32, (sb, sb), 1)
    for qb in range(tq // sb):
        nk = (qb + 1) * sb
        lanes = slice(qb * sb, (qb + 1) * sb)
        vt = vt_ref[:, pl.ds(k0, nk)]
        for c in range(2):
            st = jnp.dot(k_ref[pl.ds(k0, nk), :], qc[c][:, lanes], preferred_element_type=F32)
            p = jnp.exp2(st)
            p_tri = jnp.where(tri, p[qb * sb:, :], 0.0)
            p = jnp.concatenate([p[:qb * sb, :], p_tri], axis=0) if qb else p_tri
            l_sc[c, :, lanes] += jnp.sum(p.reshape(nk // 8, 8, sb), axis=0)
            acc_sc[c, :, lanes] += jnp.dot(vt, p.astype(BF16), preferred_element_type=F32)

    l0 = jnp.sum(l_sc[0], axis=0, keepdims=True)
    l1 = jnp.sum(l_sc[1], axis=0, keepdims=True)
    _diff_finalize(acc_sc[0] / l0, acc_sc[1] / l1, lam_ref, sg_ref, o_ref)


def _diff_kernel(qt_ref, k_ref, vt_ref, lam_ref, sg_ref, o_ref, m_sc, l_sc, acc_sc, *, tq, tk):
    i = pl.program_id(1)
    qc = _component_queries(qt_ref[...])
    m_sc[...] = jnp.full(m_sc.shape, NEG_INF, F32)
    l_sc[...] = jnp.zeros(l_sc.shape, F32)
    acc_sc[...] = jnp.zeros(acc_sc.shape, F32)

    def chunk(j, masked):
        k0 = pl.multiple_of(j * tk, tk)
        vt = vt_ref[:, pl.ds(k0, tk)]
        for c in range(2):
            st = jnp.dot(k_ref[pl.ds(k0, tk), :], qc[c], preferred_element_type=F32)
            if masked:
                st = jnp.where(_causal_keep(i, k0, tq, tk), st, NEG_INF)
            m_old = m_sc[c]
            m_new = jnp.maximum(m_old, jnp.max(st, axis=0, keepdims=True))
            alpha = jnp.exp2(m_old - m_new)
            p = jnp.exp2(st - m_new)
            l_sc[c] = alpha * l_sc[c] + jnp.sum(p, axis=0, keepdims=True)
            acc_sc[c] = alpha * acc_sc[c] + jnp.dot(vt, p.astype(BF16), preferred_element_type=F32)
            m_sc[c] = m_new

    n_full = (i * tq) // tk

    def full_chunk(j, carry):
        chunk(j, False)
        return carry

    lax.fori_loop(0, n_full, full_chunk, 0)
    for jj in range(tq // tk):
        chunk(n_full + jj, True)

    _diff_finalize(acc_sc[0] / l_sc[0], acc_sc[1] / l_sc[1], lam_ref, sg_ref, o_ref)


def _diff_call(body, scratch, name, qd_t, kd_tok, vd_t, lam_params, subln_col):
    s = qd_t.shape[1]
    tq, tk = DIFF_TQ, DIFF_TK
    assert tq == tk and s % tq == 0 and tq % DIFF_DIAG_BLOCK == 0
    return pl.pallas_call(
        functools.partial(body, tq=tq, tk=tk),
        grid=(DIFF_HEADS, s // tq),
        in_specs=[
            pl.BlockSpec((DIFF_V_DIM, tq), lambda h, i: (h, i)),
            pl.BlockSpec((s, 2 * HEAD_DIM), lambda h, i: (0, h)),
            pl.BlockSpec((DIFF_V_DIM, s), lambda h, i: (h, 0)),
            _const_spec(lam_params.shape),
            _const_spec(subln_col.shape),
        ],
        out_specs=pl.BlockSpec((tq, DIFF_V_DIM), lambda h, i: (i, h)),
        out_shape=jax.ShapeDtypeStruct((s, DIFF_HEADS * DIFF_V_DIM), F32),
        scratch_shapes=scratch(tq),
        compiler_params=_params("parallel", "arbitrary"),
        name=name,
    )(qd_t, kd_tok, vd_t, lam_params, subln_col)


def _diff_attention(qd_t, kd_tok, vd_t, lam_params, subln_col, score_bound):
    acc = lambda tq: pltpu.VMEM((2, DIFF_V_DIM, tq), F32)
    unshifted = functools.partial(
        _diff_call, _diff_unshifted_kernel, lambda tq: [pltpu.VMEM((2, 8, tq), F32), acc(tq)],
        "diff_attention_unshifted")
    online = functools.partial(
        _diff_call, _diff_kernel, lambda tq: [pltpu.VMEM((2, 1, tq), F32), pltpu.VMEM((2, 1, tq), F32), acc(tq)],
        "diff_attention_online")
    return lax.cond(score_bound <= DIFF_UNSHIFTED_SCORE_BOUND, unshifted, online,
                    qd_t, kd_tok, vd_t, lam_params, subln_col)


def _att_out_compute(x_ref, acc_ref, l_ref, xexp_ref, od_ref, g_ref, w_ref, tok_sc, dil):
    n_lane_tiles = SPLIT // LANES
    rows = tok_sc.shape[1] // dil
    for r in range(dil):
        dst = pl.ds(r, rows, stride=dil)
        for c in range(n_lane_tiles):
            col0 = r * SPLIT + c * LANES
            tok_sc[c, dst, :] = acc_ref[0, :, col0:col0 + LANES]
        tok_sc[n_lane_tiles, dst, :] = l_ref[0, :, r * LANES:(r + 1) * LANES]
    acc = jnp.concatenate([tok_sc[c] for c in range(n_lane_tiles)], axis=1)
    r = 1.0 / tok_sc[n_lane_tiles]
    r_hi = r.astype(BF16)
    r_lo = (r - r_hi.astype(F32)).astype(BF16)
    r_exp = (jnp.dot(r_hi, xexp_ref[...], preferred_element_type=F32)
             + jnp.dot(r_lo, xexp_ref[...], preferred_element_type=F32))
    g = g_ref[...]
    y = jnp.concatenate([acc * r_exp, od_ref[...]], axis=-1) * (g * jax.nn.sigmoid(g))
    return x_ref[...] + jnp.dot(y.astype(BF16), w_ref[...], preferred_element_type=F32)


def _gelu(x):
    return 0.5 * x * (1.0 + lax.erf(x * (2.0 ** -0.5)))


def _sgu_compute(x, ng_ref, win_ref, lng_ref, lnb_ref, ws_ref, bs_ref, wout_ref, width):
    tm = x.shape[0]
    ms = jnp.mean(x * x, axis=-1, keepdims=True)
    h = (x * lax.rsqrt(ms + RMS_EPS) * ng_ref[...]).astype(BF16)
    v = _gelu(jnp.dot(h, win_ref[:, width:2 * width], preferred_element_type=F32))
    mu = jnp.mean(v, axis=-1, keepdims=True)
    vc = v - mu
    vn = vc * lax.rsqrt(jnp.mean(vc * vc, axis=-1, keepdims=True) + LN_EPS)
    vn = (vn * lng_ref[...] + lnb_ref[...]).astype(BF16)
    gw = width // SGU_GROUPS
    row = lax.broadcasted_iota(jnp.int32, (SGU_CHUNK, SGU_CHUNK), 0)
    col = lax.broadcasted_iota(jnp.int32, (SGU_CHUNK, SGU_CHUNK), 1)
    causal = col <= row
    sp_groups = []
    for grp in range(SGU_GROUPS):
        ws = jnp.where(causal, ws_ref[grp], 0.0).astype(BF16)
        bias = bs_ref[grp]
        chunks = []
        for c in range(tm // SGU_CHUNK):
            vg = vn[c * SGU_CHUNK:(c + 1) * SGU_CHUNK, grp * gw:(grp + 1) * gw]
            chunks.append(jnp.dot(ws, vg, preferred_element_type=F32) + bias)
        sp_groups.append(jnp.concatenate(chunks, axis=0))
    sp = jnp.concatenate(sp_groups, axis=1)
    u = _gelu(jnp.dot(h, win_ref[:, 0:width], preferred_element_type=F32))
    g = jnp.dot(h, win_ref[:, 2 * width:3 * width], preferred_element_type=F32)
    y = u * sp * (g * jax.nn.sigmoid(g))
    return x + jnp.dot(y.astype(BF16), wout_ref[...], preferred_element_type=F32)


def _att_out_sgu_kernel(x_ref, acc_ref, l_ref, xexp_ref, od_ref, g_ref, watt_ref,
                        ng_ref, win_ref, lng_ref, lnb_ref, ws_ref, bs_ref, wout_ref, o_ref, tok_sc, *, dil, width):
    x1 = _att_out_compute(x_ref, acc_ref, l_ref, xexp_ref, od_ref, g_ref, watt_ref, tok_sc, dil)
    o_ref[...] = _sgu_compute(x1, ng_ref, win_ref, lng_ref, lnb_ref, ws_ref, bs_ref, wout_ref, width)


def _att_out_sgu(x2d, acc_view, l_view, od, g, w_att_out, norm_g, w_in, ln_g, ln_b, w_s, b_s, w_out):
    s, d = x2d.shape
    width = w_out.shape[0]
    tm = ROW_TILE
    dil = DSW_PATTERNS[-1][1]
    rows = tm // dil
    per_blk = BLK // rows
    assert acc_view.shape == (s // (BLK * dil), BLK, dil * SPLIT) and tm % dil == 0 and BLK % rows == 0
    assert l_view.shape == (s // (BLK * dil), BLK, dil * LANES) and tm % SGU_CHUNK == 0
    _, xexp = _dsw_constants()
    row = lambda w: pl.BlockSpec((tm, w), lambda i: (i, 0))
    view_spec = lambda w: pl.BlockSpec((1, rows, dil * w), lambda i: (i // per_blk, i % per_blk, 0))
    single = lambda shape: pl.BlockSpec(shape, lambda *_: (0,) * len(shape), pipeline_mode=pl.Buffered(1))
    return pl.pallas_call(
        functools.partial(_att_out_sgu_kernel, dil=dil, width=width),
        grid=(s // tm,),
        in_specs=[row(d), view_spec(SPLIT), view_spec(LANES), _const_spec(xexp.shape), row(SPLIT), row(2 * SPLIT),
                  single(w_att_out.shape),
                  _const_spec((1, d)), single(w_in.shape), _const_spec((1, width)), _const_spec((1, width)),
                  _const_spec(w_s.shape), _const_spec((SGU_GROUPS, SGU_CHUNK, 1)), single(w_out.shape)],
        out_specs=row(d),
        out_shape=jax.ShapeDtypeStruct((s, d), F32),
        scratch_shapes=[pltpu.VMEM((SPLIT // LANES + 1, tm, LANES), F32)],
        compiler_params=_params("parallel"),
        name="att_out_sgu",
    )(x2d, acc_view, l_view, xexp, od, g, w_att_out.astype(BF16),
      norm_g.reshape(1, d), w_in.astype(BF16), ln_g.reshape(1, width), ln_b.reshape(1, width),
      w_s, b_s.reshape(SGU_GROUPS, SGU_CHUNK, 1), w_out.astype(BF16))


def _rope_tables_fm(seq_len):
    pos = jnp.arange(seq_len, dtype=F32)
    inv = 1.0 / (ROPE_THETA ** (jnp.arange(0, 2 * ROT_HALF, 2, dtype=F32) / (2 * ROT_HALF)))
    ang = inv[:, None] * pos[None, :]
    return jnp.cos(ang), jnp.sin(ang)


def kernel(x, att_norm, att_w_in, dsw_q_norm, dsw_k_norm, diff_q_norm, diff_k_norm, diff_lam_q1, diff_lam_k1,
           diff_lam_q2, diff_lam_k2, diff_subln, att_w_out, sgu_norm, sgu_w_in, sgu_ln_g, sgu_ln_b, sgu_w_s,
           sgu_b_s, sgu_w_out):
    b, s, d = x.shape
    assert b == 1 and s % (BLK * DSW_PATTERNS[-1][1]) == 0
    x2d = x.reshape(s, d)
    cos_t, sin_t = _rope_tables_fm(s)

    gains = jnp.stack([dsw_q_norm[0], dsw_k_norm[0], diff_q_norm[0], diff_k_norm[0]])
    gains = jnp.broadcast_to(gains[:, None, :, None], (4, N_GROUPS64, HEAD_DIM, 1))
    qa_views, ka_views, va_views, g, kd_tok, qd_t, vd_t = _att_in(x2d, att_norm[0], att_w_in[0], gains, cos_t, sin_t)
    bound = lambda gq, gk: 1.02 * LOG2_E * HEAD_DIM ** 0.5 * jnp.max(jnp.abs(gq)) * jnp.max(jnp.abs(gk))
    oa_acc, oa_l = _dsw_attention(qa_views, ka_views, va_views, bound(dsw_q_norm[0], dsw_k_norm[0]))
    lam_params = jnp.stack([diff_lam_q1[0], diff_lam_k1[0], diff_lam_q2[0], diff_lam_k2[0]])
    od = _diff_attention(qd_t, kd_tok, vd_t, lam_params, diff_subln[0].reshape(DIFF_V_DIM, 1),
                         bound(diff_q_norm[0], diff_k_norm[0]))
    x2d = _att_out_sgu(x2d, oa_acc, oa_l, od, g, att_w_out[0],
                       sgu_norm[0], sgu_w_in[0], sgu_ln_g[0], sgu_ln_b[0], sgu_w_s[0], sgu_b_s[0], sgu_w_out[0])
    return x2d.reshape(b, s, d)
```

```python
import functools
import math

import numpy as np
import jax
import jax.numpy as jnp
from jax import lax
from jax.experimental import pallas as pl
from jax.experimental.pallas import tpu as pltpu

F32 = jnp.float32
BF16 = jnp.bfloat16

LANES = 128
HEAD_DIM = 64
ROT_HALF = 8
ROPE_THETA = 500000.0
BLK = 128
N_GROUPS64 = 8
SPLIT = N_GROUPS64 * HEAD_DIM
DSW_PATTERNS = ((128, 1), (512, 4), (2048, 16))
DSW_DIL_RATIO = 4
DIFF_HEADS = 4
DIFF_V_DIM = 128
SGU_GROUPS = 8
SGU_CHUNK = 128
RMS_EPS = 1e-6
LN_EPS = 1e-5
NEG_INF = -1e30
LAM_INIT = 0.8 - 0.6 * math.exp(-0.3 * 0)
LOG2_E = math.log2(math.e)
DIFF_UNSHIFTED_SCORE_BOUND = 96.0
DSW_UNSHIFTED_SCORE_BOUND = 96.0

V7X_VMEM_LIMIT_BYTES = 56 * 1024 * 1024

ROW_TILE = 512
ATT_IN_PARTS = 2
DSW_BLOCKS_PER_STEP = 8
DIFF_TQ = 1024
DIFF_TK = 1024
DIFF_DIAG_BLOCK = 512


def _params(*sem):
    return pltpu.CompilerParams(dimension_semantics=sem, vmem_limit_bytes=V7X_VMEM_LIMIT_BYTES)


def _const_spec(shape):
    nd = len(shape)
    return pl.BlockSpec(shape, lambda *_: (0,) * nd)


def _norm_rope_fm(p, gain, cos, sin, scale):
    t = p.shape[1]
    p3 = p.reshape(N_GROUPS64, HEAD_DIM, t)
    ms = jnp.mean(p3 * p3, axis=1, keepdims=True)
    y = p3 * lax.rsqrt(ms + RMS_EPS) * gain
    x1 = y[:, 0:ROT_HALF, :]
    x2 = y[:, ROT_HALF:2 * ROT_HALF, :]
    out = jnp.concatenate([x1 * cos - x2 * sin, x2 * cos + x1 * sin, y[:, 2 * ROT_HALF:, :]], axis=1)
    if scale != 1.0:
        out = out * scale
    return out.reshape(SPLIT, t)


def _emit_pattern_views(tok, tok_sc, view_refs, part, n_parts):
    t = tok.shape[0]
    n_lane_tiles = SPLIT // LANES
    for c in range(n_lane_tiles):
        tok_sc[c] = tok[:, c * LANES:(c + 1) * LANES]
    for (_, dil), ref in zip(DSW_PATTERNS, view_refs):
        if dil == 1:
            ref[part * t:(part + 1) * t, :] = tok.astype(BF16)
            continue
        n = t // dil
        for r in range(dil):
            rows = jnp.concatenate([tok_sc[c, pl.ds(r, n, stride=dil), :] for c in range(n_lane_tiles)], axis=1)
            ref[0, part * n:(part + 1) * n, r * SPLIT:(r + 1) * SPLIT] = rows.astype(BF16)


def _att_in_kernel(x_ref, ng_ref, wtok_ref, wfm_ref, gains_ref, cos_ref, sin_ref, *refs):
    n_pat = len(DSW_PATTERNS)
    qa_views, ka_views, va_views = refs[0:n_pat], refs[n_pat:2 * n_pat], refs[2 * n_pat:3 * n_pat]
    g_ref, kd_ref, qd_ref, vd_ref, tok_sc = refs[3 * n_pat:]
    scales = (HEAD_DIM ** -0.5 * LOG2_E, 1.0, HEAD_DIM ** -0.5 * LOG2_E, 1.0)
    n_parts = ATT_IN_PARTS
    t = x_ref.shape[0] // n_parts
    for part in range(n_parts):
        rows = slice(part * t, (part + 1) * t)
        x = x_ref[rows, :]
        ms = jnp.mean(x * x, axis=-1, keepdims=True)
        h = (x * lax.rsqrt(ms + RMS_EPS) * ng_ref[...]).astype(BF16)
        tok = jnp.dot(h, wtok_ref[...], preferred_element_type=F32)
        _emit_pattern_views(tok[:, :SPLIT], tok_sc.at[part, 0], va_views, part, n_parts)
        g_ref[rows, :] = tok[:, SPLIT:]
        cos = cos_ref[:, rows][None]
        sin = sin_ref[:, rows][None]
        def feature_major(idx):
            return lax.dot_general(wfm_ref[idx * SPLIT:(idx + 1) * SPLIT, :], h, (((1,), (1,)), ((), ())),
                                   preferred_element_type=F32)

        def normed(idx):
            return _norm_rope_fm(feature_major(idx), gains_ref[idx], cos, sin, scales[idx])

        _emit_pattern_views(normed(0).T, tok_sc.at[part, 1], qa_views, part, n_parts)
        _emit_pattern_views(normed(1).T, tok_sc.at[part, 2], ka_views, part, n_parts)
        qd_ref[:, rows] = normed(2).astype(BF16)
        kd_ref[rows, :] = normed(3).T.astype(BF16)
        vd_ref[:, rows] = feature_major(4).astype(BF16)


def _att_in(x2d, norm_g, w_in, gains, cos_t, sin_t):
    s, d = x2d.shape
    tm = ROW_TILE
    w = w_in.astype(BF16)
    w_tok = jnp.concatenate([w[:, 2 * SPLIT:4 * SPLIT], w[:, 7 * SPLIT:8 * SPLIT]], axis=1)
    w_fm = jnp.concatenate([w[:, 0:2 * SPLIT], w[:, 4 * SPLIT:7 * SPLIT]], axis=1).T
    view_shapes, view_specs = [], []
    for _, dil in DSW_PATTERNS:
        if dil == 1:
            view_shapes.append(jax.ShapeDtypeStruct((s, SPLIT), BF16))
            view_specs.append(pl.BlockSpec((tm, SPLIT), lambda i: (i, 0)))
            continue
        rows = tm // dil
        assert tm % dil == 0 and BLK % rows == 0 and rows % (16 * ATT_IN_PARTS) == 0
        per_blk = BLK // rows
        view_shapes.append(jax.ShapeDtypeStruct((s // (BLK * dil), BLK, dil * SPLIT), BF16))
        view_specs.append(pl.BlockSpec((1, rows, dil * SPLIT),
                                       functools.partial(lambda i, per_blk: (i // per_blk, i % per_blk, 0),
                                                         per_blk=per_blk)))
    fm = jax.ShapeDtypeStruct((SPLIT, s), BF16)
    fm_spec = pl.BlockSpec((SPLIT, tm), lambda i: (0, i))
    outs = pl.pallas_call(
        _att_in_kernel,
        grid=(s // tm,),
        in_specs=[
            pl.BlockSpec((tm, d), lambda i: (i, 0)),
            _const_spec((1, d)),
            _const_spec(w_tok.shape),
            _const_spec(w_fm.shape),
            _const_spec(gains.shape),
            pl.BlockSpec((ROT_HALF, tm), lambda i: (0, i)),
            pl.BlockSpec((ROT_HALF, tm), lambda i: (0, i)),
        ],
        out_specs=view_specs * 3 + [
            pl.BlockSpec((tm, 2 * SPLIT), lambda i: (i, 0)),
            pl.BlockSpec((tm, SPLIT), lambda i: (i, 0)),
            fm_spec, fm_spec,
        ],
        out_shape=view_shapes * 3 + [
            jax.ShapeDtypeStruct((s, 2 * SPLIT), F32),
            jax.ShapeDtypeStruct((s, SPLIT), BF16),
            fm, fm,
        ],
        scratch_shapes=[pltpu.VMEM((ATT_IN_PARTS, 3, SPLIT // LANES, tm // ATT_IN_PARTS, LANES), F32)],
        compiler_params=_params("parallel"),
        name="att_in_proj",
    )(x2d, norm_g.reshape(1, d), w_tok, w_fm, gains, cos_t, sin_t)
    n_pat = len(DSW_PATTERNS)
    return (outs[0:n_pat], outs[n_pat:2 * n_pat], outs[2 * n_pat:3 * n_pat]) + tuple(outs[3 * n_pat:])


def _dsw_kernel(*refs, has_state, blocks_per_step):
    if has_state:
        (q_ref, kprev_ref, kcur_ref, vprev_ref, vcur_ref, oin_ref, lsein_ref,
         o_ref, lse_ref, kall, vall) = refs
    else:
        (q_ref, kprev_ref, kcur_ref, vprev_ref, vcur_ref, o_ref, lse_ref, kall, vall) = refs
        oin_ref = lsein_ref = None
    g_blocks = blocks_per_step
    step = pl.program_id(1)
    kall[0:BLK, :] = kprev_ref[0]
    vall[0:BLK, :] = vprev_ref[0]
    kall[BLK:, :] = kcur_ref[...].reshape(g_blocks * BLK, SPLIT)
    vall[BLK:, :] = vcur_ref[...].reshape(g_blocks * BLK, SPLIT)

    qi = lax.broadcasted_iota(jnp.int32, (BLK, 2 * BLK), 0) + BLK
    kj = lax.broadcasted_iota(jnp.int32, (BLK, 2 * BLK), 1)
    dist = qi - kj
    band = (dist >= 0) & (dist <= BLK)
    lane = lax.broadcasted_iota(jnp.int32, (BLK, 2 * HEAD_DIM), 1)
    low_half = lane < HEAD_DIM

    def block(g, carry):
        first_key = jnp.where((step * g_blocks + g) == 0, BLK, 0)
        mask = band & (kj >= first_key)
        row0 = pl.multiple_of(g * BLK, BLK)
        lse_tile = jnp.zeros((BLK, BLK), F32)
        lse_in = lsein_ref[g] if has_state else None
        for hp in range(N_GROUPS64 // 2):
            cols = slice(hp * 2 * HEAD_DIM, (hp + 1) * 2 * HEAD_DIM)
            q_pair = q_ref[g, :, cols]
            kk = kall[pl.ds(row0, 2 * BLK), cols]
            vv = vall[pl.ds(row0, 2 * BLK), cols]
            o_in = oin_ref[g, :, cols] if has_state else None
            halves = []
            for e in range(2):
                head = 2 * hp + e
                sel = low_half if e == 0 else jnp.logical_not(low_half)
                qm = jnp.where(sel, q_pair, jnp.zeros_like(q_pair))
                sc = lax.dot_general(qm, kk, (((1,), (1,)), ((), ())), preferred_element_type=F32)
                sc = jnp.where(mask, sc, NEG_INF)
                m_blk = jnp.max(sc, axis=-1, keepdims=True)
                if has_state:
                    m_old = lse_in[:, head:head + 1]
                    m_new = jnp.maximum(m_old, m_blk)
                    alpha = jnp.exp2(m_old - m_new)
                else:
                    m_new = m_blk
                p = jnp.exp2(sc - m_new)
                l = jnp.sum(p, axis=-1, keepdims=True)
                pv = jnp.dot(p.astype(BF16), vv, preferred_element_type=F32)
                if has_state:
                    l = l + alpha
                    pv = pv + alpha * o_in
                halves.append(pv / l)
                lse_tile = jnp.where(lane == head, m_new + jnp.log2(l), lse_tile)
            o_ref[g, :, cols] = jnp.where(low_half, halves[0], halves[1])
        lse_ref[g] = lse_tile
        return carry

    lax.fori_loop(0, g_blocks, block, 0)


def _dsw_pattern(q, k, v, dil, state):
    s = q.size // SPLIT
    nb = s // (BLK * dil)
    g_blocks = min(DSW_BLOCKS_PER_STEP, nb)
    view = lambda a, width: a.reshape(nb, BLK, dil * width)
    cur_spec = lambda width: pl.BlockSpec((g_blocks, BLK, width), lambda r, j: (j, 0, r))
    prev_spec = pl.BlockSpec((1, BLK, SPLIT), lambda r, j: (jnp.maximum(j * g_blocks - 1, 0), 0, r))
    operands = [view(q, SPLIT), view(k, SPLIT), view(k, SPLIT), view(v, SPLIT), view(v, SPLIT)]
    in_specs = [cur_spec(SPLIT), prev_spec, cur_spec(SPLIT), prev_spec, cur_spec(SPLIT)]
    if state is not None:
        operands += [view(state[0], SPLIT), view(state[1], BLK)]
        in_specs += [cur_spec(SPLIT), cur_spec(BLK)]
    o, lse = pl.pallas_call(
        functools.partial(_dsw_kernel, has_state=state is not None, blocks_per_step=g_blocks),
        grid=(dil, nb // g_blocks),
        in_specs=in_specs,
        out_specs=[cur_spec(SPLIT), cur_spec(BLK)],
        out_shape=[jax.ShapeDtypeStruct((nb, BLK, dil * SPLIT), F32),
                   jax.ShapeDtypeStruct((nb, BLK, dil * BLK), F32)],
        scratch_shapes=[pltpu.VMEM(((g_blocks + 1) * BLK, SPLIT), BF16),
                        pltpu.VMEM(((g_blocks + 1) * BLK, SPLIT), BF16)],
        compiler_params=_params("parallel", "arbitrary"),
        name=f"dsw_attention_dil{dil}",
    )(*operands)
    return o.reshape(s, SPLIT), lse.reshape(s, BLK)


def _dsw_unshifted_kernel(*refs, has_state, last, blocks_per_step):
    refs = list(refs)
    q_ref, kprev_ref, kcur_ref, vprev_ref, vcur_ref, ehot_ref = refs[:6]
    pos = 6
    if has_state:
        accin_ref, lin_ref = refs[pos:pos + 2]
        pos += 2
    acc_ref, l_ref, kall, vall = refs[pos:pos + 4]
    perm_sc = None if last else refs[pos + 4]
    g_blocks = blocks_per_step
    step = pl.program_id(1)
    kall[0:BLK, :] = kprev_ref[0]
    vall[0:BLK, :] = vprev_ref[0]
    kall[BLK:, :] = kcur_ref[...].reshape(g_blocks * BLK, SPLIT)
    vall[BLK:, :] = vcur_ref[...].reshape(g_blocks * BLK, SPLIT)

    qi = (lax.broadcasted_iota(jnp.int32, (2 * BLK, 2 * BLK), 0) & (BLK - 1)) + BLK
    kj = lax.broadcasted_iota(jnp.int32, (2 * BLK, 2 * BLK), 1)
    dist = qi - kj
    band = (dist >= 0) & (dist <= BLK)
    q_low = lax.broadcasted_iota(jnp.int32, (BLK, 2 * HEAD_DIM), 1) < HEAD_DIM
    v_low = lax.broadcasted_iota(jnp.int32, (2 * BLK, 2 * HEAD_DIM), 1) < HEAD_DIM

    def block(g, carry):
        first_key = jnp.where((step * g_blocks + g) == 0, BLK, 0)
        mask = band & (kj >= first_key)
        row0 = pl.multiple_of(g * BLK, BLK)
        l_tile = lin_ref[g] if has_state else jnp.zeros((BLK, BLK), F32)
        for hp in range(N_GROUPS64 // 2):
            cols = slice(hp * 2 * HEAD_DIM, (hp + 1) * 2 * HEAD_DIM)
            q_pair = q_ref[g, :, cols]
            kk = kall[pl.ds(row0, 2 * BLK), cols]
            vv = vall[pl.ds(row0, 2 * BLK), cols]
            qz = jnp.zeros_like(q_pair)
            vz = jnp.zeros_like(vv)
            q2 = jnp.concatenate([jnp.where(q_low, q_pair, qz), jnp.where(q_low, qz, q_pair)], axis=0)
            sc = lax.dot_general(q2, kk, (((1,), (1,)), ((), ())), preferred_element_type=F32)
            p2 = jnp.where(mask, jnp.exp2(sc), 0.0).astype(BF16)
            p_cat = jnp.concatenate([p2[:BLK], p2[BLK:]], axis=1)
            v_split = jnp.concatenate([jnp.where(v_low, vv, vz), jnp.where(v_low, vz, vv)], axis=0)
            rhs = jnp.concatenate([v_split, ehot_ref[hp]], axis=1)
            out2 = jnp.dot(p_cat, rhs, preferred_element_type=F32)
            acc = out2[:, :2 * HEAD_DIM]
            if has_state:
                acc = acc + accin_ref[g, :, cols]
            l_tile = l_tile + out2[:, 2 * HEAD_DIM:]
            if last:
                acc_ref[g, :, cols] = acc
            else:
                perm_sc[hp] = acc
        if last:
            l_ref[g] = l_tile
        else:
            perm_sc[N_GROUPS64 // 2] = l_tile
            ratio = DSW_DIL_RATIO
            rows = BLK // ratio
            out_blk = lax.shift_right_logical(g, ratio.bit_length() - 1)
            row_dst = pl.ds(pl.multiple_of((g & (ratio - 1)) * rows, rows), rows)
            for m in range(ratio):
                src = pl.ds(m, rows, stride=ratio)
                for hp in range(N_GROUPS64 // 2):
                    col0 = m * SPLIT + hp * 2 * HEAD_DIM
                    acc_ref[out_blk, row_dst, col0:col0 + 2 * HEAD_DIM] = perm_sc[hp, src, :]
                l_ref[out_blk, row_dst, m * BLK:(m + 1) * BLK] = perm_sc[N_GROUPS64 // 2, src, :]
        return carry

    lax.fori_loop(0, g_blocks, block, 0, unroll=True)


def _dsw_residue_of(hop):
    if hop == 0:
        return lambda g: g
    prev = _dsw_residue_of(hop - 1)
    prev_dil = DSW_PATTERNS[hop - 1][1]
    return lambda g: prev_dil * (g % DSW_DIL_RATIO) + prev(g // DSW_DIL_RATIO)


def _dsw_unshifted_pattern(q, k, v, hop, state, ehot):
    dil = DSW_PATTERNS[hop][1]
    last = hop == len(DSW_PATTERNS) - 1
    s = q.size // SPLIT
    nb = s // (BLK * dil)
    g_blocks = min(DSW_BLOCKS_PER_STEP, nb)
    ratio = DSW_DIL_RATIO
    residue = _dsw_residue_of(hop)
    view = lambda a: a.reshape(nb, BLK, dil * SPLIT)
    qkv_spec = pl.BlockSpec((g_blocks, BLK, SPLIT), lambda r, j: (j, 0, residue(r)))
    prev_spec = pl.BlockSpec((1, BLK, SPLIT), lambda r, j: (jnp.maximum(j * g_blocks - 1, 0), 0, residue(r)))
    state_spec = lambda width: pl.BlockSpec((g_blocks, BLK, width), lambda r, j: (j, 0, r))
    operands = [view(q), view(k), view(k), view(v), view(v), ehot]
    in_specs = [qkv_spec, prev_spec, qkv_spec, prev_spec, qkv_spec, _const_spec(ehot.shape)]
    if state is not None:
        operands += list(state)
        in_specs += [state_spec(SPLIT), state_spec(BLK)]
    scratch = [pltpu.VMEM(((g_blocks + 1) * BLK, SPLIT), BF16), pltpu.VMEM(((g_blocks + 1) * BLK, SPLIT), BF16)]
    if last:
        own_spec = lambda width: pl.BlockSpec((g_blocks, BLK, width), lambda r, j: (j, 0, residue(r)))
        out_specs = [own_spec(SPLIT), own_spec(BLK)]
        out_shape = [jax.ShapeDtypeStruct((nb, BLK, dil * SPLIT), F32),
                     jax.ShapeDtypeStruct((nb, BLK, dil * BLK), F32)]
    else:
        assert DSW_PATTERNS[hop + 1][1] == dil * ratio and g_blocks % ratio == 0
        next_spec = lambda width: pl.BlockSpec((g_blocks // ratio, BLK, ratio * width), lambda r, j: (j, 0, r))
        out_specs = [next_spec(SPLIT), next_spec(BLK)]
        out_shape = [jax.ShapeDtypeStruct((nb // ratio, BLK, dil * ratio * SPLIT), F32),
                     jax.ShapeDtypeStruct((nb // ratio, BLK, dil * ratio * BLK), F32)]
        scratch.append(pltpu.VMEM((N_GROUPS64 // 2 + 1, BLK, 2 * HEAD_DIM), F32))
    out = pl.pallas_call(
        functools.partial(_dsw_unshifted_kernel, has_state=state is not None, last=last, blocks_per_step=g_blocks),
        grid=(dil, nb // g_blocks),
        in_specs=in_specs,
        out_specs=out_specs,
        out_shape=out_shape,
        scratch_shapes=scratch,
        compiler_params=_params("parallel", "arbitrary"),
        name=f"dsw_unshifted_dil{dil}",
    )(*operands)
    return tuple(out)


DSW_L_LANES_PER_HEAD = BLK // N_GROUPS64


def _dsw_constants():
    ehot = np.zeros((N_GROUPS64 // 2, 4 * BLK, BLK), np.float32)
    xexp = np.zeros((BLK, SPLIT), np.float32)
    for head in range(N_GROUPS64):
        hp, e = divmod(head, 2)
        lanes = slice(head * DSW_L_LANES_PER_HEAD, (head + 1) * DSW_L_LANES_PER_HEAD)
        ehot[hp, e * 2 * BLK:(e + 1) * 2 * BLK, lanes] = 1.0
        xexp[head * DSW_L_LANES_PER_HEAD, head * HEAD_DIM:(head + 1) * HEAD_DIM] = 1.0
    return jnp.asarray(ehot, BF16), jnp.asarray(xexp, BF16)


def _dsw_attention_unshifted(q_views, k_views, v_views):
    ehot, _ = _dsw_constants()
    state = None
    for hop in range(len(DSW_PATTERNS)):
        state = _dsw_unshifted_pattern(q_views[hop], k_views[hop], v_views[hop], hop, state, ehot)
    return state


def _dsw_attention_online(q_views, k_views, v_views):
    state = None
    for idx, (_, dil) in enumerate(DSW_PATTERNS):
        state = _dsw_pattern(q_views[idx], k_views[idx], v_views[idx], dil, state)
    o_view = state[0].reshape(q_views[-1].shape)
    return o_view, jnp.ones(o_view.shape[:2] + (o_view.shape[2] // SPLIT * BLK,), F32)


def _dsw_attention(q_views, k_views, v_views, score_bound):
    return lax.cond(score_bound <= DSW_UNSHIFTED_SCORE_BOUND, _dsw_attention_unshifted, _dsw_attention_online,
                    q_views, k_views, v_views)


def _diff_finalize(o0, o1, lam_ref, sg_ref, o_ref):
    lam_p = lam_ref[...]
    lam = (jnp.exp(jnp.sum(lam_p[0:1] * lam_p[1:2], axis=-1, keepdims=True))
           - jnp.exp(jnp.sum(lam_p[2:3] * lam_p[3:4], axis=-1, keepdims=True)) + LAM_INIT)
    od = o0 - lam * o1
    ms = jnp.mean(od * od, axis=0, keepdims=True)
    y = od * lax.rsqrt(ms + RMS_EPS) * sg_ref[...] * (1.0 - LAM_INIT)
    o_ref[...] = y.T


def _component_queries(qt):
    zeros = jnp.zeros((HEAD_DIM, qt.shape[1]), qt.dtype)
    return (jnp.concatenate([qt[:HEAD_DIM], zeros], axis=0), jnp.concatenate([zeros, qt[HEAD_DIM:]], axis=0))


def _causal_keep(i, k0, tq, tk):
    kpos = k0 + lax.broadcasted_iota(jnp.int32, (tk, tq), 0)
    qpos = i * tq + lax.broadcasted_iota(jnp.int32, (tk, tq), 1)
    return kpos <= qpos


def _diff_unshifted_kernel(qt_ref, k_ref, vt_ref, lam_ref, sg_ref, o_ref, l_sc, acc_sc, *, tq, tk):
    i = pl.program_id(1)
    qc = _component_queries(qt_ref[...])
    l_sc[...] = jnp.zeros(l_sc.shape, F32)
    acc_sc[...] = jnp.zeros(acc_sc.shape, F32)

    def full_chunk(j):
        k0 = pl.multiple_of(j * tk, tk)
        vt = vt_ref[:, pl.ds(k0, tk)]
        for c in range(2):
            st = jnp.dot(k_ref[pl.ds(k0, tk), :], qc[c], preferred_element_type=F32)
            p = jnp.exp2(st)
            l_sc[c] += jnp.sum(p.reshape(tk // 8, 8, tq), axis=0)
            acc_sc[c] += jnp.dot(vt, p.astype(BF16), preferred_element_type=F32)

    def chunk_quad(jq, carry):
        for u in range(4):
            full_chunk(4 * jq + u)
        return carry

    n_quads = lax.shift_right_logical(i, 2)
    lax.fori_loop(0, n_quads, chunk_quad, 0)

    @pl.when((i & 2) == 2)
    def _():
        full_chunk(4 * n_quads)
        full_chunk(4 * n_quads + 1)

    @pl.when((i & 1) == 1)
    def _():
        full_chunk(i - 1)

    sb = DIFF_DIAG_BLOCK
    k0 = pl.multiple_of(i * tk, tk)
    tri = lax.broadcasted_iota(jnp.int32, (sb, sb), 0) <= lax.broadcasted_iota(jnp.int32, (sb, sb), 1)
    for qb in range(tq // sb):
        nk = (qb + 1) * sb
        lanes = slice(qb * sb, (qb + 1) * sb)
        vt = vt_ref[:, pl.ds(k0, nk)]
        for c in range(2):
            st = jnp.dot(k_ref[pl.ds(k0, nk), :], qc[c][:, lanes], preferred_element_type=F32)
            p = jnp.exp2(st)
            p_tri = jnp.where(tri, p[qb * sb:, :], 0.0)
            p = jnp.concatenate([p[:qb * sb, :], p_tri], axis=0) if qb else p_tri
            l_sc[c, :, lanes] += jnp.sum(p.reshape(nk // 8, 8, sb), axis=0)
            acc_sc[c, :, lanes] += jnp.dot(vt, p.astype(BF16), preferred_element_type=F32)

    l0 = jnp.sum(l_sc[0], axis=0, keepdims=True)
    l1 = jnp.sum(l_sc[1], axis=0, keepdims=True)
    _diff_finalize(acc_sc[0] / l0, acc_sc[1] / l1, lam_ref, sg_ref, o_ref)


def _diff_kernel(qt_ref, k_ref, vt_ref, lam_ref, sg_ref, o_ref, m_sc, l_sc, acc_sc, *, tq, tk):
    i = pl.program_id(1)
    qc = _component_queries(qt_ref[...])
    m_sc[...] = jnp.full(m_sc.shape, NEG_INF, F32)
    l_sc[...] = jnp.zeros(l_sc.shape, F32)
    acc_sc[...] = jnp.zeros(acc_sc.shape, F32)

    def chunk(j, masked):
        k0 = pl.multiple_of(j * tk, tk)
        vt = vt_ref[:, pl.ds(k0, tk)]
        for c in range(2):
            st = jnp.dot(k_ref[pl.ds(k0, tk), :], qc[c], preferred_element_type=F32)
            if masked:
                st = jnp.where(_causal_keep(i, k0, tq, tk), st, NEG_INF)
            m_old = m_sc[c]
            m_new = jnp.maximum(m_old, jnp.max(st, axis=0, keepdims=True))
            alpha = jnp.exp2(m_old - m_new)
            p = jnp.exp2(st - m_new)
            l_sc[c] = alpha * l_sc[c] + jnp.sum(p, axis=0, keepdims=True)
            acc_sc[c] = alpha * acc_sc[c] + jnp.dot(vt, p.astype(BF16), preferred_element_type=F32)
            m_sc[c] = m_new

    n_full = (i * tq) // tk

    def full_chunk(j, carry):
        chunk(j, False)
        return carry

    lax.fori_loop(0, n_full, full_chunk, 0)
    for jj in range(tq // tk):
        chunk(n_full + jj, True)

    _diff_finalize(acc_sc[0] / l_sc[0], acc_sc[1] / l_sc[1], lam_ref, sg_ref, o_ref)


def _diff_call(body, scratch, name, qd_t, kd_tok, vd_t, lam_params, subln_col):
    s = qd_t.shape[1]
    tq, tk = DIFF_TQ, DIFF_TK
    assert tq == tk and s % tq == 0 and tq % DIFF_DIAG_BLOCK == 0
    return pl.pallas_call(
        functools.partial(body, tq=tq, tk=tk),
        grid=(DIFF_HEADS, s // tq),
        in_specs=[
            pl.BlockSpec((DIFF_V_DIM, tq), lambda h, i: (h, i)),
            pl.BlockSpec((s, 2 * HEAD_DIM), lambda h, i: (0, h)),
            pl.BlockSpec((DIFF_V_DIM, s), lambda h, i: (h, 0)),
            _const_spec(lam_params.shape),
            _const_spec(subln_col.shape),
        ],
        out_specs=pl.BlockSpec((tq, DIFF_V_DIM), lambda h, i: (i, h)),
        out_shape=jax.ShapeDtypeStruct((s, DIFF_HEADS * DIFF_V_DIM), F32),
        scratch_shapes=scratch(tq),
        compiler_params=_params("parallel", "arbitrary"),
        name=name,
    )(qd_t, kd_tok, vd_t, lam_params, subln_col)


def _diff_attention(qd_t, kd_tok, vd_t, lam_params, subln_col, score_bound):
    acc = lambda tq: pltpu.VMEM((2, DIFF_V_DIM, tq), F32)
    unshifted = functools.partial(
        _diff_call, _diff_unshifted_kernel, lambda tq: [pltpu.VMEM((2, 8, tq), F32), acc(tq)],
        "diff_attention_unshifted")
    online = functools.partial(
        _diff_call, _diff_kernel, lambda tq: [pltpu.VMEM((2, 1, tq), F32), pltpu.VMEM((2, 1, tq), F32), acc(tq)],
        "diff_attention_online")
    return lax.cond(score_bound <= DIFF_UNSHIFTED_SCORE_BOUND, unshifted, online,
                    qd_t, kd_tok, vd_t, lam_params, subln_col)


def _att_out_compute(x_ref, acc_ref, l_ref, xexp_ref, od_ref, g_ref, w_ref, tok_sc, dil):
    n_lane_tiles = SPLIT // LANES
    rows = tok_sc.shape[1] // dil
    for r in range(dil):
        dst = pl.ds(r, rows, stride=dil)
        for c in range(n_lane_tiles):
            col0 = r * SPLIT + c * LANES
            tok_sc[c, dst, :] = acc_ref[0, :, col0:col0 + LANES]
        tok_sc[n_lane_tiles, dst, :] = l_ref[0, :, r * LANES:(r + 1) * LANES]
    acc = jnp.concatenate([tok_sc[c] for c in range(n_lane_tiles)], axis=1)
    r = 1.0 / tok_sc[n_lane_tiles]
    r_hi = r.astype(BF16)
    r_lo = (r - r_hi.astype(F32)).astype(BF16)
    r_exp = (jnp.dot(r_hi, xexp_ref[...], preferred_element_type=F32)
             + jnp.dot(r_lo, xexp_ref[...], preferred_element_type=F32))
    g = g_ref[...]
    y = jnp.concatenate([acc * r_exp, od_ref[...]], axis=-1) * (g * jax.nn.sigmoid(g))
    return x_ref[...] + jnp.dot(y.astype(BF16), w_ref[...], preferred_element_type=F32)


def _gelu(x):
    return 0.5 * x * (1.0 + lax.erf(x * (2.0 ** -0.5)))


def _sgu_compute(x, ng_ref, win_ref, lng_ref, lnb_ref, ws_ref, bs_ref, wout_ref, width):
    tm = x.shape[0]
    ms = jnp.mean(x * x, axis=-1, keepdims=True)
    h = (x * lax.rsqrt(ms + RMS_EPS) * ng_ref[...]).astype(BF16)
    v = _gelu(jnp.dot(h, win_ref[:, width:2 * width], preferred_element_type=F32))
    mu = jnp.mean(v, axis=-1, keepdims=True)
    vc = v - mu
    vn = vc * lax.rsqrt(jnp.mean(vc * vc, axis=-1, keepdims=True) + LN_EPS)
    vn = (vn * lng_ref[...] + lnb_ref[...]).astype(BF16)
    gw = width // SGU_GROUPS
    row = lax.broadcasted_iota(jnp.int32, (SGU_CHUNK, SGU_CHUNK), 0)
    col = lax.broadcasted_iota(jnp.int32, (SGU_CHUNK, SGU_CHUNK), 1)
    causal = col <= row
    sp_groups = []
    for grp in range(SGU_GROUPS):
        ws = jnp.where(causal, ws_ref[grp], 0.0).astype(BF16)
        bias = bs_ref[grp]
        chunks = []
        for c in range(tm // SGU_CHUNK):
            vg = vn[c * SGU_CHUNK:(c + 1) * SGU_CHUNK, grp * gw:(grp + 1) * gw]
            chunks.append(jnp.dot(ws, vg, preferred_element_type=F32) + bias)
        sp_groups.append(jnp.concatenate(chunks, axis=0))
    sp = jnp.concatenate(sp_groups, axis=1)
    u = _gelu(jnp.dot(h, win_ref[:, 0:width], preferred_element_type=F32))
    g = jnp.dot(h, win_ref[:, 2 * width:3 * width], preferred_element_type=F32)
    y = u * sp * (g * jax.nn.sigmoid(g))
    return x + jnp.dot(y.astype(BF16), wout_ref[...], preferred_element_type=F32)


def _att_out_sgu_kernel(x_ref, acc_ref, l_ref, xexp_ref, od_ref, g_ref, watt_ref,
                        ng_ref, win_ref, lng_ref, lnb_ref, ws_ref, bs_ref, wout_ref, o_ref, tok_sc, *, dil, width):
    x1 = _att_out_compute(x_ref, acc_ref, l_ref, xexp_ref, od_ref, g_ref, watt_ref, tok_sc, dil)
    o_ref[...] = _sgu_compute(x1, ng_ref, win_ref, lng_ref, lnb_ref, ws_ref, bs_ref, wout_ref, width)


def _att_out_sgu(x2d, acc_view, l_view, od, g, w_att_out, norm_g, w_in, ln_g, ln_b, w_s, b_s, w_out):
    s, d = x2d.shape
    width = w_out.shape[0]
    tm = ROW_TILE
    dil = DSW_PATTERNS[-1][1]
    rows = tm // dil
    per_blk = BLK // rows
    assert acc_view.shape == (s // (BLK * dil), BLK, dil * SPLIT) and tm % dil == 0 and BLK % rows == 0
    assert l_view.shape == (s // (BLK * dil), BLK, dil * LANES) and tm % SGU_CHUNK == 0
    _, xexp = _dsw_constants()
    row = lambda w: pl.BlockSpec((tm, w), lambda i: (i, 0))
    view_spec = lambda w: pl.BlockSpec((1, rows, dil * w), lambda i: (i // per_blk, i % per_blk, 0))
    single = lambda shape: pl.BlockSpec(shape, lambda *_: (0,) * len(shape), pipeline_mode=pl.Buffered(1))
    return pl.pallas_call(
        functools.partial(_att_out_sgu_kernel, dil=dil, width=width),
        grid=(s // tm,),
        in_specs=[row(d), view_spec(SPLIT), view_spec(LANES), _const_spec(xexp.shape), row(SPLIT), row(2 * SPLIT),
                  single(w_att_out.shape),
                  _const_spec((1, d)), single(w_in.shape), _const_spec((1, width)), _const_spec((1, width)),
                  _const_spec(w_s.shape), _const_spec((SGU_GROUPS, SGU_CHUNK, 1)), single(w_out.shape)],
        out_specs=row(d),
        out_shape=jax.ShapeDtypeStruct((s, d), F32),
        scratch_shapes=[pltpu.VMEM((SPLIT // LANES + 1, tm, LANES), F32)],
        compiler_params=_params("parallel"),
        name="att_out_sgu",
    )(x2d, acc_view, l_view, xexp, od, g, w_att_out.astype(BF16),
      norm_g.reshape(1, d), w_in.astype(BF16), ln_g.reshape(1, width), ln_b.reshape(1, width),
      w_s, b_s.reshape(SGU_GROUPS, SGU_CHUNK, 1), w_out.astype(BF16))


def _rope_tables_fm(seq_len):
    pos = jnp.arange(seq_len, dtype=F32)
    inv = 1.0 / (ROPE_THETA ** (jnp.arange(0, 2 * ROT_HALF, 2, dtype=F32) / (2 * ROT_HALF)))
    ang = inv[:, None] * pos[None, :]
    return jnp.cos(ang), jnp.sin(ang)


def kernel(x, att_norm, att_w_in, dsw_q_norm, dsw_k_norm, diff_q_norm, diff_k_norm, diff_lam_q1, diff_lam_k1,
           diff_lam_q2, diff_lam_k2, diff_subln, att_w_out, sgu_norm, sgu_w_in, sgu_ln_g, sgu_ln_b, sgu_w_s,
           sgu_b_s, sgu_w_out):
    b, s, d = x.shape
    assert b == 1 and s % (BLK * DSW_PATTERNS[-1][1]) == 0
    x2d = x.reshape(s, d)
    cos_t, sin_t = _rope_tables_fm(s)

    gains = jnp.stack([dsw_q_norm[0], dsw_k_norm[0], diff_q_norm[0], diff_k_norm[0]])
    gains = jnp.broadcast_to(gains[:, None, :, None], (4, N_GROUPS64, HEAD_DIM, 1))
    qa_views, ka_views, va_views, g, kd_tok, qd_t, vd_t = _att_in(x2d, att_norm[0], att_w_in[0], gains, cos_t, sin_t)
    bound = lambda gq, gk: 1.02 * LOG2_E * HEAD_DIM ** 0.5 * jnp.max(jnp.abs(gq)) * jnp.max(jnp.abs(gk))
    oa_acc, oa_l = _dsw_attention(qa_views, ka_views, va_views, bound(dsw_q_norm[0], dsw_k_norm[0]))
    lam_params = jnp.stack([diff_lam_q1[0], diff_lam_k1[0], diff_lam_q2[0], diff_lam_k2[0]])
    od = _diff_attention(qd_t, kd_tok, vd_t, lam_params, diff_subln[0].reshape(DIFF_V_DIM, 1),
                         bound(diff_q_norm[0], diff_k_norm[0]))
    x2d = _att_out_sgu(x2d, oa_acc, oa_l, od, g, att_w_out[0],
                       sgu_norm[0], sgu_w_in[0], sgu_ln_g[0], sgu_ln_b[0], sgu_w_s[0], sgu_b_s[0], sgu_w_out[0])
    return x2d.reshape(b, s, d)
```

```python
import functools
import math

import numpy as np
import jax
import jax.numpy as jnp
from jax import lax
from jax.experimental import pallas as pl
from jax.experimental.pallas import tpu as pltpu

F32 = jnp.float32
BF16 = jnp.bfloat16

LANES = 128
HEAD_DIM = 64
ROT_HALF = 8
ROPE_THETA = 500000.0
BLK = 128
N_GROUPS64 = 8
SPLIT = N_GROUPS64 * HEAD_DIM
DSW_PATTERNS = ((128, 1), (512, 4), (2048, 16))
DSW_DIL_RATIO = 4
DIFF_HEADS = 4
DIFF_V_DIM = 128
SGU_GROUPS = 8
SGU_CHUNK = 128
RMS_EPS = 1e-6
LN_EPS = 1e-5
NEG_INF = -1e30
LAM_INIT = 0.8 - 0.6 * math.exp(-0.3 * 0)
LOG2_E = math.log2(math.e)
DIFF_UNSHIFTED_SCORE_BOUND = 96.0
DSW_UNSHIFTED_SCORE_BOUND = 96.0

V7X_VMEM_LIMIT_BYTES = 56 * 1024 * 1024

ROW_TILE = 512
ATT_IN_PARTS = 2
DSW_BLOCKS_PER_STEP = 8
DIFF_TQ = 1024
DIFF_TK = 1024
DIFF_DIAG_BLOCK = 512


def _params(*sem):
    return pltpu.CompilerParams(dimension_semantics=sem, vmem_limit_bytes=V7X_VMEM_LIMIT_BYTES)


def _const_spec(shape):
    nd = len(shape)
    return pl.BlockSpec(shape, lambda *_: (0,) * nd)


def _norm_rope_fm(p, gain, cos, sin, scale):
    t = p.shape[1]
    p3 = p.reshape(N_GROUPS64, HEAD_DIM, t)
    ms = jnp.mean(p3 * p3, axis=1, keepdims=True)
    y = p3 * lax.rsqrt(ms + RMS_EPS) * gain
    x1 = y[:, 0:ROT_HALF, :]
    x2 = y[:, ROT_HALF:2 * ROT_HALF, :]
    out = jnp.concatenate([x1 * cos - x2 * sin, x2 * cos + x1 * sin, y[:, 2 * ROT_HALF:, :]], axis=1)
    if scale != 1.0:
        out = out * scale
    return out.reshape(SPLIT, t)


def _emit_pattern_views(tok, tok_sc, view_refs, part, n_parts):
    t = tok.shape[0]
    n_lane_tiles = SPLIT // LANES
    for c in range(n_lane_tiles):
        tok_sc[c] = tok[:, c * LANES:(c + 1) * LANES]
    for (_, dil), ref in zip(DSW_PATTERNS, view_refs):
        if dil == 1:
            ref[part * t:(part + 1) * t, :] = tok.astype(BF16)
            continue
        n = t // dil
        for r in range(dil):
            rows = jnp.concatenate([tok_sc[c, pl.ds(r, n, stride=dil), :] for c in range(n_lane_tiles)], axis=1)
            ref[0, part * n:(part + 1) * n, r * SPLIT:(r + 1) * SPLIT] = rows.astype(BF16)


def _att_in_kernel(x_ref, ng_ref, wtok_ref, wfm_ref, gains_ref, cos_ref, sin_ref, *refs):
    n_pat = len(DSW_PATTERNS)
    qa_views, ka_views, va_views = refs[0:n_pat], refs[n_pat:2 * n_pat], refs[2 * n_pat:3 * n_pat]
    g_ref, kd_ref, qd_ref, vd_ref, tok_sc = refs[3 * n_pat:]
    scales = (HEAD_DIM ** -0.5 * LOG2_E, 1.0, HEAD_DIM ** -0.5 * LOG2_E, 1.0)
    n_parts = ATT_IN_PARTS
    t = x_ref.shape[0] // n_parts
    for part in range(n_parts):
        rows = slice(part * t, (part + 1) * t)
        x = x_ref[rows, :]
        ms = jnp.mean(x * x, axis=-1, keepdims=True)
        h = (x * lax.rsqrt(ms + RMS_EPS) * ng_ref[...]).astype(BF16)
        tok = jnp.dot(h, wtok_ref[...], preferred_element_type=F32)
        _emit_pattern_views(tok[:, :SPLIT], tok_sc.at[part, 0], va_views, part, n_parts)
        g_ref[rows, :] = tok[:, SPLIT:]
        cos = cos_ref[:, rows][None]
        sin = sin_ref[:, rows][None]
        def feature_major(idx):
            return lax.dot_general(wfm_ref[idx * SPLIT:(idx + 1) * SPLIT, :], h, (((1,), (1,)), ((), ())),
                                   preferred_element_type=F32)

        def normed(idx):
            return _norm_rope_fm(feature_major(idx), gains_ref[idx], cos, sin, scales[idx])

        _emit_pattern_views(normed(0).T, tok_sc.at[part, 1], qa_views, part, n_parts)
        _emit_pattern_views(normed(1).T, tok_sc.at[part, 2], ka_views, part, n_parts)
        qd_ref[:, rows] = normed(2).astype(BF16)
        kd_ref[rows, :] = normed(3).T.astype(BF16)
        vd_ref[:, rows] = feature_major(4).astype(BF16)


def _att_in(x2d, norm_g, w_in, gains, cos_t, sin_t):
    s, d = x2d.shape
    tm = ROW_TILE
    w = w_in.astype(BF16)
    w_tok = jnp.concatenate([w[:, 2 * SPLIT:4 * SPLIT], w[:, 7 * SPLIT:8 * SPLIT]], axis=1)
    w_fm = jnp.concatenate([w[:, 0:2 * SPLIT], w[:, 4 * SPLIT:7 * SPLIT]], axis=1).T
    view_shapes, view_specs = [], []
    for _, dil in DSW_PATTERNS:
        if dil == 1:
            view_shapes.append(jax.ShapeDtypeStruct((s, SPLIT), BF16))
            view_specs.append(pl.BlockSpec((tm, SPLIT), lambda i: (i, 0)))
            continue
        rows = tm // dil
        assert tm % dil == 0 and BLK % rows == 0 and rows % (16 * ATT_IN_PARTS) == 0
        per_blk = BLK // rows
        view_shapes.append(jax.ShapeDtypeStruct((s // (BLK * dil), BLK, dil * SPLIT), BF16))
        view_specs.append(pl.BlockSpec((1, rows, dil * SPLIT),
                                       functools.partial(lambda i, per_blk: (i // per_blk, i % per_blk, 0),
                                                         per_blk=per_blk)))
    fm = jax.ShapeDtypeStruct((SPLIT, s), BF16)
    fm_spec = pl.BlockSpec((SPLIT, tm), lambda i: (0, i))
    outs = pl.pallas_call(
        _att_in_kernel,
        grid=(s // tm,),
        in_specs=[
            pl.BlockSpec((tm, d), lambda i: (i, 0)),
            _const_spec((1, d)),
            _const_spec(w_tok.shape),
            _const_spec(w_fm.shape),
            _const_spec(gains.shape),
            pl.BlockSpec((ROT_HALF, tm), lambda i: (0, i)),
            pl.BlockSpec((ROT_HALF, tm), lambda i: (0, i)),
        ],
        out_specs=view_specs * 3 + [
            pl.BlockSpec((tm, 2 * SPLIT), lambda i: (i, 0)),
            pl.BlockSpec((tm, SPLIT), lambda i: (i, 0)),
            fm_spec, fm_spec,
        ],
        out_shape=view_shapes * 3 + [
            jax.ShapeDtypeStruct((s, 2 * SPLIT), F32),
            jax.ShapeDtypeStruct((s, SPLIT), BF16),
            fm, fm,
        ],
        scratch_shapes=[pltpu.VMEM((ATT_IN_PARTS, 3, SPLIT // LANES, tm // ATT_IN_PARTS, LANES), F32)],
        compiler_params=_params("parallel"),
        name="att_in_proj",
    )(x2d, norm_g.reshape(1, d), w_tok, w_fm, gains, cos_t, sin_t)
    n_pat = len(DSW_PATTERNS)
    return (outs[0:n_pat], outs[n_pat:2 * n_pat], outs[2 * n_pat:3 * n_pat]) + tuple(outs[3 * n_pat:])


def _dsw_kernel(*refs, has_state, blocks_per_step):
    if has_state:
        (q_ref, kprev_ref, kcur_ref, vprev_ref, vcur_ref, oin_ref, lsein_ref,
         o_ref, lse_ref, kall, vall) = refs
    else:
        (q_ref, kprev_ref, kcur_ref, vprev_ref, vcur_ref, o_ref, lse_ref, kall, vall) = refs
        oin_ref = lsein_ref = None
    g_blocks = blocks_per_step
    step = pl.program_id(1)
    kall[0:BLK, :] = kprev_ref[0]
    vall[0:BLK, :] = vprev_ref[0]
    kall[BLK:, :] = kcur_ref[...].reshape(g_blocks * BLK, SPLIT)
    vall[BLK:, :] = vcur_ref[...].reshape(g_blocks * BLK, SPLIT)

    qi = lax.broadcasted_iota(jnp.int32, (BLK, 2 * BLK), 0) + BLK
    kj = lax.broadcasted_iota(jnp.int32, (BLK, 2 * BLK), 1)
    dist = qi - kj
    band = (dist >= 0) & (dist <= BLK)
    lane = lax.broadcasted_iota(jnp.int32, (BLK, 2 * HEAD_DIM), 1)
    low_half = lane < HEAD_DIM

    def block(g, carry):
        first_key = jnp.where((step * g_blocks + g) == 0, BLK, 0)
        mask = band & (kj >= first_key)
        row0 = pl.multiple_of(g * BLK, BLK)
        lse_tile = jnp.zeros((BLK, BLK), F32)
        lse_in = lsein_ref[g] if has_state else None
        for hp in range(N_GROUPS64 // 2):
            cols = slice(hp * 2 * HEAD_DIM, (hp + 1) * 2 * HEAD_DIM)
            q_pair = q_ref[g, :, cols]
            kk = kall[pl.ds(row0, 2 * BLK), cols]
            vv = vall[pl.ds(row0, 2 * BLK), cols]
            o_in = oin_ref[g, :, cols] if has_state else None
            halves = []
            for e in range(2):
                head = 2 * hp + e
                sel = low_half if e == 0 else jnp.logical_not(low_half)
                qm = jnp.where(sel, q_pair, jnp.zeros_like(q_pair))
                sc = lax.dot_general(qm, kk, (((1,), (1,)), ((), ())), preferred_element_type=F32)
                sc = jnp.where(mask, sc, NEG_INF)
                m_blk = jnp.max(sc, axis=-1, keepdims=True)
                if has_state:
                    m_old = lse_in[:, head:head + 1]
                    m_new = jnp.maximum(m_old, m_blk)
                    alpha = jnp.exp2(m_old - m_new)
                else:
                    m_new = m_blk
                p = jnp.exp2(sc - m_new)
                l = jnp.sum(p, axis=-1, keepdims=True)
                pv = jnp.dot(p.astype(BF16), vv, preferred_element_type=F32)
                if has_state:
                    l = l + alpha
                    pv = pv + alpha * o_in
                halves.append(pv / l)
                lse_tile = jnp.where(lane == head, m_new + jnp.log2(l), lse_tile)
            o_ref[g, :, cols] = jnp.where(low_half, halves[0], halves[1])
        lse_ref[g] = lse_tile
        return carry

    lax.fori_loop(0, g_blocks, block, 0)


def _dsw_pattern(q, k, v, dil, state):
    s = q.size // SPLIT
    nb = s // (BLK * dil)
    g_blocks = min(DSW_BLOCKS_PER_STEP, nb)
    view = lambda a, width: a.reshape(nb, BLK, dil * width)
    cur_spec = lambda width: pl.BlockSpec((g_blocks, BLK, width), lambda r, j: (j, 0, r))
    prev_spec = pl.BlockSpec((1, BLK, SPLIT), lambda r, j: (jnp.maximum(j * g_blocks - 1, 0), 0, r))
    operands = [view(q, SPLIT), view(k, SPLIT), view(k, SPLIT), view(v, SPLIT), view(v, SPLIT)]
    in_specs = [cur_spec(SPLIT), prev_spec, cur_spec(SPLIT), prev_spec, cur_spec(SPLIT)]
    if state is not None:
        operands += [view(state[0], SPLIT), view(state[1], BLK)]
        in_specs += [cur_spec(SPLIT), cur_spec(BLK)]
    o, lse = pl.pallas_call(
        functools.partial(_dsw_kernel, has_state=state is not None, blocks_per_step=g_blocks),
        grid=(dil, nb // g_blocks),
        in_specs=in_specs,
        out_specs=[cur_spec(SPLIT), cur_spec(BLK)],
        out_shape=[jax.ShapeDtypeStruct((nb, BLK, dil * SPLIT), F32),
                   jax.ShapeDtypeStruct((nb, BLK, dil * BLK), F32)],
        scratch_shapes=[pltpu.VMEM(((g_blocks + 1) * BLK, SPLIT), BF16),
                        pltpu.VMEM(((g_blocks + 1) * BLK, SPLIT), BF16)],
        compiler_params=_params("parallel", "arbitrary"),
        name=f"dsw_attention_dil{dil}",
    )(*operands)
    return o.reshape(s, SPLIT), lse.reshape(s, BLK)


def _dsw_unshifted_kernel(*refs, has_state, last, blocks_per_step):
    refs = list(refs)
    q_ref, kprev_ref, kcur_ref, vprev_ref, vcur_ref, ehot_ref = refs[:6]
    pos = 6
    if has_state:
        accin_ref, lin_ref = refs[pos:pos + 2]
        pos += 2
    acc_ref, l_ref = refs[pos:pos + 2]
    perm_sc = None if last else refs[pos + 2]
    g_blocks = blocks_per_step
    step = pl.program_id(1)

    def window(prev_ref, cur_ref, g, cols):
        before = prev_ref[0, :, cols] if g == 0 else cur_ref[g - 1, :, cols]
        return jnp.concatenate([before, cur_ref[g, :, cols]], axis=0)

    qi = (lax.broadcasted_iota(jnp.int32, (2 * BLK, 2 * BLK), 0) & (BLK - 1)) + BLK
    kj = lax.broadcasted_iota(jnp.int32, (2 * BLK, 2 * BLK), 1)
    dist = qi - kj
    band = (dist >= 0) & (dist <= BLK)
    q_low = lax.broadcasted_iota(jnp.int32, (BLK, 2 * HEAD_DIM), 1) < HEAD_DIM
    v_low = lax.broadcasted_iota(jnp.int32, (2 * BLK, 2 * HEAD_DIM), 1) < HEAD_DIM

    for g in range(g_blocks):
        if g == 0:
            mask = band & (kj >= jnp.where(step == 0, BLK, 0))
        else:
            mask = band
        l_tile = lin_ref[g] if has_state else jnp.zeros((BLK, BLK), F32)
        slot = perm_sc.at[g % 2] if not last else None
        for hp in range(N_GROUPS64 // 2):
            cols = slice(hp * 2 * HEAD_DIM, (hp + 1) * 2 * HEAD_DIM)
            q_pair = q_ref[g, :, cols]
            kk = window(kprev_ref, kcur_ref, g, cols)
            vv = window(vprev_ref, vcur_ref, g, cols)
            qz = jnp.zeros_like(q_pair)
            vz = jnp.zeros_like(vv)
            q2 = jnp.concatenate([jnp.where(q_low, q_pair, qz), jnp.where(q_low, qz, q_pair)], axis=0)
            sc = lax.dot_general(q2, kk, (((1,), (1,)), ((), ())), preferred_element_type=F32)
            p2 = jnp.where(mask, jnp.exp2(sc), 0.0).astype(BF16)
            p_cat = jnp.concatenate([p2[:BLK], p2[BLK:]], axis=1)
            v_split = jnp.concatenate([jnp.where(v_low, vv, vz), jnp.where(v_low, vz, vv)], axis=0)
            rhs = jnp.concatenate([v_split, ehot_ref[hp]], axis=1)
            out2 = jnp.dot(p_cat, rhs, preferred_element_type=F32)
            acc = out2[:, :2 * HEAD_DIM]
            if has_state:
                acc = acc + accin_ref[g, :, cols]
            l_tile = l_tile + out2[:, 2 * HEAD_DIM:]
            if last:
                acc_ref[g, :, cols] = acc
            else:
                slot[hp] = acc
        if last:
            l_ref[g] = l_tile
        else:
            slot[N_GROUPS64 // 2] = l_tile
            ratio = DSW_DIL_RATIO
            rows = BLK // ratio
            out_blk = g // ratio
            row_dst = slice((g % ratio) * rows, (g % ratio + 1) * rows)
            for m in range(ratio):
                src = pl.ds(m, rows, stride=ratio)
                for hp in range(N_GROUPS64 // 2):
                    col0 = m * SPLIT + hp * 2 * HEAD_DIM
                    acc_ref[out_blk, row_dst, col0:col0 + 2 * HEAD_DIM] = slot[hp, src, :]
                l_ref[out_blk, row_dst, m * BLK:(m + 1) * BLK] = slot[N_GROUPS64 // 2, src, :]


def _dsw_residue_of(hop):
    if hop == 0:
        return lambda g: g
    prev = _dsw_residue_of(hop - 1)
    prev_dil = DSW_PATTERNS[hop - 1][1]
    return lambda g: prev_dil * (g % DSW_DIL_RATIO) + prev(g // DSW_DIL_RATIO)


def _dsw_unshifted_pattern(q, k, v, hop, state, ehot):
    dil = DSW_PATTERNS[hop][1]
    last = hop == len(DSW_PATTERNS) - 1
    s = q.size // SPLIT
    nb = s // (BLK * dil)
    g_blocks = min(DSW_BLOCKS_PER_STEP, nb)
    ratio = DSW_DIL_RATIO
    residue = _dsw_residue_of(hop)
    view = lambda a: a.reshape(nb, BLK, dil * SPLIT)
    qkv_spec = pl.BlockSpec((g_blocks, BLK, SPLIT), lambda r, j: (j, 0, residue(r)))
    prev_spec = pl.BlockSpec((1, BLK, SPLIT), lambda r, j: (jnp.maximum(j * g_blocks - 1, 0), 0, residue(r)))
    state_spec = lambda width: pl.BlockSpec((g_blocks, BLK, width), lambda r, j: (j, 0, r))
    operands = [view(q), view(k), view(k), view(v), view(v), ehot]
    in_specs = [qkv_spec, prev_spec, qkv_spec, prev_spec, qkv_spec, _const_spec(ehot.shape)]
    if state is not None:
        operands += list(state)
        in_specs += [state_spec(SPLIT), state_spec(BLK)]
    scratch = []
    if last:
        own_spec = lambda width: pl.BlockSpec((g_blocks, BLK, width), lambda r, j: (j, 0, residue(r)))
        out_specs = [own_spec(SPLIT), own_spec(BLK)]
        out_shape = [jax.ShapeDtypeStruct((nb, BLK, dil * SPLIT), F32),
                     jax.ShapeDtypeStruct((nb, BLK, dil * BLK), F32)]
    else:
        assert DSW_PATTERNS[hop + 1][1] == dil * ratio and g_blocks % ratio == 0
        next_spec = lambda width: pl.BlockSpec((g_blocks // ratio, BLK, ratio * width), lambda r, j: (j, 0, r))
        out_specs = [next_spec(SPLIT), next_spec(BLK)]
        out_shape = [jax.ShapeDtypeStruct((nb // ratio, BLK, dil * ratio * SPLIT), F32),
                     jax.ShapeDtypeStruct((nb // ratio, BLK, dil * ratio * BLK), F32)]
        scratch.append(pltpu.VMEM((2, N_GROUPS64 // 2 + 1, BLK, 2 * HEAD_DIM), F32))
    out = pl.pallas_call(
        functools.partial(_dsw_unshifted_kernel, has_state=state is not None, last=last, blocks_per_step=g_blocks),
        grid=(dil, nb // g_blocks),
        in_specs=in_specs,
        out_specs=out_specs,
        out_shape=out_shape,
        scratch_shapes=scratch,
        compiler_params=_params("parallel", "arbitrary"),
        name=f"dsw_unshifted_dil{dil}",
    )(*operands)
    return tuple(out)


DSW_L_LANES_PER_HEAD = BLK // N_GROUPS64


def _dsw_constants():
    ehot = np.zeros((N_GROUPS64 // 2, 4 * BLK, BLK), np.float32)
    xexp = np.zeros((BLK, SPLIT), np.float32)
    for head in range(N_GROUPS64):
        hp, e = divmod(head, 2)
        lanes = slice(head * DSW_L_LANES_PER_HEAD, (head + 1) * DSW_L_LANES_PER_HEAD)
        ehot[hp, e * 2 * BLK:(e + 1) * 2 * BLK, lanes] = 1.0
        xexp[head * DSW_L_LANES_PER_HEAD, head * HEAD_DIM:(head + 1) * HEAD_DIM] = 1.0
    return jnp.asarray(ehot, BF16), jnp.asarray(xexp, BF16)


def _dsw_attention_unshifted(q_views, k_views, v_views):
    ehot, _ = _dsw_constants()
    state = None
    for hop in range(len(DSW_PATTERNS)):
        state = _dsw_unshifted_pattern(q_views[hop], k_views[hop], v_views[hop], hop, state, ehot)
    return state


def _dsw_attention_online(q_views, k_views, v_views):
    state = None
    for idx, (_, dil) in enumerate(DSW_PATTERNS):
        state = _dsw_pattern(q_views[idx], k_views[idx], v_views[idx], dil, state)
    o_view = state[0].reshape(q_views[-1].shape)
    return o_view, jnp.ones(o_view.shape[:2] + (o_view.shape[2] // SPLIT * BLK,), F32)


def _dsw_attention(q_views, k_views, v_views, score_bound):
    return lax.cond(score_bound <= DSW_UNSHIFTED_SCORE_BOUND, _dsw_attention_unshifted, _dsw_attention_online,
                    q_views, k_views, v_views)


def _diff_finalize(o0, o1, lam_ref, sg_ref, o_ref):
    lam_p = lam_ref[...]
    lam = (jnp.exp(jnp.sum(lam_p[0:1] * lam_p[1:2], axis=-1, keepdims=True))
           - jnp.exp(jnp.sum(lam_p[2:3] * lam_p[3:4], axis=-1, keepdims=True)) + LAM_INIT)
    od = o0 - lam * o1
    ms = jnp.mean(od * od, axis=0, keepdims=True)
    y = od * lax.rsqrt(ms + RMS_EPS) * sg_ref[...] * (1.0 - LAM_INIT)
    o_ref[...] = y.T


def _component_queries(qt):
    zeros = jnp.zeros((HEAD_DIM, qt.shape[1]), qt.dtype)
    return (jnp.concatenate([qt[:HEAD_DIM], zeros], axis=0), jnp.concatenate([zeros, qt[HEAD_DIM:]], axis=0))


def _causal_keep(i, k0, tq, tk):
    kpos = k0 + lax.broadcasted_iota(jnp.int32, (tk, tq), 0)
    qpos = i * tq + lax.broadcasted_iota(jnp.int32, (tk, tq), 1)
    return kpos <= qpos


def _diff_unshifted_kernel(qt_ref, k_ref, vt_ref, lam_ref, sg_ref, o_ref, l_sc, acc_sc, *, tq, tk):
    i = pl.program_id(1)
    qc = _component_queries(qt_ref[...])
    l_sc[...] = jnp.zeros(l_sc.shape, F32)
    acc_sc[...] = jnp.zeros(acc_sc.shape, F32)

    def full_chunk(j):
        k0 = pl.multiple_of(j * tk, tk)
        vt = vt_ref[:, pl.ds(k0, tk)]
        for c in range(2):
            st = jnp.dot(k_ref[pl.ds(k0, tk), :], qc[c], preferred_element_type=F32)
            p = jnp.exp2(st)
            l_sc[c] += jnp.sum(p.reshape(tk // 8, 8, tq), axis=0)
            acc_sc[c] += jnp.dot(vt, p.astype(BF16), preferred_element_type=F32)

    def chunk_quad(jq, carry):
        for u in range(4):
            full_chunk(4 * jq + u)
        return carry

    n_quads = lax.shift_right_logical(i, 2)
    lax.fori_loop(0, n_quads, chunk_quad, 0)

    @pl.when((i & 2) == 2)
    def _():
        full_chunk(4 * n_quads)
        full_chunk(4 * n_quads + 1)

    @pl.when((i & 1) == 1)
    def _():
        full_chunk(i - 1)

    sb = DIFF_DIAG_BLOCK
    k0 = pl.multiple_of(i * tk, tk)
    tri = lax.broadcasted_iota(jnp.int32, (sb, sb), 0) <= lax.broadcasted_iota(jnp.int32, (sb, sb), 1)
    for qb in range(tq // sb):
        nk = (qb + 1) * sb
        lanes = slice(qb * sb, (qb + 1) * sb)
        vt = vt_ref[:, pl.ds(k0, nk)]
        for c in range(2):
            st = jnp.dot(k_ref[pl.ds(k0, nk), :], qc[c][:, lanes], preferred_element_type=F32)
            p = jnp.exp2(st)
            p_tri = jnp.where(tri, p[qb * sb:, :], 0.0)
            p = jnp.concatenate([p[:qb * sb, :], p_tri], axis=0) if qb else p_tri
            l_sc[c, :, lanes] += jnp.sum(p.reshape(nk // 8, 8, sb), axis=0)
            acc_sc[c, :, lanes] += jnp.dot(vt, p.astype(BF16), preferred_element_type=F32)

    l0 = jnp.sum(l_sc[0], axis=0, keepdims=True)
    l1 = jnp.sum(l_sc[1], axis=0, keepdims=True)
    _diff_finalize(acc_sc[0] / l0, acc_sc[1] / l1, lam_ref, sg_ref, o_ref)


def _diff_kernel(qt_ref, k_ref, vt_ref, lam_ref, sg_ref, o_ref, m_sc, l_sc, acc_sc, *, tq, tk):
    i = pl.program_id(1)
    qc = _component_queries(qt_ref[...])
    m_sc[...] = jnp.full(m_sc.shape, NEG_INF, F32)
    l_sc[...] = jnp.zeros(l_sc.shape, F32)
    acc_sc[...] = jnp.zeros(acc_sc.shape, F32)

    def chunk(j, masked):
        k0 = pl.multiple_of(j * tk, tk)
        vt = vt_ref[:, pl.ds(k0, tk)]
        for c in range(2):
            st = jnp.dot(k_ref[pl.ds(k0, tk), :], qc[c], preferred_element_type=F32)
            if masked:
                st = jnp.where(_causal_keep(i, k0, tq, tk), st, NEG_INF)
            m_old = m_sc[c]
            m_new = jnp.maximum(m_old, jnp.max(st, axis=0, keepdims=True))
            alpha = jnp.exp2(m_old - m_new)
            p = jnp.exp2(st - m_new)
            l_sc[c] = alpha * l_sc[c] + jnp.sum(p, axis=0, keepdims=True)
            acc_sc[c] = alpha * acc_sc[c] + jnp.dot(vt, p.astype(BF16), preferred_element_type=F32)
            m_sc[c] = m_new

    n_full = (i * tq) // tk

    def full_chunk(j, carry):
        chunk(j, False)
        return carry

    lax.fori_loop(0, n_full, full_chunk, 0)
    for jj in range(tq // tk):
        chunk(n_full + jj, True)

    _diff_finalize(acc_sc[0] / l_sc[0], acc_sc[1] / l_sc[1], lam_ref, sg_ref, o_ref)


def _diff_call(body, scratch, name, qd_t, kd_tok, vd_t, lam_params, subln_col):
    s = qd_t.shape[1]
    tq, tk = DIFF_TQ, DIFF_TK
    assert tq == tk and s % tq == 0 and tq % DIFF_DIAG_BLOCK == 0
    return pl.pallas_call(
        functools.partial(body, tq=tq, tk=tk),
        grid=(DIFF_HEADS, s // tq),
        in_specs=[
            pl.BlockSpec((DIFF_V_DIM, tq), lambda h, i: (h, i)),
            pl.BlockSpec((s, 2 * HEAD_DIM), lambda h, i: (0, h)),
            pl.BlockSpec((DIFF_V_DIM, s), lambda h, i: (h, 0)),
            _const_spec(lam_params.shape),
            _const_spec(subln_col.shape),
        ],
        out_specs=pl.BlockSpec((tq, DIFF_V_DIM), lambda h, i: (i, h)),
        out_shape=jax.ShapeDtypeStruct((s, DIFF_HEADS * DIFF_V_DIM), F32),
        scratch_shapes=scratch(tq),
        compiler_params=_params("parallel", "arbitrary"),
        name=name,
    )(qd_t, kd_tok, vd_t, lam_params, subln_col)


def _diff_attention(qd_t, kd_tok, vd_t, lam_params, subln_col, score_bound):
    acc = lambda tq: pltpu.VMEM((2, DIFF_V_DIM, tq), F32)
    unshifted = functools.partial(
        _diff_call, _diff_unshifted_kernel, lambda tq: [pltpu.VMEM((2, 8, tq), F32), acc(tq)],
        "diff_attention_unshifted")
    online = functools.partial(
        _diff_call, _diff_kernel, lambda tq: [pltpu.VMEM((2, 1, tq), F32), pltpu.VMEM((2, 1, tq), F32), acc(tq)],
        "diff_attention_online")
    return lax.cond(score_bound <= DIFF_UNSHIFTED_SCORE_BOUND, unshifted, online,
                    qd_t, kd_tok, vd_t, lam_params, subln_col)


def _att_out_compute(x_ref, acc_ref, l_ref, xexp_ref, od_ref, g_ref, w_ref, tok_sc, dil):
    n_lane_tiles = SPLIT // LANES
    rows = tok_sc.shape[1] // dil
    for r in range(dil):
        dst = pl.ds(r, rows, stride=dil)
        for c in range(n_lane_tiles):
            col0 = r * SPLIT + c * LANES
            tok_sc[c, dst, :] = acc_ref[0, :, col0:col0 + LANES]
        tok_sc[n_lane_tiles, dst, :] = l_ref[0, :, r * LANES:(r + 1) * LANES]
    acc = jnp.concatenate([tok_sc[c] for c in range(n_lane_tiles)], axis=1)
    r = 1.0 / tok_sc[n_lane_tiles]
    r_hi = r.astype(BF16)
    r_lo = (r - r_hi.astype(F32)).astype(BF16)
    r_exp = (jnp.dot(r_hi, xexp_ref[...], preferred_element_type=F32)
             + jnp.dot(r_lo, xexp_ref[...], preferred_element_type=F32))
    g = g_ref[...]
    y = jnp.concatenate([acc * r_exp, od_ref[...]], axis=-1) * (g * jax.nn.sigmoid(g))
    return x_ref[...] + jnp.dot(y.astype(BF16), w_ref[...], preferred_element_type=F32)


def _gelu(x):
    return 0.5 * x * (1.0 + lax.erf(x * (2.0 ** -0.5)))


def _sgu_compute(x, ng_ref, win_ref, lng_ref, lnb_ref, ws_ref, bs_ref, wout_ref, width):
    tm = x.shape[0]
    ms = jnp.mean(x * x, axis=-1, keepdims=True)
    h = (x * lax.rsqrt(ms + RMS_EPS) * ng_ref[...]).astype(BF16)
    v_raw = jnp.dot(h, win_ref[:, width:2 * width], preferred_element_type=F32)
    u_raw = jnp.dot(h, win_ref[:, 0:width], preferred_element_type=F32)
    v = _gelu(v_raw)
    mu = jnp.mean(v, axis=-1, keepdims=True)
    vc = v - mu
    vn = vc * lax.rsqrt(jnp.mean(vc * vc, axis=-1, keepdims=True) + LN_EPS)
    vn = (vn * lng_ref[...] + lnb_ref[...]).astype(BF16)
    gw = width // SGU_GROUPS
    row = lax.broadcasted_iota(jnp.int32, (SGU_CHUNK, SGU_CHUNK), 0)
    col = lax.broadcasted_iota(jnp.int32, (SGU_CHUNK, SGU_CHUNK), 1)
    causal = col <= row
    sp_groups = []
    for grp in range(SGU_GROUPS):
        ws = jnp.where(causal, ws_ref[grp], 0.0).astype(BF16)
        bias = bs_ref[grp]
        chunks = []
        for c in range(tm // SGU_CHUNK):
            vg = vn[c * SGU_CHUNK:(c + 1) * SGU_CHUNK, grp * gw:(grp + 1) * gw]
            chunks.append(jnp.dot(ws, vg, preferred_element_type=F32) + bias)
        sp_groups.append(jnp.concatenate(chunks, axis=0))
    sp = jnp.concatenate(sp_groups, axis=1)
    g = jnp.dot(h, win_ref[:, 2 * width:3 * width], preferred_element_type=F32)
    u = _gelu(u_raw)
    y = u * sp * (g * jax.nn.sigmoid(g))
    return x + jnp.dot(y.astype(BF16), wout_ref[...], preferred_element_type=F32)


def _att_out_sgu_kernel(x_ref, acc_ref, l_ref, xexp_ref, od_ref, g_ref, watt_ref,
                        ng_ref, win_ref, lng_ref, lnb_ref, ws_ref, bs_ref, wout_ref, o_ref, tok_sc, *, dil, width):
    x1 = _att_out_compute(x_ref, acc_ref, l_ref, xexp_ref, od_ref, g_ref, watt_ref, tok_sc, dil)
    o_ref[...] = _sgu_compute(x1, ng_ref, win_ref, lng_ref, lnb_ref, ws_ref, bs_ref, wout_ref, width)


def _att_out_sgu(x2d, acc_view, l_view, od, g, w_att_out, norm_g, w_in, ln_g, ln_b, w_s, b_s, w_out):
    s, d = x2d.shape
    width = w_out.shape[0]
    tm = ROW_TILE
    dil = DSW_PATTERNS[-1][1]
    rows = tm // dil
    per_blk = BLK // rows
    assert acc_view.shape == (s // (BLK * dil), BLK, dil * SPLIT) and tm % dil == 0 and BLK % rows == 0
    assert l_view.shape == (s // (BLK * dil), BLK, dil * LANES)
    assert tm % SGU_CHUNK == 0 and rows % 8 == 0
    _, xexp = _dsw_constants()
    row = lambda w: pl.BlockSpec((tm, w), lambda i: (i, 0))
    view_spec = lambda w: pl.BlockSpec((1, rows, dil * w), lambda i: (i // per_blk, i % per_blk, 0))
    single = lambda shape: pl.BlockSpec(shape, lambda *_: (0,) * len(shape), pipeline_mode=pl.Buffered(1))
    return pl.pallas_call(
        functools.partial(_att_out_sgu_kernel, dil=dil, width=width),
        grid=(s // tm,),
        in_specs=[row(d), view_spec(SPLIT), view_spec(LANES), _const_spec(xexp.shape), row(SPLIT), row(2 * SPLIT),
                  single(w_att_out.shape),
                  _const_spec((1, d)), single(w_in.shape), _const_spec((1, width)), _const_spec((1, width)),
                  _const_spec(w_s.shape), _const_spec((SGU_GROUPS, SGU_CHUNK, 1)), single(w_out.shape)],
        out_specs=row(d),
        out_shape=jax.ShapeDtypeStruct((s, d), F32),
        scratch_shapes=[pltpu.VMEM((SPLIT // LANES + 1, tm, LANES), F32)],
        compiler_params=_params("parallel"),
        name="att_out_sgu",
    )(x2d, acc_view, l_view, xexp, od, g, w_att_out.astype(BF16),
      norm_g.reshape(1, d), w_in.astype(BF16), ln_g.reshape(1, width), ln_b.reshape(1, width),
      w_s, b_s.reshape(SGU_GROUPS, SGU_CHUNK, 1), w_out.astype(BF16))


def _rope_tables_fm(seq_len):
    pos = jnp.arange(seq_len, dtype=F32)
    inv = 1.0 / (ROPE_THETA ** (jnp.arange(0, 2 * ROT_HALF, 2, dtype=F32) / (2 * ROT_HALF)))
    ang = inv[:, None] * pos[None, :]
    return jnp.cos(ang), jnp.sin(ang)


def kernel(x, att_norm, att_w_in, dsw_q_norm, dsw_k_norm, diff_q_norm, diff_k_norm, diff_lam_q1, diff_lam_k1,
           diff_lam_q2, diff_lam_k2, diff_subln, att_w_out, sgu_norm, sgu_w_in, sgu_ln_g, sgu_ln_b, sgu_w_s,
           sgu_b_s, sgu_w_out):
    b, s, d = x.shape
    assert b == 1 and s % (BLK * DSW_PATTERNS[-1][1]) == 0
    x2d = x.reshape(s, d)
    cos_t, sin_t = _rope_tables_fm(s)

    gains = jnp.stack([dsw_q_norm[0], dsw_k_norm[0], diff_q_norm[0], diff_k_norm[0]])
    gains = jnp.broadcast_to(gains[:, None, :, None], (4, N_GROUPS64, HEAD_DIM, 1))
    qa_views, ka_views, va_views, g, kd_tok, qd_t, vd_t = _att_in(x2d, att_norm[0], att_w_in[0], gains, cos_t, sin_t)
    bound = lambda gq, gk: 1.02 * LOG2_E * HEAD_DIM ** 0.5 * jnp.max(jnp.abs(gq)) * jnp.max(jnp.abs(gk))
    oa_acc, oa_l = _dsw_attention(qa_views, ka_views, va_views, bound(dsw_q_norm[0], dsw_k_norm[0]))
    lam_params = jnp.stack([diff_lam_q1[0], diff_lam_k1[0], diff_lam_q2[0], diff_lam_k2[0]])
    od = _diff_attention(qd_t, kd_tok, vd_t, lam_params, diff_subln[0].reshape(DIFF_V_DIM, 1),
                         bound(diff_q_norm[0], diff_k_norm[0]))
    x2d = _att_out_sgu(x2d, oa_acc, oa_l, od, g, att_w_out[0],
                       sgu_norm[0], sgu_w_in[0], sgu_ln_g[0], sgu_ln_b[0], sgu_w_s[0], sgu_b_s[0], sgu_w_out[0])
    return x2d.reshape(b, s, d)
```

```python
import functools
import math

import numpy as np
import jax
import jax.numpy as jnp
from jax import lax
from jax.experimental import pallas as pl
from jax.experimental.pallas import tpu as pltpu

F32 = jnp.float32
BF16 = jnp.bfloat16

LANES = 128
HEAD_DIM = 64
ROT_HALF = 8
ROPE_THETA = 500000.0
BLK = 128
N_GROUPS64 = 8
SPLIT = N_GROUPS64 * HEAD_DIM
DSW_PATTERNS = ((128, 1), (512, 4), (2048, 16))
DSW_DIL_RATIO = 4
DIFF_HEADS = 4
DIFF_V_DIM = 128
SGU_GROUPS = 8
SGU_CHUNK = 128
RMS_EPS = 1e-6
LN_EPS = 1e-5
NEG_INF = -1e30
LAM_INIT = 0.8 - 0.6 * math.exp(-0.3 * 0)
LOG2_E = math.log2(math.e)
DIFF_UNSHIFTED_SCORE_BOUND = 96.0
DSW_UNSHIFTED_SCORE_BOUND = 96.0

V7X_VMEM_LIMIT_BYTES = 56 * 1024 * 1024

ROW_TILE = 512
ATT_IN_PARTS = 2
DSW_BLOCKS_PER_STEP = 8
DIFF_TQ = 1024
DIFF_TK = 1024
DIFF_DIAG_BLOCK = 512


def _params(*sem):
    return pltpu.CompilerParams(dimension_semantics=sem, vmem_limit_bytes=V7X_VMEM_LIMIT_BYTES)


def _const_spec(shape):
    nd = len(shape)
    return pl.BlockSpec(shape, lambda *_: (0,) * nd)


def _norm_rope_fm(p, gain, cos, sin, scale):
    t = p.shape[1]
    p3 = p.reshape(N_GROUPS64, HEAD_DIM, t)
    ms = jnp.mean(p3 * p3, axis=1, keepdims=True)
    y = p3 * lax.rsqrt(ms + RMS_EPS) * gain
    x1 = y[:, 0:ROT_HALF, :]
    x2 = y[:, ROT_HALF:2 * ROT_HALF, :]
    out = jnp.concatenate([x1 * cos - x2 * sin, x2 * cos + x1 * sin, y[:, 2 * ROT_HALF:, :]], axis=1)
    if scale != 1.0:
        out = out * scale
    return out.reshape(SPLIT, t)


def _emit_pattern_views(tok, tok_sc, view_refs, part, n_parts):
    t = tok.shape[0]
    n_lane_tiles = SPLIT // LANES
    for c in range(n_lane_tiles):
        tok_sc[c] = tok[:, c * LANES:(c + 1) * LANES]
    for (_, dil), ref in zip(DSW_PATTERNS, view_refs):
        if dil == 1:
            ref[part * t:(part + 1) * t, :] = tok.astype(BF16)
            continue
        n = t // dil
        for r in range(dil):
            rows = jnp.concatenate([tok_sc[c, pl.ds(r, n, stride=dil), :] for c in range(n_lane_tiles)], axis=1)
            ref[0, part * n:(part + 1) * n, r * SPLIT:(r + 1) * SPLIT] = rows.astype(BF16)


def _att_in_kernel(x_ref, ng_ref, wtok_ref, wfm_ref, gains_ref, cos_ref, sin_ref, *refs):
    n_pat = len(DSW_PATTERNS)
    qa_views, ka_views, va_views = refs[0:n_pat], refs[n_pat:2 * n_pat], refs[2 * n_pat:3 * n_pat]
    g_ref, kd_ref, qd_ref, vd_ref, tok_sc = refs[3 * n_pat:]
    scales = (HEAD_DIM ** -0.5 * LOG2_E, 1.0, HEAD_DIM ** -0.5 * LOG2_E, 1.0)
    n_parts = ATT_IN_PARTS
    t = x_ref.shape[0] // n_parts
    for part in range(n_parts):
        rows = slice(part * t, (part + 1) * t)
        x = x_ref[rows, :]
        ms = jnp.mean(x * x, axis=-1, keepdims=True)
        h = (x * lax.rsqrt(ms + RMS_EPS) * ng_ref[...]).astype(BF16)
        tok = jnp.dot(h, wtok_ref[...], preferred_element_type=F32)
        _emit_pattern_views(tok[:, :SPLIT], tok_sc.at[part, 0], va_views, part, n_parts)
        g_ref[rows, :] = tok[:, SPLIT:]
        cos = cos_ref[:, rows][None]
        sin = sin_ref[:, rows][None]
        def feature_major(idx):
            return lax.dot_general(wfm_ref[idx * SPLIT:(idx + 1) * SPLIT, :], h, (((1,), (1,)), ((), ())),
                                   preferred_element_type=F32)

        def normed(idx):
            return _norm_rope_fm(feature_major(idx), gains_ref[idx], cos, sin, scales[idx])

        _emit_pattern_views(normed(0).T, tok_sc.at[part, 1], qa_views, part, n_parts)
        _emit_pattern_views(normed(1).T, tok_sc.at[part, 2], ka_views, part, n_parts)
        qd_ref[:, rows] = normed(2).astype(BF16)
        kd_ref[rows, :] = normed(3).T.astype(BF16)
        vd_ref[:, rows] = feature_major(4).astype(BF16)


def _att_in(x2d, norm_g, w_in, gains, cos_t, sin_t):
    s, d = x2d.shape
    tm = ROW_TILE
    w = w_in.astype(BF16)
    w_tok = jnp.concatenate([w[:, 2 * SPLIT:4 * SPLIT], w[:, 7 * SPLIT:8 * SPLIT]], axis=1)
    w_fm = jnp.concatenate([w[:, 0:2 * SPLIT], w[:, 4 * SPLIT:7 * SPLIT]], axis=1).T
    view_shapes, view_specs = [], []
    for _, dil in DSW_PATTERNS:
        if dil == 1:
            view_shapes.append(jax.ShapeDtypeStruct((s, SPLIT), BF16))
            view_specs.append(pl.BlockSpec((tm, SPLIT), lambda i: (i, 0)))
            continue
        rows = tm // dil
        assert tm % dil == 0 and BLK % rows == 0 and rows % (16 * ATT_IN_PARTS) == 0
        per_blk = BLK // rows
        view_shapes.append(jax.ShapeDtypeStruct((s // (BLK * dil), BLK, dil * SPLIT), BF16))
        view_specs.append(pl.BlockSpec((1, rows, dil * SPLIT),
                                       functools.partial(lambda i, per_blk: (i // per_blk, i % per_blk, 0),
                                                         per_blk=per_blk)))
    fm = jax.ShapeDtypeStruct((SPLIT, s), BF16)
    fm_spec = pl.BlockSpec((SPLIT, tm), lambda i: (0, i))
    outs = pl.pallas_call(
        _att_in_kernel,
        grid=(s // tm,),
        in_specs=[
            pl.BlockSpec((tm, d), lambda i: (i, 0)),
            _const_spec((1, d)),
            _const_spec(w_tok.shape),
            _const_spec(w_fm.shape),
            _const_spec(gains.shape),
            pl.BlockSpec((ROT_HALF, tm), lambda i: (0, i)),
            pl.BlockSpec((ROT_HALF, tm), lambda i: (0, i)),
        ],
        out_specs=view_specs * 3 + [
            pl.BlockSpec((tm, 2 * SPLIT), lambda i: (i, 0)),
            pl.BlockSpec((tm, SPLIT), lambda i: (i, 0)),
            fm_spec, fm_spec,
        ],
        out_shape=view_shapes * 3 + [
            jax.ShapeDtypeStruct((s, 2 * SPLIT), F32),
            jax.ShapeDtypeStruct((s, SPLIT), BF16),
            fm, fm,
        ],
        scratch_shapes=[pltpu.VMEM((ATT_IN_PARTS, 3, SPLIT // LANES, tm // ATT_IN_PARTS, LANES), F32)],
        compiler_params=_params("parallel"),
        name="att_in_proj",
    )(x2d, norm_g.reshape(1, d), w_tok, w_fm, gains, cos_t, sin_t)
    n_pat = len(DSW_PATTERNS)
    return (outs[0:n_pat], outs[n_pat:2 * n_pat], outs[2 * n_pat:3 * n_pat]) + tuple(outs[3 * n_pat:])


DSW_L_LANES_PER_HEAD = BLK // N_GROUPS64


def _dsw_online_pair(q_pair, kk, vv, mask, acc_in, l_tile_in, hp, last):
    low_half = lax.broadcasted_iota(jnp.int32, (BLK, 2 * HEAD_DIM), 1) < HEAD_DIM
    halves, state_cols = [], []
    for e in range(2):
        head = 2 * hp + e
        sel = low_half if e == 0 else jnp.logical_not(low_half)
        qm = jnp.where(sel, q_pair, jnp.zeros_like(q_pair))
        sc = lax.dot_general(qm, kk, (((1,), (1,)), ((), ())), preferred_element_type=F32)
        sc = jnp.where(mask, sc, NEG_INF)
        m_new = jnp.max(sc, axis=-1, keepdims=True)
        if acc_in is not None:
            m_old = l_tile_in[:, head * DSW_L_LANES_PER_HEAD:head * DSW_L_LANES_PER_HEAD + 1]
            m_new = jnp.maximum(m_old, m_new)
            alpha = jnp.exp2(m_old - m_new)
        p = jnp.exp2(sc - m_new)
        l = jnp.sum(p, axis=-1, keepdims=True)
        pv = jnp.dot(p.astype(BF16), vv, preferred_element_type=F32)
        if acc_in is not None:
            l = l + alpha
            pv = pv + alpha * acc_in
        halves.append(pv / l)
        state_cols.append(jnp.ones_like(l) if last else m_new + jnp.log2(l))
    return jnp.where(low_half, halves[0], halves[1]), state_cols


def _dsw_hop_kernel(bound_ref, *refs, has_state, last, blocks_per_step):
    refs = list(refs)
    q_ref, kprev_ref, kcur_ref, vprev_ref, vcur_ref, ehot_ref = refs[:6]
    pos = 6
    if has_state:
        accin_ref, lin_ref = refs[pos:pos + 2]
        pos += 2
    acc_ref, l_ref = refs[pos:pos + 2]
    perm_sc = None if last else refs[pos + 2]
    g_blocks = blocks_per_step
    step = pl.program_id(1)

    def window(prev_ref, cur_ref, g, cols):
        before = prev_ref[0, :, cols] if g == 0 else cur_ref[g - 1, :, cols]
        return jnp.concatenate([before, cur_ref[g, :, cols]], axis=0)

    def emit(g, accs, l_tile):
        n_pairs = N_GROUPS64 // 2
        if last:
            for hp in range(n_pairs):
                acc_ref[g, :, hp * 2 * HEAD_DIM:(hp + 1) * 2 * HEAD_DIM] = accs[hp]
            l_ref[g] = l_tile
            return
        slot = perm_sc.at[g % 2]
        for hp in range(n_pairs):
            slot[hp] = accs[hp]
        slot[n_pairs] = l_tile
        ratio = DSW_DIL_RATIO
        rows = BLK // ratio
        out_blk = g // ratio
        row_dst = slice((g % ratio) * rows, (g % ratio + 1) * rows)
        for m in range(ratio):
            src = pl.ds(m, rows, stride=ratio)
            for hp in range(n_pairs):
                col0 = m * SPLIT + hp * 2 * HEAD_DIM
                acc_ref[out_blk, row_dst, col0:col0 + 2 * HEAD_DIM] = slot[hp, src, :]
            l_ref[out_blk, row_dst, m * BLK:(m + 1) * BLK] = slot[n_pairs, src, :]

    unshifted = bound_ref[0] <= DSW_UNSHIFTED_SCORE_BOUND

    @pl.when(jnp.logical_not(unshifted))
    def _():
        qi = lax.broadcasted_iota(jnp.int32, (BLK, 2 * BLK), 0) + BLK
        kj = lax.broadcasted_iota(jnp.int32, (BLK, 2 * BLK), 1)
        dist = qi - kj
        band = (dist >= 0) & (dist <= BLK)
        lane_head = lax.broadcasted_iota(jnp.int32, (BLK, BLK), 1) // DSW_L_LANES_PER_HEAD
        for g in range(g_blocks):
            mask = band & (kj >= jnp.where(step == 0, BLK, 0)) if g == 0 else band
            l_in = lin_ref[g] if has_state else None
            accs = []
            l_tile = jnp.zeros((BLK, BLK), F32)
            for hp in range(N_GROUPS64 // 2):
                cols = slice(hp * 2 * HEAD_DIM, (hp + 1) * 2 * HEAD_DIM)
                acc, state_cols = _dsw_online_pair(
                    q_ref[g, :, cols], window(kprev_ref, kcur_ref, g, cols), window(vprev_ref, vcur_ref, g, cols),
                    mask, accin_ref[g, :, cols] if has_state else None, l_in, hp, last)
                accs.append(acc)
                for e in range(2):
                    l_tile = jnp.where(lane_head == 2 * hp + e, state_cols[e], l_tile)
            emit(g, accs, l_tile)

    @pl.when(unshifted)
    def _():
        _dsw_unshifted_blocks(q_ref, kprev_ref, kcur_ref, vprev_ref, vcur_ref, ehot_ref,
                              accin_ref if has_state else None, lin_ref if has_state else None,
                              window, emit, step, g_blocks)


def _dsw_unshifted_blocks(q_ref, kprev_ref, kcur_ref, vprev_ref, vcur_ref, ehot_ref, accin_ref, lin_ref,
                          window, emit, step, g_blocks):
    has_state = accin_ref is not None

    qi = (lax.broadcasted_iota(jnp.int32, (2 * BLK, 2 * BLK), 0) & (BLK - 1)) + BLK
    kj = lax.broadcasted_iota(jnp.int32, (2 * BLK, 2 * BLK), 1)
    dist = qi - kj
    band = (dist >= 0) & (dist <= BLK)
    q_low = lax.broadcasted_iota(jnp.int32, (BLK, 2 * HEAD_DIM), 1) < HEAD_DIM
    v_low = lax.broadcasted_iota(jnp.int32, (2 * BLK, 2 * HEAD_DIM), 1) < HEAD_DIM

    for g in range(g_blocks):
        if g == 0:
            mask = band & (kj >= jnp.where(step == 0, BLK, 0))
        else:
            mask = band
        l_tile = lin_ref[g] if has_state else jnp.zeros((BLK, BLK), F32)
        accs = []
        for hp in range(N_GROUPS64 // 2):
            cols = slice(hp * 2 * HEAD_DIM, (hp + 1) * 2 * HEAD_DIM)
            q_pair = q_ref[g, :, cols]
            kk = window(kprev_ref, kcur_ref, g, cols)
            vv = window(vprev_ref, vcur_ref, g, cols)
            qz = jnp.zeros_like(q_pair)
            vz = jnp.zeros_like(vv)
            q2 = jnp.concatenate([jnp.where(q_low, q_pair, qz), jnp.where(q_low, qz, q_pair)], axis=0)
            sc = lax.dot_general(q2, kk, (((1,), (1,)), ((), ())), preferred_element_type=F32)
            p2 = jnp.where(mask, jnp.exp2(sc), 0.0).astype(BF16)
            p_cat = jnp.concatenate([p2[:BLK], p2[BLK:]], axis=1)
            v_split = jnp.concatenate([jnp.where(v_low, vv, vz), jnp.where(v_low, vz, vv)], axis=0)
            rhs = jnp.concatenate([v_split, ehot_ref[hp]], axis=1)
            out2 = jnp.dot(p_cat, rhs, preferred_element_type=F32)
            acc = out2[:, :2 * HEAD_DIM]
            if has_state:
                acc = acc + accin_ref[g, :, cols]
            l_tile = l_tile + out2[:, 2 * HEAD_DIM:]
            accs.append(acc)
        emit(g, accs, l_tile)


def _dsw_residue_of(hop):
    if hop == 0:
        return lambda g: g
    prev = _dsw_residue_of(hop - 1)
    prev_dil = DSW_PATTERNS[hop - 1][1]
    return lambda g: prev_dil * (g % DSW_DIL_RATIO) + prev(g // DSW_DIL_RATIO)


def _dsw_pattern(q, k, v, hop, state, ehot, bound):
    dil = DSW_PATTERNS[hop][1]
    last = hop == len(DSW_PATTERNS) - 1
    s = q.size // SPLIT
    nb = s // (BLK * dil)
    g_blocks = min(DSW_BLOCKS_PER_STEP, nb)
    ratio = DSW_DIL_RATIO
    residue = _dsw_residue_of(hop)
    view = lambda a: a.reshape(nb, BLK, dil * SPLIT)
    qkv_spec = pl.BlockSpec((g_blocks, BLK, SPLIT), lambda r, j, b: (j, 0, residue(r)))
    prev_spec = pl.BlockSpec((1, BLK, SPLIT), lambda r, j, b: (jnp.maximum(j * g_blocks - 1, 0), 0, residue(r)))
    state_spec = lambda width: pl.BlockSpec((g_blocks, BLK, width), lambda r, j, b: (j, 0, r))
    operands = [view(q), view(k), view(k), view(v), view(v), ehot]
    in_specs = [qkv_spec, prev_spec, qkv_spec, prev_spec, qkv_spec,
                pl.BlockSpec(ehot.shape, lambda r, j, b: (0, 0, 0))]
    if state is not None:
        operands += list(state)
        in_specs += [state_spec(SPLIT), state_spec(BLK)]
    scratch = []
    if last:
        own_spec = lambda width: pl.BlockSpec((g_blocks, BLK, width), lambda r, j, b: (j, 0, residue(r)))
        out_specs = [own_spec(SPLIT), own_spec(BLK)]
        out_shape = [jax.ShapeDtypeStruct((nb, BLK, dil * SPLIT), F32),
                     jax.ShapeDtypeStruct((nb, BLK, dil * BLK), F32)]
    else:
        assert DSW_PATTERNS[hop + 1][1] == dil * ratio and g_blocks % ratio == 0
        next_spec = lambda width: pl.BlockSpec((g_blocks // ratio, BLK, ratio * width), lambda r, j, b: (j, 0, r))
        out_specs = [next_spec(SPLIT), next_spec(BLK)]
        out_shape = [jax.ShapeDtypeStruct((nb // ratio, BLK, dil * ratio * SPLIT), F32),
                     jax.ShapeDtypeStruct((nb // ratio, BLK, dil * ratio * BLK), F32)]
        scratch.append(pltpu.VMEM((2, N_GROUPS64 // 2 + 1, BLK, 2 * HEAD_DIM), F32))
    out = pl.pallas_call(
        functools.partial(_dsw_hop_kernel, has_state=state is not None, last=last, blocks_per_step=g_blocks),
        grid_spec=pltpu.PrefetchScalarGridSpec(
            num_scalar_prefetch=1, grid=(dil, nb // g_blocks), in_specs=in_specs, out_specs=out_specs,
            scratch_shapes=scratch),
        out_shape=out_shape,
        compiler_params=_params("parallel", "arbitrary"),
        name=f"dsw_attention_dil{dil}",
    )(bound, *operands)
    return tuple(out)


def _dsw_constants():
    ehot = np.zeros((N_GROUPS64 // 2, 4 * BLK, BLK), np.float32)
    xexp = np.zeros((BLK, SPLIT), np.float32)
    for head in range(N_GROUPS64):
        hp, e = divmod(head, 2)
        lanes = slice(head * DSW_L_LANES_PER_HEAD, (head + 1) * DSW_L_LANES_PER_HEAD)
        ehot[hp, e * 2 * BLK:(e + 1) * 2 * BLK, lanes] = 1.0
        xexp[head * DSW_L_LANES_PER_HEAD, head * HEAD_DIM:(head + 1) * HEAD_DIM] = 1.0
    return jnp.asarray(ehot, BF16), jnp.asarray(xexp, BF16)


def _dsw_attention(q_views, k_views, v_views, score_bound):
    ehot, _ = _dsw_constants()
    bound = score_bound.reshape(1).astype(F32)
    state = None
    for hop in range(len(DSW_PATTERNS)):
        state = _dsw_pattern(q_views[hop], k_views[hop], v_views[hop], hop, state, ehot, bound)
    return state


def _diff_finalize(o0, o1, lam_ref, sg_ref, o_ref):
    lam_p = lam_ref[...]
    lam = (jnp.exp(jnp.sum(lam_p[0:1] * lam_p[1:2], axis=-1, keepdims=True))
           - jnp.exp(jnp.sum(lam_p[2:3] * lam_p[3:4], axis=-1, keepdims=True)) + LAM_INIT)
    od = o0 - lam * o1
    ms = jnp.mean(od * od, axis=0, keepdims=True)
    y = od * lax.rsqrt(ms + RMS_EPS) * sg_ref[...] * (1.0 - LAM_INIT)
    o_ref[...] = y.T


def _component_queries(qt):
    zeros = jnp.zeros((HEAD_DIM, qt.shape[1]), qt.dtype)
    return (jnp.concatenate([qt[:HEAD_DIM], zeros], axis=0), jnp.concatenate([zeros, qt[HEAD_DIM:]], axis=0))


def _causal_keep(i, k0, tq, tk):
    kpos = k0 + lax.broadcasted_iota(jnp.int32, (tk, tq), 0)
    qpos = i * tq + lax.broadcasted_iota(jnp.int32, (tk, tq), 1)
    return kpos <= qpos


def _diff_unshifted_kernel(qt_ref, k_ref, vt_ref, lam_ref, sg_ref, o_ref, l_sc, acc_sc, *, tq, tk):
    i = pl.program_id(1)
    qc = _component_queries(qt_ref[...])
    l_sc[...] = jnp.zeros(l_sc.shape, F32)
    acc_sc[...] = jnp.zeros(acc_sc.shape, F32)

    def full_chunk(j):
        k0 = pl.multiple_of(j * tk, tk)
        vt = vt_ref[:, pl.ds(k0, tk)]
        for c in range(2):
            st = jnp.dot(k_ref[pl.ds(k0, tk), :], qc[c], preferred_element_type=F32)
            p = jnp.exp2(st)
            l_sc[c] += jnp.sum(p.reshape(tk // 8, 8, tq), axis=0)
            acc_sc[c] += jnp.dot(vt, p.astype(BF16), preferred_element_type=F32)

    def chunk_quad(jq, carry):
        for u in range(4):
            full_chunk(4 * jq + u)
        return carry

    n_quads = lax.shift_right_logical(i, 2)
    lax.fori_loop(0, n_quads, chunk_quad, 0)

    @pl.when((i & 2) == 2)
    def _():
        full_chunk(4 * n_quads)
        full_chunk(4 * n_quads + 1)

    @pl.when((i & 1) == 1)
    def _():
        full_chunk(i - 1)

    sb = DIFF_DIAG_BLOCK
    k0 = pl.multiple_of(i * tk, tk)
    tri = lax.broadcasted_iota(jnp.int32, (sb, sb), 0) <= lax.broadcasted_iota(jnp.int32, (sb, sb), 1)
    for qb in range(tq // sb):
        nk = (qb + 1) * sb
        lanes = slice(qb * sb, (qb + 1) * sb)
        vt = vt_ref[:, pl.ds(k0, nk)]
        for c in range(2):
            st = jnp.dot(k_ref[pl.ds(k0, nk), :], qc[c][:, lanes], preferred_element_type=F32)
            p = jnp.exp2(st)
            p_tri = jnp.where(tri, p[qb * sb:, :], 0.0)
            p = jnp.concatenate([p[:qb * sb, :], p_tri], axis=0) if qb else p_tri
            l_sc[c, :, lanes] += jnp.sum(p.reshape(nk // 8, 8, sb), axis=0)
            acc_sc[c, :, lanes] += jnp.dot(vt, p.astype(BF16), preferred_element_type=F32)

    l0 = jnp.sum(l_sc[0], axis=0, keepdims=True)
    l1 = jnp.sum(l_sc[1], axis=0, keepdims=True)
    _diff_finalize(acc_sc[0] / l0, acc_sc[1] / l1, lam_ref, sg_ref, o_ref)


def _diff_kernel(qt_ref, k_ref, vt_ref, lam_ref, sg_ref, o_ref, m_sc, l_sc, acc_sc, *, tq, tk):
    i = pl.program_id(1)
    qc = _component_queries(qt_ref[...])
    m_sc[...] = jnp.full(m_sc.shape, NEG_INF, F32)
    l_sc[...] = jnp.zeros(l_sc.shape, F32)
    acc_sc[...] = jnp.zeros(acc_sc.shape, F32)

    def chunk(j, masked):
        k0 = pl.multiple_of(j * tk, tk)
        vt = vt_ref[:, pl.ds(k0, tk)]
        for c in range(2):
            st = jnp.dot(k_ref[pl.ds(k0, tk), :], qc[c], preferred_element_type=F32)
            if masked:
                st = jnp.where(_causal_keep(i, k0, tq, tk), st, NEG_INF)
            m_old = m_sc[c]
            m_new = jnp.maximum(m_old, jnp.max(st, axis=0, keepdims=True))
            alpha = jnp.exp2(m_old - m_new)
            p = jnp.exp2(st - m_new)
            l_sc[c] = alpha * l_sc[c] + jnp.sum(p, axis=0, keepdims=True)
            acc_sc[c] = alpha * acc_sc[c] + jnp.dot(vt, p.astype(BF16), preferred_element_type=F32)
            m_sc[c] = m_new

    n_full = (i * tq) // tk

    def full_chunk(j, carry):
        chunk(j, False)
        return carry

    lax.fori_loop(0, n_full, full_chunk, 0)
    for jj in range(tq // tk):
        chunk(n_full + jj, True)

    _diff_finalize(acc_sc[0] / l_sc[0], acc_sc[1] / l_sc[1], lam_ref, sg_ref, o_ref)


def _diff_dispatch_kernel(bound_ref, qt_ref, k_ref, vt_ref, lam_ref, sg_ref, o_ref, m_sc, l1_sc, l8_sc, acc_sc, *,
                          tq, tk):
    unshifted = bound_ref[0] <= DIFF_UNSHIFTED_SCORE_BOUND

    @pl.when(unshifted)
    def _():
        _diff_unshifted_kernel(qt_ref, k_ref, vt_ref, lam_ref, sg_ref, o_ref, l8_sc, acc_sc, tq=tq, tk=tk)

    @pl.when(jnp.logical_not(unshifted))
    def _():
        _diff_kernel(qt_ref, k_ref, vt_ref, lam_ref, sg_ref, o_ref, m_sc, l1_sc, acc_sc, tq=tq, tk=tk)


def _diff_attention(qd_t, kd_tok, vd_t, lam_params, subln_col, score_bound):
    s = qd_t.shape[1]
    tq, tk = DIFF_TQ, DIFF_TK
    assert tq == tk and s % tq == 0 and tq % DIFF_DIAG_BLOCK == 0
    grid_spec = pltpu.PrefetchScalarGridSpec(
        num_scalar_prefetch=1,
        grid=(DIFF_HEADS, s // tq),
        in_specs=[
            pl.BlockSpec((DIFF_V_DIM, tq), lambda h, i, bound: (h, i)),
            pl.BlockSpec((s, 2 * HEAD_DIM), lambda h, i, bound: (0, h)),
            pl.BlockSpec((DIFF_V_DIM, s), lambda h, i, bound: (h, 0)),
            pl.BlockSpec(lam_params.shape, lambda h, i, bound: (0, 0)),
            pl.BlockSpec(subln_col.shape, lambda h, i, bound: (0, 0)),
        ],
        out_specs=pl.BlockSpec((tq, DIFF_V_DIM), lambda h, i, bound: (i, h)),
        scratch_shapes=[pltpu.VMEM((2, 1, tq), F32), pltpu.VMEM((2, 1, tq), F32), pltpu.VMEM((2, 8, tq), F32),
                        pltpu.VMEM((2, DIFF_V_DIM, tq), F32)],
    )
    return pl.pallas_call(
        functools.partial(_diff_dispatch_kernel, tq=tq, tk=tk),
        grid_spec=grid_spec,
        out_shape=jax.ShapeDtypeStruct((s, DIFF_HEADS * DIFF_V_DIM), F32),
        compiler_params=_params("parallel", "arbitrary"),
        name="diff_attention",
    )(score_bound.reshape(1).astype(F32), qd_t, kd_tok, vd_t, lam_params, subln_col)


def _att_out_compute(x_ref, acc_ref, l_ref, xexp_ref, od_ref, g_ref, w_ref, tok_sc, dil):
    n_lane_tiles = SPLIT // LANES
    rows = tok_sc.shape[1] // dil
    for r in range(dil):
        dst = pl.ds(r, rows, stride=dil)
        for c in range(n_lane_tiles):
            col0 = r * SPLIT + c * LANES
            tok_sc[c, dst, :] = acc_ref[0, :, col0:col0 + LANES]
        tok_sc[n_lane_tiles, dst, :] = l_ref[0, :, r * LANES:(r + 1) * LANES]
    acc = jnp.concatenate([tok_sc[c] for c in range(n_lane_tiles)], axis=1)
    r = 1.0 / tok_sc[n_lane_tiles]
    r_hi = r.astype(BF16)
    r_lo = (r - r_hi.astype(F32)).astype(BF16)
    r_exp = (jnp.dot(r_hi, xexp_ref[...], preferred_element_type=F32)
             + jnp.dot(r_lo, xexp_ref[...], preferred_element_type=F32))
    g = g_ref[...]
    y = jnp.concatenate([acc * r_exp, od_ref[...]], axis=-1) * (g * jax.nn.sigmoid(g))
    return x_ref[...] + jnp.dot(y.astype(BF16), w_ref[...], preferred_element_type=F32)


def _gelu(x):
    return 0.5 * x * (1.0 + lax.erf(x * (2.0 ** -0.5)))


def _sgu_compute(x, ng_ref, win_ref, lng_ref, lnb_ref, ws_ref, bs_ref, wout_ref, width):
    tm = x.shape[0]
    ms = jnp.mean(x * x, axis=-1, keepdims=True)
    h = (x * lax.rsqrt(ms + RMS_EPS) * ng_ref[...]).astype(BF16)
    v_raw = jnp.dot(h, win_ref[:, width:2 * width], preferred_element_type=F32)
    u_raw = jnp.dot(h, win_ref[:, 0:width], preferred_element_type=F32)
    v = _gelu(v_raw)
    mu = jnp.mean(v, axis=-1, keepdims=True)
    vc = v - mu
    vn = vc * lax.rsqrt(jnp.mean(vc * vc, axis=-1, keepdims=True) + LN_EPS)
    vn = (vn * lng_ref[...] + lnb_ref[...]).astype(BF16)
    gw = width // SGU_GROUPS
    row = lax.broadcasted_iota(jnp.int32, (SGU_CHUNK, SGU_CHUNK), 0)
    col = lax.broadcasted_iota(jnp.int32, (SGU_CHUNK, SGU_CHUNK), 1)
    causal = col <= row
    sp_groups = []
    for grp in range(SGU_GROUPS):
        ws = jnp.where(causal, ws_ref[grp], 0.0).astype(BF16)
        bias = bs_ref[grp]
        chunks = []
        for c in range(tm // SGU_CHUNK):
            vg = vn[c * SGU_CHUNK:(c + 1) * SGU_CHUNK, grp * gw:(grp + 1) * gw]
            chunks.append(jnp.dot(ws, vg, preferred_element_type=F32) + bias)
        sp_groups.append(jnp.concatenate(chunks, axis=0))
    sp = jnp.concatenate(sp_groups, axis=1)
    g = jnp.dot(h, win_ref[:, 2 * width:3 * width], preferred_element_type=F32)
    u = _gelu(u_raw)
    y = u * sp * (g * jax.nn.sigmoid(g))
    return x + jnp.dot(y.astype(BF16), wout_ref[...], preferred_element_type=F32)


def _att_out_sgu_kernel(x_ref, acc_ref, l_ref, xexp_ref, od_ref, g_ref, watt_ref,
                        ng_ref, win_ref, lng_ref, lnb_ref, ws_ref, bs_ref, wout_ref, o_ref, tok_sc, *, dil, width):
    x1 = _att_out_compute(x_ref, acc_ref, l_ref, xexp_ref, od_ref, g_ref, watt_ref, tok_sc, dil)
    o_ref[...] = _sgu_compute(x1, ng_ref, win_ref, lng_ref, lnb_ref, ws_ref, bs_ref, wout_ref, width)


def _att_out_sgu(x2d, acc_view, l_view, od, g, w_att_out, norm_g, w_in, ln_g, ln_b, w_s, b_s, w_out):
    s, d = x2d.shape
    width = w_out.shape[0]
    tm = ROW_TILE
    dil = DSW_PATTERNS[-1][1]
    rows = tm // dil
    per_blk = BLK // rows
    assert acc_view.shape == (s // (BLK * dil), BLK, dil * SPLIT) and tm % dil == 0 and BLK % rows == 0
    assert l_view.shape == (s // (BLK * dil), BLK, dil * LANES)
    assert tm % SGU_CHUNK == 0 and rows % 8 == 0
    _, xexp = _dsw_constants()
    row = lambda w: pl.BlockSpec((tm, w), lambda i: (i, 0))
    view_spec = lambda w: pl.BlockSpec((1, rows, dil * w), lambda i: (i // per_blk, i % per_blk, 0))
    single = lambda shape: pl.BlockSpec(shape, lambda *_: (0,) * len(shape), pipeline_mode=pl.Buffered(1))
    return pl.pallas_call(
        functools.partial(_att_out_sgu_kernel, dil=dil, width=width),
        grid=(s // tm,),
        in_specs=[row(d), view_spec(SPLIT), view_spec(LANES), _const_spec(xexp.shape), row(SPLIT), row(2 * SPLIT),
                  single(w_att_out.shape),
                  _const_spec((1, d)), single(w_in.shape), _const_spec((1, width)), _const_spec((1, width)),
                  _const_spec(w_s.shape), _const_spec((SGU_GROUPS, SGU_CHUNK, 1)), single(w_out.shape)],
        out_specs=row(d),
        out_shape=jax.ShapeDtypeStruct((s, d), F32),
        scratch_shapes=[pltpu.VMEM((SPLIT // LANES + 1, tm, LANES), F32)],
        compiler_params=_params("parallel"),
        name="att_out_sgu",
    )(x2d, acc_view, l_view, xexp, od, g, w_att_out.astype(BF16),
      norm_g.reshape(1, d), w_in.astype(BF16), ln_g.reshape(1, width), ln_b.reshape(1, width),
      w_s, b_s.reshape(SGU_GROUPS, SGU_CHUNK, 1), w_out.astype(BF16))


def _rope_tables_fm(seq_len):
    pos = jnp.arange(seq_len, dtype=F32)
    inv = 1.0 / (ROPE_THETA ** (jnp.arange(0, 2 * ROT_HALF, 2, dtype=F32) / (2 * ROT_HALF)))
    ang = inv[:, None] * pos[None, :]
    return jnp.cos(ang), jnp.sin(ang)


def kernel(x, att_norm, att_w_in, dsw_q_norm, dsw_k_norm, diff_q_norm, diff_k_norm, diff_lam_q1, diff_lam_k1,
           diff_lam_q2, diff_lam_k2, diff_subln, att_w_out, sgu_norm, sgu_w_in, sgu_ln_g, sgu_ln_b, sgu_w_s,
           sgu_b_s, sgu_w_out):
    b, s, d = x.shape
    assert b == 1 and s % (BLK * DSW_PATTERNS[-1][1]) == 0
    x2d = x.reshape(s, d)
    cos_t, sin_t = _rope_tables_fm(s)

    gains = jnp.stack([dsw_q_norm[0], dsw_k_norm[0], diff_q_norm[0], diff_k_norm[0]])
    gains = jnp.broadcast_to(gains[:, None, :, None], (4, N_GROUPS64, HEAD_DIM, 1))
    qa_views, ka_views, va_views, g, kd_tok, qd_t, vd_t = _att_in(x2d, att_norm[0], att_w_in[0], gains, cos_t, sin_t)
    bound = lambda gq, gk: 1.02 * LOG2_E * HEAD_DIM ** 0.5 * jnp.max(jnp.abs(gq)) * jnp.max(jnp.abs(gk))
    oa_acc, oa_l = _dsw_attention(qa_views, ka_views, va_views, bound(dsw_q_norm[0], dsw_k_norm[0]))
    lam_params = jnp.stack([diff_lam_q1[0], diff_lam_k1[0], diff_lam_q2[0], diff_lam_k2[0]])
    od = _diff_attention(qd_t, kd_tok, vd_t, lam_params, diff_subln[0].reshape(DIFF_V_DIM, 1),
                         bound(diff_q_norm[0], diff_k_norm[0]))
    x2d = _att_out_sgu(x2d, oa_acc, oa_l, od, g, att_w_out[0],
                       sgu_norm[0], sgu_w_in[0], sgu_ln_g[0], sgu_ln_b[0], sgu_w_s[0], sgu_b_s[0], sgu_w_out[0])
    return x2d.reshape(b, s, d)
```

```python
import functools
import math

import numpy as np
import jax
import jax.numpy as jnp
from jax import lax
from jax.experimental import pallas as pl
from jax.experimental.pallas import tpu as pltpu

F32 = jnp.float32
BF16 = jnp.bfloat16

LANES = 128
SUBLANES = 8
HEAD_DIM = 64
ROT_HALF = 8
ROPE_THETA = 500000.0
BLK = 128
N_GROUPS64 = 8
SPLIT = N_GROUPS64 * HEAD_DIM
DSW_PATTERNS = ((128, 1), (512, 4), (2048, 16))
DSW_DIL_RATIO = 4
DIFF_HEADS = 4
DIFF_V_DIM = 128
SGU_GROUPS = 8
SGU_CHUNK = 128
RMS_EPS = 1e-6
LN_EPS = 1e-5
NEG_INF = -1e30
LAM_INIT = 0.8 - 0.6 * math.exp(-0.3 * 0)
LOG2_E = math.log2(math.e)
DIFF_UNSHIFTED_SCORE_BOUND = 96.0
DSW_UNSHIFTED_SCORE_BOUND = 96.0

V7X_VMEM_LIMIT_BYTES = 56 * 1024 * 1024

ROW_TILE = 512
ATT_IN_PARTS = 2
DSW_BLOCKS_PER_STEP = 8
DIFF_TQ = 1024
DIFF_TK = 1024
DIFF_DIAG_BLOCK = 512


def _params(*sem):
    return pltpu.CompilerParams(dimension_semantics=sem, vmem_limit_bytes=V7X_VMEM_LIMIT_BYTES)


def _const_spec(shape):
    nd = len(shape)
    return pl.BlockSpec(shape, lambda *_: (0,) * nd)


def _norm_rope_fm(p, gain, cos, sin, scale):
    t = p.shape[1]
    p3 = p.reshape(N_GROUPS64, HEAD_DIM, t)
    ms = jnp.mean(p3 * p3, axis=1, keepdims=True)
    y = p3 * lax.rsqrt(ms + RMS_EPS) * gain
    x1 = y[:, 0:ROT_HALF, :]
    x2 = y[:, ROT_HALF:2 * ROT_HALF, :]
    out = jnp.concatenate([x1 * cos - x2 * sin, x2 * cos + x1 * sin, y[:, 2 * ROT_HALF:, :]], axis=1)
    if scale != 1.0:
        out = out * scale
    return out.reshape(SPLIT, t)


def _emit_pattern_views(tok, tok_sc, view_refs, part, n_parts):
    t = tok.shape[0]
    n_lane_tiles = SPLIT // LANES
    for c in range(n_lane_tiles):
        tok_sc[c] = tok[:, c * LANES:(c + 1) * LANES]
    for (_, dil), ref in zip(DSW_PATTERNS, view_refs):
        if dil == 1:
            ref[part * t:(part + 1) * t, :] = tok.astype(BF16)
            continue
        n = t // dil
        for r in range(dil):
            rows = jnp.concatenate([tok_sc[c, pl.ds(r, n, stride=dil), :] for c in range(n_lane_tiles)], axis=1)
            ref[0, part * n:(part + 1) * n, r * SPLIT:(r + 1) * SPLIT] = rows.astype(BF16)


def _att_in_kernel(x_ref, ng_ref, wtok_ref, wfm_ref, gains_ref, cos_ref, sin_ref, *refs):
    n_pat = len(DSW_PATTERNS)
    qa_views, ka_views, va_views = refs[0:n_pat], refs[n_pat:2 * n_pat], refs[2 * n_pat:3 * n_pat]
    g_ref, kd_ref, qd_ref, vd_ref, tok_sc = refs[3 * n_pat:]
    scales = (HEAD_DIM ** -0.5 * LOG2_E, 1.0, HEAD_DIM ** -0.5 * LOG2_E, 1.0)
    n_parts = ATT_IN_PARTS
    t = x_ref.shape[0] // n_parts
    for part in range(n_parts):
        rows = slice(part * t, (part + 1) * t)
        x = x_ref[rows, :]
        ms = jnp.mean(x * x, axis=-1, keepdims=True)
        h = (x * lax.rsqrt(ms + RMS_EPS) * ng_ref[...]).astype(BF16)
        tok = jnp.dot(h, wtok_ref[...], preferred_element_type=F32)
        _emit_pattern_views(tok[:, :SPLIT], tok_sc.at[part, 0], va_views, part, n_parts)
        g_ref[rows, :] = tok[:, SPLIT:]
        cos = cos_ref[:, rows][None]
        sin = sin_ref[:, rows][None]
        def feature_major(idx):
            return lax.dot_general(wfm_ref[idx * SPLIT:(idx + 1) * SPLIT, :], h, (((1,), (1,)), ((), ())),
                                   preferred_element_type=F32)

        def normed(idx):
            return _norm_rope_fm(feature_major(idx), gains_ref[idx], cos, sin, scales[idx])

        _emit_pattern_views(normed(0).T, tok_sc.at[part, 1], qa_views, part, n_parts)
        _emit_pattern_views(normed(1).T, tok_sc.at[part, 2], ka_views, part, n_parts)
        qd_ref[:, rows] = normed(2).astype(BF16)
        kd_ref[rows, :] = normed(3).T.astype(BF16)
        vd_ref[:, rows] = feature_major(4).astype(BF16)


def _att_in(x2d, norm_g, w_in, gains, cos_t, sin_t):
    s, d = x2d.shape
    tm = ROW_TILE
    w = w_in.astype(BF16)
    w_tok = jnp.concatenate([w[:, 2 * SPLIT:4 * SPLIT], w[:, 7 * SPLIT:8 * SPLIT]], axis=1)
    w_fm = jnp.concatenate([w[:, 0:2 * SPLIT], w[:, 4 * SPLIT:7 * SPLIT]], axis=1).T
    view_shapes, view_specs = [], []
    for _, dil in DSW_PATTERNS:
        if dil == 1:
            view_shapes.append(jax.ShapeDtypeStruct((s, SPLIT), BF16))
            view_specs.append(pl.BlockSpec((tm, SPLIT), lambda i: (i, 0)))
            continue
        rows = tm // dil
        assert tm % dil == 0 and BLK % rows == 0 and rows % (16 * ATT_IN_PARTS) == 0
        per_blk = BLK // rows
        view_shapes.append(jax.ShapeDtypeStruct((s // (BLK * dil), BLK, dil * SPLIT), BF16))
        view_specs.append(pl.BlockSpec((1, rows, dil * SPLIT),
                                       functools.partial(lambda i, per_blk: (i // per_blk, i % per_blk, 0),
                                                         per_blk=per_blk)))
    fm = jax.ShapeDtypeStruct((SPLIT, s), BF16)
    fm_spec = pl.BlockSpec((SPLIT, tm), lambda i: (0, i))
    outs = pl.pallas_call(
        _att_in_kernel,
        grid=(s // tm,),
        in_specs=[
            pl.BlockSpec((tm, d), lambda i: (i, 0)),
            _const_spec((1, d)),
            _const_spec(w_tok.shape),
            _const_spec(w_fm.shape),
            _const_spec(gains.shape),
            pl.BlockSpec((ROT_HALF, tm), lambda i: (0, i)),
            pl.BlockSpec((ROT_HALF, tm), lambda i: (0, i)),
        ],
        out_specs=view_specs * 3 + [
            pl.BlockSpec((tm, 2 * SPLIT), lambda i: (i, 0)),
            pl.BlockSpec((tm, SPLIT), lambda i: (i, 0)),
            fm_spec, fm_spec,
        ],
        out_shape=view_shapes * 3 + [
            jax.ShapeDtypeStruct((s, 2 * SPLIT), F32),
            jax.ShapeDtypeStruct((s, SPLIT), BF16),
            fm, fm,
        ],
        scratch_shapes=[pltpu.VMEM((ATT_IN_PARTS, 3, SPLIT // LANES, tm // ATT_IN_PARTS, LANES), F32)],
        compiler_params=_params("parallel"),
        name="att_in_proj",
    )(x2d, norm_g.reshape(1, d), w_tok, w_fm, gains, cos_t, sin_t)
    n_pat = len(DSW_PATTERNS)
    return (outs[0:n_pat], outs[n_pat:2 * n_pat], outs[2 * n_pat:3 * n_pat]) + tuple(outs[3 * n_pat:])


DSW_L_LANES_PER_HEAD = BLK // N_GROUPS64


def _dsw_online_pair(q_pair, kk, vv, mask, acc_in, l_tile_in, hp, last):
    low_half = lax.broadcasted_iota(jnp.int32, (BLK, 2 * HEAD_DIM), 1) < HEAD_DIM
    halves, state_cols = [], []
    for e in range(2):
        head = 2 * hp + e
        sel = low_half if e == 0 else jnp.logical_not(low_half)
        qm = jnp.where(sel, q_pair, jnp.zeros_like(q_pair))
        sc = lax.dot_general(qm, kk, (((1,), (1,)), ((), ())), preferred_element_type=F32)
        sc = jnp.where(mask, sc, NEG_INF)
        m_new = jnp.max(sc, axis=-1, keepdims=True)
        if acc_in is not None:
            m_old = l_tile_in[:, head * DSW_L_LANES_PER_HEAD:head * DSW_L_LANES_PER_HEAD + 1]
            m_new = jnp.maximum(m_old, m_new)
            alpha = jnp.exp2(m_old - m_new)
        p = jnp.exp2(sc - m_new)
        l = jnp.sum(p, axis=-1, keepdims=True)
        pv = jnp.dot(p.astype(BF16), vv, preferred_element_type=F32)
        if acc_in is not None:
            l = l + alpha
            pv = pv + alpha * acc_in
        halves.append(pv / l)
        state_cols.append(jnp.ones_like(l) if last else m_new + jnp.log2(l))
    return jnp.where(low_half, halves[0], halves[1]), state_cols


def _dsw_hop_kernel(bound_ref, *refs, has_state, last, blocks_per_step):
    refs = list(refs)
    q_ref, kprev_ref, kcur_ref, vprev_ref, vcur_ref, ehot_ref = refs[:6]
    pos = 6
    if has_state:
        accin_ref, lin_ref = refs[pos:pos + 2]
        pos += 2
    acc_ref, l_ref = refs[pos:pos + 2]
    perm_sc = None if last else refs[pos + 2]
    g_blocks = blocks_per_step
    step = pl.program_id(1)

    def window(prev_ref, cur_ref, g, cols):
        before = prev_ref[0, :, cols] if g == 0 else cur_ref[g - 1, :, cols]
        return jnp.concatenate([before, cur_ref[g, :, cols]], axis=0)

    def emit(g, accs, l_tile):
        n_pairs = N_GROUPS64 // 2
        if last:
            for hp in range(n_pairs):
                acc_ref[g, :, hp * 2 * HEAD_DIM:(hp + 1) * 2 * HEAD_DIM] = accs[hp]
            l_ref[g] = l_tile
            return
        slot = perm_sc.at[g % 2]
        for hp in range(n_pairs):
            slot[hp] = accs[hp]
        slot[n_pairs] = l_tile
        ratio = DSW_DIL_RATIO
        rows = BLK // ratio
        out_blk = g // ratio
        row_dst = slice((g % ratio) * rows, (g % ratio + 1) * rows)
        for m in range(ratio):
            src = pl.ds(m, rows, stride=ratio)
            for hp in range(n_pairs):
                col0 = m * SPLIT + hp * 2 * HEAD_DIM
                acc_ref[out_blk, row_dst, col0:col0 + 2 * HEAD_DIM] = slot[hp, src, :]
            l_ref[out_blk, row_dst, m * BLK:(m + 1) * BLK] = slot[n_pairs, src, :]

    unshifted = bound_ref[0] <= DSW_UNSHIFTED_SCORE_BOUND

    @pl.when(jnp.logical_not(unshifted))
    def _():
        qi = lax.broadcasted_iota(jnp.int32, (BLK, 2 * BLK), 0) + BLK
        kj = lax.broadcasted_iota(jnp.int32, (BLK, 2 * BLK), 1)
        dist = qi - kj
        band = (dist >= 0) & (dist <= BLK)
        lane_head = lax.broadcasted_iota(jnp.int32, (BLK, BLK), 1) // DSW_L_LANES_PER_HEAD
        for g in range(g_blocks):
            mask = band & (kj >= jnp.where(step == 0, BLK, 0)) if g == 0 else band
            l_in = lin_ref[g] if has_state else None
            accs = []
            l_tile = jnp.zeros((BLK, BLK), F32)
            for hp in range(N_GROUPS64 // 2):
                cols = slice(hp * 2 * HEAD_DIM, (hp + 1) * 2 * HEAD_DIM)
                acc, state_cols = _dsw_online_pair(
                    q_ref[g, :, cols], window(kprev_ref, kcur_ref, g, cols), window(vprev_ref, vcur_ref, g, cols),
                    mask, accin_ref[g, :, cols] if has_state else None, l_in, hp, last)
                accs.append(acc)
                for e in range(2):
                    l_tile = jnp.where(lane_head == 2 * hp + e, state_cols[e], l_tile)
            emit(g, accs, l_tile)

    @pl.when(unshifted)
    def _():
        _dsw_unshifted_blocks(q_ref, kprev_ref, kcur_ref, vprev_ref, vcur_ref, ehot_ref,
                              accin_ref if has_state else None, lin_ref if has_state else None,
                              window, emit, step, g_blocks)


def _dsw_unshifted_blocks(q_ref, kprev_ref, kcur_ref, vprev_ref, vcur_ref, ehot_ref, accin_ref, lin_ref,
                          window, emit, step, g_blocks):
    has_state = accin_ref is not None

    qi = (lax.broadcasted_iota(jnp.int32, (2 * BLK, 2 * BLK), 0) & (BLK - 1)) + BLK
    kj = lax.broadcasted_iota(jnp.int32, (2 * BLK, 2 * BLK), 1)
    dist = qi - kj
    band = (dist >= 0) & (dist <= BLK)
    q_low = lax.broadcasted_iota(jnp.int32, (BLK, 2 * HEAD_DIM), 1) < HEAD_DIM
    v_low = lax.broadcasted_iota(jnp.int32, (2 * BLK, 2 * HEAD_DIM), 1) < HEAD_DIM

    for g in range(g_blocks):
        if g == 0:
            mask = band & (kj >= jnp.where(step == 0, BLK, 0))
        else:
            mask = band
        l_tile = lin_ref[g] if has_state else jnp.zeros((BLK, BLK), F32)
        accs = []
        for hp in range(N_GROUPS64 // 2):
            cols = slice(hp * 2 * HEAD_DIM, (hp + 1) * 2 * HEAD_DIM)
            q_pair = q_ref[g, :, cols]
            kk = window(kprev_ref, kcur_ref, g, cols)
            vv = window(vprev_ref, vcur_ref, g, cols)
            qz = jnp.zeros_like(q_pair)
            vz = jnp.zeros_like(vv)
            q2 = jnp.concatenate([jnp.where(q_low, q_pair, qz), jnp.where(q_low, qz, q_pair)], axis=0)
            sc = lax.dot_general(q2, kk, (((1,), (1,)), ((), ())), preferred_element_type=F32)
            p2 = jnp.where(mask, jnp.exp2(sc), 0.0).astype(BF16)
            p_cat = jnp.concatenate([p2[:BLK], p2[BLK:]], axis=1)
            v_split = jnp.concatenate([jnp.where(v_low, vv, vz), jnp.where(v_low, vz, vv)], axis=0)
            rhs = jnp.concatenate([v_split, ehot_ref[hp]], axis=1)
            out2 = jnp.dot(p_cat, rhs, preferred_element_type=F32)
            acc = out2[:, :2 * HEAD_DIM]
            if has_state:
                acc = acc + accin_ref[g, :, cols]
            l_tile = l_tile + out2[:, 2 * HEAD_DIM:]
            accs.append(acc)
        emit(g, accs, l_tile)


def _dsw_residue_of(hop):
    if hop == 0:
        return lambda g: g
    prev = _dsw_residue_of(hop - 1)
    prev_dil = DSW_PATTERNS[hop - 1][1]
    return lambda g: prev_dil * (g % DSW_DIL_RATIO) + prev(g // DSW_DIL_RATIO)


def _dsw_pattern(q, k, v, hop, state, ehot, bound):
    dil = DSW_PATTERNS[hop][1]
    last = hop == len(DSW_PATTERNS) - 1
    s = q.size // SPLIT
    nb = s // (BLK * dil)
    g_blocks = min(DSW_BLOCKS_PER_STEP, nb)
    ratio = DSW_DIL_RATIO
    residue = _dsw_residue_of(hop)
    view = lambda a: a.reshape(nb, BLK, dil * SPLIT)
    qkv_spec = pl.BlockSpec((g_blocks, BLK, SPLIT), lambda r, j, b: (j, 0, residue(r)))
    prev_spec = pl.BlockSpec((1, BLK, SPLIT), lambda r, j, b: (jnp.maximum(j * g_blocks - 1, 0), 0, residue(r)))
    state_spec = lambda width: pl.BlockSpec((g_blocks, BLK, width), lambda r, j, b: (j, 0, r))
    operands = [view(q), view(k), view(k), view(v), view(v), ehot]
    in_specs = [qkv_spec, prev_spec, qkv_spec, prev_spec, qkv_spec,
                pl.BlockSpec(ehot.shape, lambda r, j, b: (0, 0, 0))]
    if state is not None:
        operands += list(state)
        in_specs += [state_spec(SPLIT), state_spec(BLK)]
    scratch = []
    if last:
        own_spec = lambda width: pl.BlockSpec((g_blocks, BLK, width), lambda r, j, b: (j, 0, residue(r)))
        out_specs = [own_spec(SPLIT), own_spec(BLK)]
        out_shape = [jax.ShapeDtypeStruct((nb, BLK, dil * SPLIT), F32),
                     jax.ShapeDtypeStruct((nb, BLK, dil * BLK), F32)]
    else:
        assert DSW_PATTERNS[hop + 1][1] == dil * ratio and g_blocks % ratio == 0
        next_spec = lambda width: pl.BlockSpec((g_blocks // ratio, BLK, ratio * width), lambda r, j, b: (j, 0, r))
        out_specs = [next_spec(SPLIT), next_spec(BLK)]
        out_shape = [jax.ShapeDtypeStruct((nb // ratio, BLK, dil * ratio * SPLIT), F32),
                     jax.ShapeDtypeStruct((nb // ratio, BLK, dil * ratio * BLK), F32)]
        scratch.append(pltpu.VMEM((2, N_GROUPS64 // 2 + 1, BLK, 2 * HEAD_DIM), F32))
    out = pl.pallas_call(
        functools.partial(_dsw_hop_kernel, has_state=state is not None, last=last, blocks_per_step=g_blocks),
        grid_spec=pltpu.PrefetchScalarGridSpec(
            num_scalar_prefetch=1, grid=(dil, nb // g_blocks), in_specs=in_specs, out_specs=out_specs,
            scratch_shapes=scratch),
        out_shape=out_shape,
        compiler_params=_params("parallel", "arbitrary"),
        name=f"dsw_attention_dil{dil}",
    )(bound, *operands)
    return tuple(out)


def _dsw_constants():
    ehot = np.zeros((N_GROUPS64 // 2, 4 * BLK, BLK), np.float32)
    xexp = np.zeros((BLK, SPLIT), np.float32)
    for head in range(N_GROUPS64):
        hp, e = divmod(head, 2)
        lanes = slice(head * DSW_L_LANES_PER_HEAD, (head + 1) * DSW_L_LANES_PER_HEAD)
        ehot[hp, e * 2 * BLK:(e + 1) * 2 * BLK, lanes] = 1.0
        xexp[head * DSW_L_LANES_PER_HEAD, head * HEAD_DIM:(head + 1) * HEAD_DIM] = 1.0
    return jnp.asarray(ehot, BF16), jnp.asarray(xexp, BF16)


def _dsw_attention(q_views, k_views, v_views, score_bound):
    ehot, _ = _dsw_constants()
    bound = score_bound.reshape(1).astype(F32)
    state = None
    for hop in range(len(DSW_PATTERNS)):
        state = _dsw_pattern(q_views[hop], k_views[hop], v_views[hop], hop, state, ehot, bound)
    return state


def _diff_finalize(o0, o1, lam_ref, sg_ref, o_ref):
    lam_p = lam_ref[...]
    lam = (jnp.exp(jnp.sum(lam_p[0:1] * lam_p[1:2], axis=-1, keepdims=True))
           - jnp.exp(jnp.sum(lam_p[2:3] * lam_p[3:4], axis=-1, keepdims=True)) + LAM_INIT)
    od = o0 - lam * o1
    ms = jnp.mean(od * od, axis=0, keepdims=True)
    y = od * lax.rsqrt(ms + RMS_EPS) * sg_ref[...] * (1.0 - LAM_INIT)
    o_ref[...] = y.T


def _component_queries(qt):
    zeros = jnp.zeros((HEAD_DIM, qt.shape[1]), qt.dtype)
    return (jnp.concatenate([qt[:HEAD_DIM], zeros], axis=0), jnp.concatenate([zeros, qt[HEAD_DIM:]], axis=0))


def _causal_keep(i, k0, tq, tk):
    kpos = k0 + lax.broadcasted_iota(jnp.int32, (tk, tq), 0)
    qpos = i * tq + lax.broadcasted_iota(jnp.int32, (tk, tq), 1)
    return kpos <= qpos


def _diff_unshifted_kernel(qt_ref, k_ref, vt_ref, lam_ref, sg_ref, o_ref, l_sc, acc_sc, *, tq, tk):
    i = pl.program_id(1)
    qc = _component_queries(qt_ref[...])
    l_sc[...] = jnp.zeros(l_sc.shape, F32)
    acc_sc[...] = jnp.zeros(acc_sc.shape, F32)

    def full_chunk(j):
        k0 = pl.multiple_of(j * tk, tk)
        vt = vt_ref[:, pl.ds(k0, tk)]
        for c in range(2):
            st = jnp.dot(k_ref[pl.ds(k0, tk), :], qc[c], preferred_element_type=F32)
            p = jnp.exp2(st)
            l_sc[c] += jnp.sum(p.reshape(tk // SUBLANES, SUBLANES, tq), axis=0)
            acc_sc[c] += jnp.dot(vt, p.astype(BF16), preferred_element_type=F32)

    def chunk_quad(jq, carry):
        for u in range(4):
            full_chunk(4 * jq + u)
        return carry

    n_quads = lax.shift_right_logical(i, 2)
    lax.fori_loop(0, n_quads, chunk_quad, 0)

    @pl.when((i & 2) == 2)
    def _():
        full_chunk(4 * n_quads)
        full_chunk(4 * n_quads + 1)

    @pl.when((i & 1) == 1)
    def _():
        full_chunk(i - 1)

    sb = DIFF_DIAG_BLOCK
    k0 = pl.multiple_of(i * tk, tk)
    tri = lax.broadcasted_iota(jnp.int32, (sb, sb), 0) <= lax.broadcasted_iota(jnp.int32, (sb, sb), 1)
    for qb in range(tq // sb):
        nk = (qb + 1) * sb
        lanes = slice(qb * sb, (qb + 1) * sb)
        vt = vt_ref[:, pl.ds(k0, nk)]
        for c in range(2):
            st = jnp.dot(k_ref[pl.ds(k0, nk), :], qc[c][:, lanes], preferred_element_type=F32)
            p = jnp.exp2(st)
            p_tri = jnp.where(tri, p[qb * sb:, :], 0.0)
            p = jnp.concatenate([p[:qb * sb, :], p_tri], axis=0) if qb else p_tri
            l_sc[c, :, lanes] += jnp.sum(p.reshape(nk // SUBLANES, SUBLANES, sb), axis=0)
            acc_sc[c, :, lanes] += jnp.dot(vt, p.astype(BF16), preferred_element_type=F32)

    l0 = jnp.sum(l_sc[0], axis=0, keepdims=True)
    l1 = jnp.sum(l_sc[1], axis=0, keepdims=True)
    _diff_finalize(acc_sc[0] / l0, acc_sc[1] / l1, lam_ref, sg_ref, o_ref)


def _diff_kernel(qt_ref, k_ref, vt_ref, lam_ref, sg_ref, o_ref, m_sc, l_sc, acc_sc, *, tq, tk):
    i = pl.program_id(1)
    qc = _component_queries(qt_ref[...])
    m_sc[...] = jnp.full(m_sc.shape, NEG_INF, F32)
    l_sc[...] = jnp.zeros(l_sc.shape, F32)
    acc_sc[...] = jnp.zeros(acc_sc.shape, F32)

    def chunk(j, masked):
        k0 = pl.multiple_of(j * tk, tk)
        vt = vt_ref[:, pl.ds(k0, tk)]
        for c in range(2):
            st = jnp.dot(k_ref[pl.ds(k0, tk), :], qc[c], preferred_element_type=F32)
            if masked:
                st = jnp.where(_causal_keep(i, k0, tq, tk), st, NEG_INF)
            m_old = m_sc[c]
            m_new = jnp.maximum(m_old, jnp.max(st, axis=0, keepdims=True))
            alpha = jnp.exp2(m_old - m_new)
            p = jnp.exp2(st - m_new)
            l_sc[c] = alpha * l_sc[c] + jnp.sum(p, axis=0, keepdims=True)
            acc_sc[c] = alpha * acc_sc[c] + jnp.dot(vt, p.astype(BF16), preferred_element_type=F32)
            m_sc[c] = m_new

    n_full = (i * tq) // tk

    def full_chunk(j, carry):
        chunk(j, False)
        return carry

    lax.fori_loop(0, n_full, full_chunk, 0)
    for jj in range(tq // tk):
        chunk(n_full + jj, True)

    _diff_finalize(acc_sc[0] / l_sc[0], acc_sc[1] / l_sc[1], lam_ref, sg_ref, o_ref)


def _diff_dispatch_kernel(bound_ref, qt_ref, k_ref, vt_ref, lam_ref, sg_ref, o_ref, m_sc, l1_sc, l8_sc, acc_sc, *,
                          tq, tk):
    unshifted = bound_ref[0] <= DIFF_UNSHIFTED_SCORE_BOUND

    @pl.when(unshifted)
    def _():
        _diff_unshifted_kernel(qt_ref, k_ref, vt_ref, lam_ref, sg_ref, o_ref, l8_sc, acc_sc, tq=tq, tk=tk)

    @pl.when(jnp.logical_not(unshifted))
    def _():
        _diff_kernel(qt_ref, k_ref, vt_ref, lam_ref, sg_ref, o_ref, m_sc, l1_sc, acc_sc, tq=tq, tk=tk)


def _diff_attention(qd_t, kd_tok, vd_t, lam_params, subln_col, score_bound):
    s = qd_t.shape[1]
    tq, tk = DIFF_TQ, DIFF_TK
    assert tq == tk and s % tq == 0 and tq % DIFF_DIAG_BLOCK == 0
    grid_spec = pltpu.PrefetchScalarGridSpec(
        num_scalar_prefetch=1,
        grid=(DIFF_HEADS, s // tq),
        in_specs=[
            pl.BlockSpec((DIFF_V_DIM, tq), lambda h, i, bound: (h, i)),
            pl.BlockSpec((s, 2 * HEAD_DIM), lambda h, i, bound: (0, h)),
            pl.BlockSpec((DIFF_V_DIM, s), lambda h, i, bound: (h, 0)),
            pl.BlockSpec(lam_params.shape, lambda h, i, bound: (0, 0)),
            pl.BlockSpec(subln_col.shape, lambda h, i, bound: (0, 0)),
        ],
        out_specs=pl.BlockSpec((tq, DIFF_V_DIM), lambda h, i, bound: (i, h)),
        scratch_shapes=[pltpu.VMEM((2, 1, tq), F32), pltpu.VMEM((2, 1, tq), F32), pltpu.VMEM((2, SUBLANES, tq), F32),
                        pltpu.VMEM((2, DIFF_V_DIM, tq), F32)],
    )
    return pl.pallas_call(
        functools.partial(_diff_dispatch_kernel, tq=tq, tk=tk),
        grid_spec=grid_spec,
        out_shape=jax.ShapeDtypeStruct((s, DIFF_HEADS * DIFF_V_DIM), F32),
        compiler_params=_params("parallel", "arbitrary"),
        name="diff_attention",
    )(score_bound.reshape(1).astype(F32), qd_t, kd_tok, vd_t, lam_params, subln_col)


def _att_out_compute(x_ref, acc_ref, l_ref, xexp_ref, od_ref, g_ref, w_ref, tok_sc, dil):
    n_lane_tiles = SPLIT // LANES
    rows = tok_sc.shape[1] // dil
    for r in range(dil):
        dst = pl.ds(r, rows, stride=dil)
        for c in range(n_lane_tiles):
            col0 = r * SPLIT + c * LANES
            tok_sc[c, dst, :] = acc_ref[0, :, col0:col0 + LANES]
        tok_sc[n_lane_tiles, dst, :] = l_ref[0, :, r * LANES:(r + 1) * LANES]
    acc = jnp.concatenate([tok_sc[c] for c in range(n_lane_tiles)], axis=1)
    r = 1.0 / tok_sc[n_lane_tiles]
    r_hi = r.astype(BF16)
    r_lo = (r - r_hi.astype(F32)).astype(BF16)
    r_exp = (jnp.dot(r_hi, xexp_ref[...], preferred_element_type=F32)
             + jnp.dot(r_lo, xexp_ref[...], preferred_element_type=F32))
    g = g_ref[...]
    y = jnp.concatenate([acc * r_exp, od_ref[...]], axis=-1) * (g * jax.nn.sigmoid(g))
    return x_ref[...] + jnp.dot(y.astype(BF16), w_ref[...], preferred_element_type=F32)


def _gelu(x):
    return 0.5 * x * (1.0 + lax.erf(x * (2.0 ** -0.5)))


def _sgu_compute(x, ng_ref, win_ref, lng_ref, lnb_ref, ws_ref, bs_ref, wout_ref, width):
    tm = x.shape[0]
    ms = jnp.mean(x * x, axis=-1, keepdims=True)
    h = (x * lax.rsqrt(ms + RMS_EPS) * ng_ref[...]).astype(BF16)
    v_raw = jnp.dot(h, win_ref[:, width:2 * width], preferred_element_type=F32)
    u_raw = jnp.dot(h, win_ref[:, 0:width], preferred_element_type=F32)
    v = _gelu(v_raw)
    mu = jnp.mean(v, axis=-1, keepdims=True)
    vc = v - mu
    vn = vc * lax.rsqrt(jnp.mean(vc * vc, axis=-1, keepdims=True) + LN_EPS)
    vn = (vn * lng_ref[...] + lnb_ref[...]).astype(BF16)
    gw = width // SGU_GROUPS
    row = lax.broadcasted_iota(jnp.int32, (SGU_CHUNK, SGU_CHUNK), 0)
    col = lax.broadcasted_iota(jnp.int32, (SGU_CHUNK, SGU_CHUNK), 1)
    causal = col <= row
    sp_groups = []
    for grp in range(SGU_GROUPS):
        ws = jnp.where(causal, ws_ref[grp], 0.0).astype(BF16)
        bias = bs_ref[grp]
        chunks = []
        for c in range(tm // SGU_CHUNK):
            vg = vn[c * SGU_CHUNK:(c + 1) * SGU_CHUNK, grp * gw:(grp + 1) * gw]
            chunks.append(jnp.dot(ws, vg, preferred_element_type=F32) + bias)
        sp_groups.append(jnp.concatenate(chunks, axis=0))
    sp = jnp.concatenate(sp_groups, axis=1)
    g = jnp.dot(h, win_ref[:, 2 * width:3 * width], preferred_element_type=F32)
    u = _gelu(u_raw)
    y = u * sp * (g * jax.nn.sigmoid(g))
    return x + jnp.dot(y.astype(BF16), wout_ref[...], preferred_element_type=F32)


def _att_out_sgu_kernel(x_ref, acc_ref, l_ref, xexp_ref, od_ref, g_ref, watt_ref,
                        ng_ref, win_ref, lng_ref, lnb_ref, ws_ref, bs_ref, wout_ref, o_ref, tok_sc, *, dil, width):
    x1 = _att_out_compute(x_ref, acc_ref, l_ref, xexp_ref, od_ref, g_ref, watt_ref, tok_sc, dil)
    o_ref[...] = _sgu_compute(x1, ng_ref, win_ref, lng_ref, lnb_ref, ws_ref, bs_ref, wout_ref, width)


def _att_out_sgu(x2d, acc_view, l_view, od, g, w_att_out, norm_g, w_in, ln_g, ln_b, w_s, b_s, w_out):
    s, d = x2d.shape
    width = w_out.shape[0]
    tm = ROW_TILE
    dil = DSW_PATTERNS[-1][1]
    rows = tm // dil
    per_blk = BLK // rows
    assert acc_view.shape == (s // (BLK * dil), BLK, dil * SPLIT) and tm % dil == 0 and BLK % rows == 0
    assert l_view.shape == (s // (BLK * dil), BLK, dil * LANES)
    assert tm % SGU_CHUNK == 0 and rows % 8 == 0
    _, xexp = _dsw_constants()
    row = lambda w: pl.BlockSpec((tm, w), lambda i: (i, 0))
    view_spec = lambda w: pl.BlockSpec((1, rows, dil * w), lambda i: (i // per_blk, i % per_blk, 0))
    single = lambda shape: pl.BlockSpec(shape, lambda *_: (0,) * len(shape), pipeline_mode=pl.Buffered(1))
    return pl.pallas_call(
        functools.partial(_att_out_sgu_kernel, dil=dil, width=width),
        grid=(s // tm,),
        in_specs=[row(d), view_spec(SPLIT), view_spec(LANES), _const_spec(xexp.shape), row(SPLIT), row(2 * SPLIT),
                  single(w_att_out.shape),
                  _const_spec((1, d)), single(w_in.shape), _const_spec((1, width)), _const_spec((1, width)),
                  _const_spec(w_s.shape), _const_spec((SGU_GROUPS, SGU_CHUNK, 1)), single(w_out.shape)],
        out_specs=row(d),
        out_shape=jax.ShapeDtypeStruct((s, d), F32),
        scratch_shapes=[pltpu.VMEM((SPLIT // LANES + 1, tm, LANES), F32)],
        compiler_params=_params("parallel"),
        name="att_out_sgu",
    )(x2d, acc_view, l_view, xexp, od, g, w_att_out.astype(BF16),
      norm_g.reshape(1, d), w_in.astype(BF16), ln_g.reshape(1, width), ln_b.reshape(1, width),
      w_s, b_s.reshape(SGU_GROUPS, SGU_CHUNK, 1), w_out.astype(BF16))


def _rope_tables_fm(seq_len):
    pos = jnp.arange(seq_len, dtype=F32)
    inv = 1.0 / (ROPE_THETA ** (jnp.arange(0, 2 * ROT_HALF, 2, dtype=F32) / (2 * ROT_HALF)))
    ang = inv[:, None] * pos[None, :]
    return jnp.cos(ang), jnp.sin(ang)


def kernel(x, att_norm, att_w_in, dsw_q_norm, dsw_k_norm, diff_q_norm, diff_k_norm, diff_lam_q1, diff_lam_k1,
           diff_lam_q2, diff_lam_k2, diff_subln, att_w_out, sgu_norm, sgu_w_in, sgu_ln_g, sgu_ln_b, sgu_w_s,
           sgu_b_s, sgu_w_out):
    b, s, d = x.shape
    assert b == 1 and s % (BLK * DSW_PATTERNS[-1][1]) == 0
    x2d = x.reshape(s, d)
    cos_t, sin_t = _rope_tables_fm(s)

    gains = jnp.stack([dsw_q_norm[0], dsw_k_norm[0], diff_q_norm[0], diff_k_norm[0]])
    gains = jnp.broadcast_to(gains[:, None, :, None], (4, N_GROUPS64, HEAD_DIM, 1))
    qa_views, ka_views, va_views, g, kd_tok, qd_t, vd_t = _att_in(x2d, att_norm[0], att_w_in[0], gains, cos_t, sin_t)
    bound = lambda gq, gk: 1.02 * LOG2_E * HEAD_DIM ** 0.5 * jnp.max(jnp.abs(gq)) * jnp.max(jnp.abs(gk))
    oa_acc, oa_l = _dsw_attention(qa_views, ka_views, va_views, bound(dsw_q_norm[0], dsw_k_norm[0]))
    lam_params = jnp.stack([diff_lam_q1[0], diff_lam_k1[0], diff_lam_q2[0], diff_lam_k2[0]])
    od = _diff_attention(qd_t, kd_tok, vd_t, lam_params, diff_subln[0].reshape(DIFF_V_DIM, 1),
                         bound(diff_q_norm[0], diff_k_norm[0]))
    x2d = _att_out_sgu(x2d, oa_acc, oa_l, od, g, att_w_out[0],
                       sgu_norm[0], sgu_w_in[0], sgu_ln_g[0], sgu_ln_b[0], sgu_w_s[0], sgu_b_s[0], sgu_w_out[0])
    return x2d.reshape(b, s, d)
```

```python
import functools
import math

import numpy as np
import jax
import jax.numpy as jnp
from jax import lax
from jax.experimental import pallas as pl
from jax.experimental.pallas import tpu as pltpu

F32 = jnp.float32
BF16 = jnp.bfloat16

LANES = 128
SUBLANES = 8
HEAD_DIM = 64
ROT_HALF = 8
ROPE_THETA = 500000.0
BLK = 128
N_GROUPS64 = 8
SPLIT = N_GROUPS64 * HEAD_DIM
DSW_PATTERNS = ((128, 1), (512, 4), (2048, 16))
DSW_DIL_RATIO = 4
DSW_STATE_DTYPE = jnp.bfloat16
DIFF_HEADS = 4
DIFF_V_DIM = 128
SGU_GROUPS = 8
SGU_CHUNK = 128
RMS_EPS = 1e-6
LN_EPS = 1e-5
NEG_INF = -1e30
LAM_INIT = 0.8 - 0.6 * math.exp(-0.3 * 0)
LOG2_E = math.log2(math.e)
DIFF_UNSHIFTED_SCORE_BOUND = 96.0
DSW_UNSHIFTED_SCORE_BOUND = 96.0

V7X_VMEM_LIMIT_BYTES = 56 * 1024 * 1024

ROW_TILE = 512
ATT_IN_PARTS = 2
DSW_BLOCKS_PER_STEP = 8
DIFF_TQ = 1024
DIFF_TK = 1024
DIFF_DIAG_BLOCK = 512


def _params(*sem):
    return pltpu.CompilerParams(dimension_semantics=sem, vmem_limit_bytes=V7X_VMEM_LIMIT_BYTES)


def _const_spec(shape):
    nd = len(shape)
    return pl.BlockSpec(shape, lambda *_: (0,) * nd)


def _norm_rope_fm(p, gain, cos, sin, scale):
    t = p.shape[1]
    p3 = p.reshape(N_GROUPS64, HEAD_DIM, t)
    ms = jnp.mean(p3 * p3, axis=1, keepdims=True)
    y = p3 * lax.rsqrt(ms + RMS_EPS) * gain
    x1 = y[:, 0:ROT_HALF, :]
    x2 = y[:, ROT_HALF:2 * ROT_HALF, :]
    out = jnp.concatenate([x1 * cos - x2 * sin, x2 * cos + x1 * sin, y[:, 2 * ROT_HALF:, :]], axis=1)
    if scale != 1.0:
        out = out * scale
    return out.reshape(SPLIT, t)


def _emit_pattern_views(tok, tok_sc, view_refs, part, n_parts):
    t = tok.shape[0]
    n_lane_tiles = SPLIT // LANES
    for c in range(n_lane_tiles):
        tok_sc[c] = tok[:, c * LANES:(c + 1) * LANES]
    for (_, dil), ref in zip(DSW_PATTERNS, view_refs):
        if dil == 1:
            ref[part * t:(part + 1) * t, :] = tok.astype(BF16)
            continue
        n = t // dil
        for r in range(dil):
            rows = jnp.concatenate([tok_sc[c, pl.ds(r, n, stride=dil), :] for c in range(n_lane_tiles)], axis=1)
            ref[0, part * n:(part + 1) * n, r * SPLIT:(r + 1) * SPLIT] = rows.astype(BF16)


def _att_in_kernel(x_ref, ng_ref, wtok_ref, wfm_ref, gains_ref, cos_ref, sin_ref, *refs):
    n_pat = len(DSW_PATTERNS)
    qa_views, ka_views, va_views = refs[0:n_pat], refs[n_pat:2 * n_pat], refs[2 * n_pat:3 * n_pat]
    g_ref, kd_ref, qd_ref, vd_ref, tok_sc = refs[3 * n_pat:]
    scales = (HEAD_DIM ** -0.5 * LOG2_E, 1.0, HEAD_DIM ** -0.5 * LOG2_E, 1.0)
    n_parts = ATT_IN_PARTS
    t = x_ref.shape[0] // n_parts
    for part in range(n_parts):
        rows = slice(part * t, (part + 1) * t)
        x = x_ref[rows, :]
        ms = jnp.mean(x * x, axis=-1, keepdims=True)
        h = (x * lax.rsqrt(ms + RMS_EPS) * ng_ref[...]).astype(BF16)
        tok = jnp.dot(h, wtok_ref[...], preferred_element_type=F32)
        _emit_pattern_views(tok[:, :SPLIT], tok_sc.at[part, 0], va_views, part, n_parts)
        g_ref[rows, :] = tok[:, SPLIT:]
        cos = cos_ref[:, rows][None]
        sin = sin_ref[:, rows][None]
        def feature_major(idx):
            return lax.dot_general(wfm_ref[idx * SPLIT:(idx + 1) * SPLIT, :], h, (((1,), (1,)), ((), ())),
                                   preferred_element_type=F32)

        def normed(idx):
            return _norm_rope_fm(feature_major(idx), gains_ref[idx], cos, sin, scales[idx])

        _emit_pattern_views(normed(0).T, tok_sc.at[part, 1], qa_views, part, n_parts)
        _emit_pattern_views(normed(1).T, tok_sc.at[part, 2], ka_views, part, n_parts)
        qd_ref[:, rows] = normed(2).astype(BF16)
        kd_ref[rows, :] = normed(3).T.astype(BF16)
        vd_ref[:, rows] = feature_major(4).astype(BF16)


def _att_in(x2d, norm_g, w_in, gains, cos_t, sin_t):
    s, d = x2d.shape
    tm = ROW_TILE
    w = w_in.astype(BF16)
    w_tok = jnp.concatenate([w[:, 2 * SPLIT:4 * SPLIT], w[:, 7 * SPLIT:8 * SPLIT]], axis=1)
    w_fm = jnp.concatenate([w[:, 0:2 * SPLIT], w[:, 4 * SPLIT:7 * SPLIT]], axis=1).T
    view_shapes, view_specs = [], []
    for _, dil in DSW_PATTERNS:
        if dil == 1:
            view_shapes.append(jax.ShapeDtypeStruct((s, SPLIT), BF16))
            view_specs.append(pl.BlockSpec((tm, SPLIT), lambda i: (i, 0)))
            continue
        rows = tm // dil
        assert tm % dil == 0 and BLK % rows == 0 and rows % (16 * ATT_IN_PARTS) == 0
        per_blk = BLK // rows
        view_shapes.append(jax.ShapeDtypeStruct((s // (BLK * dil), BLK, dil * SPLIT), BF16))
        view_specs.append(pl.BlockSpec((1, rows, dil * SPLIT),
                                       functools.partial(lambda i, per_blk: (i // per_blk, i % per_blk, 0),
                                                         per_blk=per_blk)))
    fm = jax.ShapeDtypeStruct((SPLIT, s), BF16)
    fm_spec = pl.BlockSpec((SPLIT, tm), lambda i: (0, i))
    outs = pl.pallas_call(
        _att_in_kernel,
        grid=(s // tm,),
        in_specs=[
            pl.BlockSpec((tm, d), lambda i: (i, 0)),
            _const_spec((1, d)),
            _const_spec(w_tok.shape),
            _const_spec(w_fm.shape),
            _const_spec(gains.shape),
            pl.BlockSpec((ROT_HALF, tm), lambda i: (0, i)),
            pl.BlockSpec((ROT_HALF, tm), lambda i: (0, i)),
        ],
        out_specs=view_specs * 3 + [
            pl.BlockSpec((tm, 2 * SPLIT), lambda i: (i, 0)),
            pl.BlockSpec((tm, SPLIT), lambda i: (i, 0)),
            fm_spec, fm_spec,
        ],
        out_shape=view_shapes * 3 + [
            jax.ShapeDtypeStruct((s, 2 * SPLIT), F32),
            jax.ShapeDtypeStruct((s, SPLIT), BF16),
            fm, fm,
        ],
        scratch_shapes=[pltpu.VMEM((ATT_IN_PARTS, 3, SPLIT // LANES, tm // ATT_IN_PARTS, LANES), F32)],
        compiler_params=_params("parallel"),
        name="att_in_proj",
    )(x2d, norm_g.reshape(1, d), w_tok, w_fm, gains, cos_t, sin_t)
    n_pat = len(DSW_PATTERNS)
    return (outs[0:n_pat], outs[n_pat:2 * n_pat], outs[2 * n_pat:3 * n_pat]) + tuple(outs[3 * n_pat:])


DSW_L_LANES_PER_HEAD = BLK // N_GROUPS64


def _dsw_online_pair(q_pair, kk, vv, mask, acc_in, l_tile_in, hp, last):
    low_half = lax.broadcasted_iota(jnp.int32, (BLK, 2 * HEAD_DIM), 1) < HEAD_DIM
    halves, state_cols = [], []
    for e in range(2):
        head = 2 * hp + e
        sel = low_half if e == 0 else jnp.logical_not(low_half)
        qm = jnp.where(sel, q_pair, jnp.zeros_like(q_pair))
        sc = lax.dot_general(qm, kk, (((1,), (1,)), ((), ())), preferred_element_type=F32)
        sc = jnp.where(mask, sc, NEG_INF)
        m_new = jnp.max(sc, axis=-1, keepdims=True)
        if acc_in is not None:
            m_old = l_tile_in[:, head * DSW_L_LANES_PER_HEAD:head * DSW_L_LANES_PER_HEAD + 1]
            m_new = jnp.maximum(m_old, m_new)
            alpha = jnp.exp2(m_old - m_new)
        p = jnp.exp2(sc - m_new)
        l = jnp.sum(p, axis=-1, keepdims=True)
        pv = jnp.dot(p.astype(BF16), vv, preferred_element_type=F32)
        if acc_in is not None:
            l = l + alpha
            pv = pv + alpha * acc_in
        halves.append(pv / l)
        state_cols.append(jnp.ones_like(l) if last else m_new + jnp.log2(l))
    return jnp.where(low_half, halves[0], halves[1]), state_cols


def _dsw_hop_kernel(bound_ref, *refs, has_state, last, blocks_per_step):
    refs = list(refs)
    q_ref, kprev_ref, kcur_ref, vprev_ref, vcur_ref, ehot_ref = refs[:6]
    pos = 6
    if has_state:
        accin_ref, lin_ref = refs[pos:pos + 2]
        pos += 2
    acc_ref, l_ref = refs[pos:pos + 2]
    perm_sc = None if last else refs[pos + 2]
    g_blocks = blocks_per_step
    step = pl.program_id(1)

    def window(prev_ref, cur_ref, g, cols):
        before = prev_ref[0, :, cols] if g == 0 else cur_ref[g - 1, :, cols]
        return jnp.concatenate([before, cur_ref[g, :, cols]], axis=0)

    def emit(g, accs, l_tile):
        n_pairs = N_GROUPS64 // 2
        if last:
            for hp in range(n_pairs):
                acc_ref[g, :, hp * 2 * HEAD_DIM:(hp + 1) * 2 * HEAD_DIM] = accs[hp]
            l_ref[g] = l_tile
            return
        slot = perm_sc.at[g % 2]
        for hp in range(n_pairs):
            slot[hp] = accs[hp]
        slot[n_pairs] = l_tile
        ratio = DSW_DIL_RATIO
        rows = BLK // ratio
        out_blk = g // ratio
        row_dst = slice((g % ratio) * rows, (g % ratio + 1) * rows)
        for m in range(ratio):
            src = pl.ds(m, rows, stride=ratio)
            for hp in range(n_pairs):
                col0 = m * SPLIT + hp * 2 * HEAD_DIM
                acc_ref[out_blk, row_dst, col0:col0 + 2 * HEAD_DIM] = slot[hp, src, :].astype(acc_ref.dtype)
            l_ref[out_blk, row_dst, m * BLK:(m + 1) * BLK] = slot[n_pairs, src, :]

    unshifted = bound_ref[0] <= DSW_UNSHIFTED_SCORE_BOUND

    @pl.when(jnp.logical_not(unshifted))
    def _():
        qi = lax.broadcasted_iota(jnp.int32, (BLK, 2 * BLK), 0) + BLK
        kj = lax.broadcasted_iota(jnp.int32, (BLK, 2 * BLK), 1)
        dist = qi - kj
        band = (dist >= 0) & (dist <= BLK)
        lane_head = lax.broadcasted_iota(jnp.int32, (BLK, BLK), 1) // DSW_L_LANES_PER_HEAD
        for g in range(g_blocks):
            mask = band & (kj >= jnp.where(step == 0, BLK, 0)) if g == 0 else band
            l_in = lin_ref[g] if has_state else None
            accs = []
            l_tile = jnp.zeros((BLK, BLK), F32)
            for hp in range(N_GROUPS64 // 2):
                cols = slice(hp * 2 * HEAD_DIM, (hp + 1) * 2 * HEAD_DIM)
                acc, state_cols = _dsw_online_pair(
                    q_ref[g, :, cols], window(kprev_ref, kcur_ref, g, cols), window(vprev_ref, vcur_ref, g, cols),
                    mask, accin_ref[g, :, cols].astype(F32) if has_state else None, l_in, hp, last)
                accs.append(acc)
                for e in range(2):
                    l_tile = jnp.where(lane_head == 2 * hp + e, state_cols[e], l_tile)
            emit(g, accs, l_tile)

    @pl.when(unshifted)
    def _():
        _dsw_unshifted_blocks(q_ref, kprev_ref, kcur_ref, vprev_ref, vcur_ref, ehot_ref,
                              accin_ref if has_state else None, lin_ref if has_state else None,
                              window, emit, step, g_blocks)


def _dsw_unshifted_blocks(q_ref, kprev_ref, kcur_ref, vprev_ref, vcur_ref, ehot_ref, accin_ref, lin_ref,
                          window, emit, step, g_blocks):
    has_state = accin_ref is not None

    qi = (lax.broadcasted_iota(jnp.int32, (2 * BLK, 2 * BLK), 0) & (BLK - 1)) + BLK
    kj = lax.broadcasted_iota(jnp.int32, (2 * BLK, 2 * BLK), 1)
    dist = qi - kj
    band = (dist >= 0) & (dist <= BLK)
    q_low = lax.broadcasted_iota(jnp.int32, (BLK, 2 * HEAD_DIM), 1) < HEAD_DIM
    v_low = lax.broadcasted_iota(jnp.int32, (2 * BLK, 2 * HEAD_DIM), 1) < HEAD_DIM

    for g in range(g_blocks):
        if g == 0:
            mask = band & (kj >= jnp.where(step == 0, BLK, 0))
        else:
            mask = band
        l_tile = lin_ref[g] if has_state else jnp.zeros((BLK, BLK), F32)
        accs = []
        for hp in range(N_GROUPS64 // 2):
            cols = slice(hp * 2 * HEAD_DIM, (hp + 1) * 2 * HEAD_DIM)
            q_pair = q_ref[g, :, cols]
            kk = window(kprev_ref, kcur_ref, g, cols)
            vv = window(vprev_ref, vcur_ref, g, cols)
            qz = jnp.zeros_like(q_pair)
            vz = jnp.zeros_like(vv)
            q2 = jnp.concatenate([jnp.where(q_low, q_pair, qz), jnp.where(q_low, qz, q_pair)], axis=0)
            sc = lax.dot_general(q2, kk, (((1,), (1,)), ((), ())), preferred_element_type=F32)
            p2 = jnp.where(mask, jnp.exp2(sc), 0.0).astype(BF16)
            p_cat = jnp.concatenate([p2[:BLK], p2[BLK:]], axis=1)
            v_split = jnp.concatenate([jnp.where(v_low, vv, vz), jnp.where(v_low, vz, vv)], axis=0)
            rhs = jnp.concatenate([v_split, ehot_ref[hp]], axis=1)
            out2 = jnp.dot(p_cat, rhs, preferred_element_type=F32)
            acc = out2[:, :2 * HEAD_DIM]
            if has_state:
                acc = acc + accin_ref[g, :, cols].astype(F32)
            l_tile = l_tile + out2[:, 2 * HEAD_DIM:]
            accs.append(acc)
        emit(g, accs, l_tile)


def _dsw_residue_of(hop):
    if hop == 0:
        return lambda g: g
    prev = _dsw_residue_of(hop - 1)
    prev_dil = DSW_PATTERNS[hop - 1][1]
    return lambda g: prev_dil * (g % DSW_DIL_RATIO) + prev(g // DSW_DIL_RATIO)


def _dsw_pattern(q, k, v, hop, state, ehot, bound):
    dil = DSW_PATTERNS[hop][1]
    last = hop == len(DSW_PATTERNS) - 1
    s = q.size // SPLIT
    nb = s // (BLK * dil)
    g_blocks = min(DSW_BLOCKS_PER_STEP, nb)
    ratio = DSW_DIL_RATIO
    residue = _dsw_residue_of(hop)
    view = lambda a: a.reshape(nb, BLK, dil * SPLIT)
    qkv_spec = pl.BlockSpec((g_blocks, BLK, SPLIT), lambda r, j, b: (j, 0, residue(r)))
    prev_spec = pl.BlockSpec((1, BLK, SPLIT), lambda r, j, b: (jnp.maximum(j * g_blocks - 1, 0), 0, residue(r)))
    state_spec = lambda width: pl.BlockSpec((g_blocks, BLK, width), lambda r, j, b: (j, 0, r))
    operands = [view(q), view(k), view(k), view(v), view(v), ehot]
    in_specs = [qkv_spec, prev_spec, qkv_spec, prev_spec, qkv_spec,
                pl.BlockSpec(ehot.shape, lambda r, j, b: (0, 0, 0))]
    if state is not None:
        operands += list(state)
        in_specs += [state_spec(SPLIT), state_spec(BLK)]
    scratch = []
    if last:
        own_spec = lambda width: pl.BlockSpec((g_blocks, BLK, width), lambda r, j, b: (j, 0, residue(r)))
        out_specs = [own_spec(SPLIT), own_spec(BLK)]
        out_shape = [jax.ShapeDtypeStruct((nb, BLK, dil * SPLIT), F32),
                     jax.ShapeDtypeStruct((nb, BLK, dil * BLK), F32)]
    else:
        assert DSW_PATTERNS[hop + 1][1] == dil * ratio and g_blocks % ratio == 0
        next_spec = lambda width: pl.BlockSpec((g_blocks // ratio, BLK, ratio * width), lambda r, j, b: (j, 0, r))
        out_specs = [next_spec(SPLIT), next_spec(BLK)]
        out_shape = [jax.ShapeDtypeStruct((nb // ratio, BLK, dil * ratio * SPLIT), DSW_STATE_DTYPE),
                     jax.ShapeDtypeStruct((nb // ratio, BLK, dil * ratio * BLK), F32)]
        scratch.append(pltpu.VMEM((2, N_GROUPS64 // 2 + 1, BLK, 2 * HEAD_DIM), F32))
    out = pl.pallas_call(
        functools.partial(_dsw_hop_kernel, has_state=state is not None, last=last, blocks_per_step=g_blocks),
        grid_spec=pltpu.PrefetchScalarGridSpec(
            num_scalar_prefetch=1, grid=(dil, nb // g_blocks), in_specs=in_specs, out_specs=out_specs,
            scratch_shapes=scratch),
        out_shape=out_shape,
        compiler_params=_params("parallel", "arbitrary"),
        name=f"dsw_attention_dil{dil}",
    )(bound, *operands)
    return tuple(out)


def _dsw_constants():
    ehot = np.zeros((N_GROUPS64 // 2, 4 * BLK, BLK), np.float32)
    xexp = np.zeros((BLK, SPLIT), np.float32)
    for head in range(N_GROUPS64):
        hp, e = divmod(head, 2)
        lanes = slice(head * DSW_L_LANES_PER_HEAD, (head + 1) * DSW_L_LANES_PER_HEAD)
        ehot[hp, e * 2 * BLK:(e + 1) * 2 * BLK, lanes] = 1.0
        xexp[head * DSW_L_LANES_PER_HEAD, head * HEAD_DIM:(head + 1) * HEAD_DIM] = 1.0
    return jnp.asarray(ehot, BF16), jnp.asarray(xexp, BF16)


def _dsw_attention(q_views, k_views, v_views, score_bound):
    ehot, _ = _dsw_constants()
    bound = score_bound.reshape(1).astype(F32)
    state = None
    for hop in range(len(DSW_PATTERNS)):
        state = _dsw_pattern(q_views[hop], k_views[hop], v_views[hop], hop, state, ehot, bound)
    return state


def _diff_finalize(o0, o1, lam_ref, sg_ref, o_ref):
    lam_p = lam_ref[...]
    lam = (jnp.exp(jnp.sum(lam_p[0:1] * lam_p[1:2], axis=-1, keepdims=True))
           - jnp.exp(jnp.sum(lam_p[2:3] * lam_p[3:4], axis=-1, keepdims=True)) + LAM_INIT)
    od = o0 - lam * o1
    ms = jnp.mean(od * od, axis=0, keepdims=True)
    y = od * lax.rsqrt(ms + RMS_EPS) * sg_ref[...] * (1.0 - LAM_INIT)
    o_ref[...] = y.T


def _component_queries(qt):
    zeros = jnp.zeros((HEAD_DIM, qt.shape[1]), qt.dtype)
    return (jnp.concatenate([qt[:HEAD_DIM], zeros], axis=0), jnp.concatenate([zeros, qt[HEAD_DIM:]], axis=0))


def _causal_keep(i, k0, tq, tk):
    kpos = k0 + lax.broadcasted_iota(jnp.int32, (tk, tq), 0)
    qpos = i * tq + lax.broadcasted_iota(jnp.int32, (tk, tq), 1)
    return kpos <= qpos


def _diff_unshifted_kernel(qt_ref, k_ref, vt_ref, lam_ref, sg_ref, o_ref, l_sc, acc_sc, *, tq, tk):
    i = pl.program_id(1)
    qc = _component_queries(qt_ref[...])
    l_sc[...] = jnp.zeros(l_sc.shape, F32)
    acc_sc[...] = jnp.zeros(acc_sc.shape, F32)

    def full_chunk(j):
        k0 = pl.multiple_of(j * tk, tk)
        vt = vt_ref[:, pl.ds(k0, tk)]
        for c in range(2):
            st = jnp.dot(k_ref[pl.ds(k0, tk), :], qc[c], preferred_element_type=F32)
            p = jnp.exp2(st)
            l_sc[c] += jnp.sum(p.reshape(tk // SUBLANES, SUBLANES, tq), axis=0)
            acc_sc[c] += jnp.dot(vt, p.astype(BF16), preferred_element_type=F32)

    def chunk_quad(jq, carry):
        for u in range(4):
            full_chunk(4 * jq + u)
        return carry

    n_quads = lax.shift_right_logical(i, 2)
    lax.fori_loop(0, n_quads, chunk_quad, 0)

    @pl.when((i & 2) == 2)
    def _():
        full_chunk(4 * n_quads)
        full_chunk(4 * n_quads + 1)

    @pl.when((i & 1) == 1)
    def _():
        full_chunk(i - 1)

    sb = DIFF_DIAG_BLOCK
    k0 = pl.multiple_of(i * tk, tk)
    tri = lax.broadcasted_iota(jnp.int32, (sb, sb), 0) <= lax.broadcasted_iota(jnp.int32, (sb, sb), 1)
    for qb in range(tq // sb):
        nk = (qb + 1) * sb
        lanes = slice(qb * sb, (qb + 1) * sb)
        vt = vt_ref[:, pl.ds(k0, nk)]
        for c in range(2):
            st = jnp.dot(k_ref[pl.ds(k0, nk), :], qc[c][:, lanes], preferred_element_type=F32)
            p = jnp.exp2(st)
            p_tri = jnp.where(tri, p[qb * sb:, :], 0.0)
            p = jnp.concatenate([p[:qb * sb, :], p_tri], axis=0) if qb else p_tri
            l_sc[c, :, lanes] += jnp.sum(p.reshape(nk // SUBLANES, SUBLANES, sb), axis=0)
            acc_sc[c, :, lanes] += jnp.dot(vt, p.astype(BF16), preferred_element_type=F32)

    l0 = jnp.sum(l_sc[0], axis=0, keepdims=True)
    l1 = jnp.sum(l_sc[1], axis=0, keepdims=True)
    _diff_finalize(acc_sc[0] / l0, acc_sc[1] / l1, lam_ref, sg_ref, o_ref)


def _diff_kernel(qt_ref, k_ref, vt_ref, lam_ref, sg_ref, o_ref, m_sc, l_sc, acc_sc, *, tq, tk):
    i = pl.program_id(1)
    qc = _component_queries(qt_ref[...])
    m_sc[...] = jnp.full(m_sc.shape, NEG_INF, F32)
    l_sc[...] = jnp.zeros(l_sc.shape, F32)
    acc_sc[...] = jnp.zeros(acc_sc.shape, F32)

    def chunk(j, masked):
        k0 = pl.multiple_of(j * tk, tk)
        vt = vt_ref[:, pl.ds(k0, tk)]
        for c in range(2):
            st = jnp.dot(k_ref[pl.ds(k0, tk), :], qc[c], preferred_element_type=F32)
            if masked:
                st = jnp.where(_causal_keep(i, k0, tq, tk), st, NEG_INF)
            m_old = m_sc[c]
            m_new = jnp.maximum(m_old, jnp.max(st, axis=0, keepdims=True))
            alpha = jnp.exp2(m_old - m_new)
            p = jnp.exp2(st - m_new)
            l_sc[c] = alpha * l_sc[c] + jnp.sum(p, axis=0, keepdims=True)
            acc_sc[c] = alpha * acc_sc[c] + jnp.dot(vt, p.astype(BF16), preferred_element_type=F32)
            m_sc[c] = m_new

    n_full = (i * tq) // tk

    def full_chunk(j, carry):
        chunk(j, False)
        return carry

    lax.fori_loop(0, n_full, full_chunk, 0)
    for jj in range(tq // tk):
        chunk(n_full + jj, True)

    _diff_finalize(acc_sc[0] / l_sc[0], acc_sc[1] / l_sc[1], lam_ref, sg_ref, o_ref)


def _diff_dispatch_kernel(bound_ref, qt_ref, k_ref, vt_ref, lam_ref, sg_ref, o_ref, m_sc, l1_sc, l8_sc, acc_sc, *,
                          tq, tk):
    unshifted = bound_ref[0] <= DIFF_UNSHIFTED_SCORE_BOUND

    @pl.when(unshifted)
    def _():
        _diff_unshifted_kernel(qt_ref, k_ref, vt_ref, lam_ref, sg_ref, o_ref, l8_sc, acc_sc, tq=tq, tk=tk)

    @pl.when(jnp.logical_not(unshifted))
    def _():
        _diff_kernel(qt_ref, k_ref, vt_ref, lam_ref, sg_ref, o_ref, m_sc, l1_sc, acc_sc, tq=tq, tk=tk)


def _diff_attention(qd_t, kd_tok, vd_t, lam_params, subln_col, score_bound):
    s = qd_t.shape[1]
    tq, tk = DIFF_TQ, DIFF_TK
    assert tq == tk and s % tq == 0 and tq % DIFF_DIAG_BLOCK == 0
    grid_spec = pltpu.PrefetchScalarGridSpec(
        num_scalar_prefetch=1,
        grid=(DIFF_HEADS, s // tq),
        in_specs=[
            pl.BlockSpec((DIFF_V_DIM, tq), lambda h, i, bound: (h, i)),
            pl.BlockSpec((s, 2 * HEAD_DIM), lambda h, i, bound: (0, h)),
            pl.BlockSpec((DIFF_V_DIM, s), lambda h, i, bound: (h, 0)),
            pl.BlockSpec(lam_params.shape, lambda h, i, bound: (0, 0)),
            pl.BlockSpec(subln_col.shape, lambda h, i, bound: (0, 0)),
        ],
        out_specs=pl.BlockSpec((tq, DIFF_V_DIM), lambda h, i, bound: (i, h)),
        scratch_shapes=[pltpu.VMEM((2, 1, tq), F32), pltpu.VMEM((2, 1, tq), F32), pltpu.VMEM((2, SUBLANES, tq), F32),
                        pltpu.VMEM((2, DIFF_V_DIM, tq), F32)],
    )
    return pl.pallas_call(
        functools.partial(_diff_dispatch_kernel, tq=tq, tk=tk),
        grid_spec=grid_spec,
        out_shape=jax.ShapeDtypeStruct((s, DIFF_HEADS * DIFF_V_DIM), F32),
        compiler_params=_params("parallel", "arbitrary"),
        name="diff_attention",
    )(score_bound.reshape(1).astype(F32), qd_t, kd_tok, vd_t, lam_params, subln_col)


def _att_out_compute(x_ref, acc_ref, l_ref, xexp_ref, od_ref, g_ref, w_ref, tok_sc, dil):
    n_lane_tiles = SPLIT // LANES
    rows = tok_sc.shape[1] // dil
    for r in range(dil):
        dst = pl.ds(r, rows, stride=dil)
        for c in range(n_lane_tiles):
            col0 = r * SPLIT + c * LANES
            tok_sc[c, dst, :] = acc_ref[0, :, col0:col0 + LANES]
        tok_sc[n_lane_tiles, dst, :] = l_ref[0, :, r * LANES:(r + 1) * LANES]
    acc = jnp.concatenate([tok_sc[c] for c in range(n_lane_tiles)], axis=1)
    r = 1.0 / tok_sc[n_lane_tiles]
    r_hi = r.astype(BF16)
    r_lo = (r - r_hi.astype(F32)).astype(BF16)
    r_exp = (jnp.dot(r_hi, xexp_ref[...], preferred_element_type=F32)
             + jnp.dot(r_lo, xexp_ref[...], preferred_element_type=F32))
    g = g_ref[...]
    y = jnp.concatenate([acc * r_exp, od_ref[...]], axis=-1) * (g * jax.nn.sigmoid(g))
    return x_ref[...] + jnp.dot(y.astype(BF16), w_ref[...], preferred_element_type=F32)


def _gelu(x):
    return 0.5 * x * (1.0 + lax.erf(x * (2.0 ** -0.5)))


def _sgu_compute(x, ng_ref, win_ref, lng_ref, lnb_ref, ws_ref, bs_ref, wout_ref, width):
    tm = x.shape[0]
    ms = jnp.mean(x * x, axis=-1, keepdims=True)
    h = (x * lax.rsqrt(ms + RMS_EPS) * ng_ref[...]).astype(BF16)
    v_raw = jnp.dot(h, win_ref[:, width:2 * width], preferred_element_type=F32)
    u_raw = jnp.dot(h, win_ref[:, 0:width], preferred_element_type=F32)
    v = _gelu(v_raw)
    mu = jnp.mean(v, axis=-1, keepdims=True)
    vc = v - mu
    vn = vc * lax.rsqrt(jnp.mean(vc * vc, axis=-1, keepdims=True) + LN_EPS)
    vn = (vn * lng_ref[...] + lnb_ref[...]).astype(BF16)
    gw = width // SGU_GROUPS
    row = lax.broadcasted_iota(jnp.int32, (SGU_CHUNK, SGU_CHUNK), 0)
    col = lax.broadcasted_iota(jnp.int32, (SGU_CHUNK, SGU_CHUNK), 1)
    causal = col <= row
    sp_groups = []
    for grp in range(SGU_GROUPS):
        ws = jnp.where(causal, ws_ref[grp], 0.0).astype(BF16)
        bias = bs_ref[grp]
        chunks = []
        for c in range(tm // SGU_CHUNK):
            vg = vn[c * SGU_CHUNK:(c + 1) * SGU_CHUNK, grp * gw:(grp + 1) * gw]
            chunks.append(jnp.dot(ws, vg, preferred_element_type=F32) + bias)
        sp_groups.append(jnp.concatenate(chunks, axis=0))
    sp = jnp.concatenate(sp_groups, axis=1)
    g = jnp.dot(h, win_ref[:, 2 * width:3 * width], preferred_element_type=F32)
    u = _gelu(u_raw)
    y = u * sp * (g * jax.nn.sigmoid(g))
    return x + jnp.dot(y.astype(BF16), wout_ref[...], preferred_element_type=F32)


def _att_out_sgu_kernel(x_ref, acc_ref, l_ref, xexp_ref, od_ref, g_ref, watt_ref,
                        ng_ref, win_ref, lng_ref, lnb_ref, ws_ref, bs_ref, wout_ref, o_ref, tok_sc, *, dil, width):
    x1 = _att_out_compute(x_ref, acc_ref, l_ref, xexp_ref, od_ref, g_ref, watt_ref, tok_sc, dil)
    o_ref[...] = _sgu_compute(x1, ng_ref, win_ref, lng_ref, lnb_ref, ws_ref, bs_ref, wout_ref, width)


def _att_out_sgu(x2d, acc_view, l_view, od, g, w_att_out, norm_g, w_in, ln_g, ln_b, w_s, b_s, w_out):
    s, d = x2d.shape
    width = w_out.shape[0]
    tm = ROW_TILE
    dil = DSW_PATTERNS[-1][1]
    rows = tm // dil
    per_blk = BLK // rows
    assert acc_view.shape == (s // (BLK * dil), BLK, dil * SPLIT) and tm % dil == 0 and BLK % rows == 0
    assert l_view.shape == (s // (BLK * dil), BLK, dil * LANES)
    assert tm % SGU_CHUNK == 0 and rows % 8 == 0
    _, xexp = _dsw_constants()
    row = lambda w: pl.BlockSpec((tm, w), lambda i: (i, 0))
    view_spec = lambda w: pl.BlockSpec((1, rows, dil * w), lambda i: (i // per_blk, i % per_blk, 0))
    single = lambda shape: pl.BlockSpec(shape, lambda *_: (0,) * len(shape), pipeline_mode=pl.Buffered(1))
    return pl.pallas_call(
        functools.partial(_att_out_sgu_kernel, dil=dil, width=width),
        grid=(s // tm,),
        in_specs=[row(d), view_spec(SPLIT), view_spec(LANES), _const_spec(xexp.shape), row(SPLIT), row(2 * SPLIT),
                  single(w_att_out.shape),
                  _const_spec((1, d)), single(w_in.shape), _const_spec((1, width)), _const_spec((1, width)),
                  _const_spec(w_s.shape), _const_spec((SGU_GROUPS, SGU_CHUNK, 1)), single(w_out.shape)],
        out_specs=row(d),
        out_shape=jax.ShapeDtypeStruct((s, d), F32),
        scratch_shapes=[pltpu.VMEM((SPLIT // LANES + 1, tm, LANES), F32)],
        compiler_params=_params("parallel"),
        name="att_out_sgu",
    )(x2d, acc_view, l_view, xexp, od, g, w_att_out.astype(BF16),
      norm_g.reshape(1, d), w_in.astype(BF16), ln_g.reshape(1, width), ln_b.reshape(1, width),
      w_s, b_s.reshape(SGU_GROUPS, SGU_CHUNK, 1), w_out.astype(BF16))


def _rope_tables_fm(seq_len):
    pos = jnp.arange(seq_len, dtype=F32)
    inv = 1.0 / (ROPE_THETA ** (jnp.arange(0, 2 * ROT_HALF, 2, dtype=F32) / (2 * ROT_HALF)))
    ang = inv[:, None] * pos[None, :]
    return jnp.cos(ang), jnp.sin(ang)


def kernel(x, att_norm, att_w_in, dsw_q_norm, dsw_k_norm, diff_q_norm, diff_k_norm, diff_lam_q1, diff_lam_k1,
           diff_lam_q2, diff_lam_k2, diff_subln, att_w_out, sgu_norm, sgu_w_in, sgu_ln_g, sgu_ln_b, sgu_w_s,
           sgu_b_s, sgu_w_out):
    b, s, d = x.shape
    assert b == 1 and s % (BLK * DSW_PATTERNS[-1][1]) == 0
    x2d = x.reshape(s, d)
    cos_t, sin_t = _rope_tables_fm(s)

    gains = jnp.stack([dsw_q_norm[0], dsw_k_norm[0], diff_q_norm[0], diff_k_norm[0]])
    gains = jnp.broadcast_to(gains[:, None, :, None], (4, N_GROUPS64, HEAD_DIM, 1))
    qa_views, ka_views, va_views, g, kd_tok, qd_t, vd_t = _att_in(x2d, att_norm[0], att_w_in[0], gains, cos_t, sin_t)
    bound = lambda gq, gk: 1.02 * LOG2_E * HEAD_DIM ** 0.5 * jnp.max(jnp.abs(gq)) * jnp.max(jnp.abs(gk))
    oa_acc, oa_l = _dsw_attention(qa_views, ka_views, va_views, bound(dsw_q_norm[0], dsw_k_norm[0]))
    lam_params = jnp.stack([diff_lam_q1[0], diff_lam_k1[0], diff_lam_q2[0], diff_lam_k2[0]])
    od = _diff_attention(qd_t, kd_tok, vd_t, lam_params, diff_subln[0].reshape(DIFF_V_DIM, 1),
                         bound(diff_q_norm[0], diff_k_norm[0]))
    x2d = _att_out_sgu(x2d, oa_acc, oa_l, od, g, att_w_out[0],
                       sgu_norm[0], sgu_w_in[0], sgu_ln_g[0], sgu_ln_b[0], sgu_w_s[0], sgu_b_s[0], sgu_w_out[0])
    return x2d.reshape(b, s, d)
```

```python
import functools
import math

import numpy as np
import jax
import jax.numpy as jnp
from jax import lax
from jax.experimental import pallas as pl
from jax.experimental.pallas import tpu as pltpu

F32 = jnp.float32
BF16 = jnp.bfloat16

LANES = 128
SUBLANES = 8
HEAD_DIM = 64
ROT_HALF = 8
ROPE_THETA = 500000.0
BLK = 128
N_GROUPS64 = 8
SPLIT = N_GROUPS64 * HEAD_DIM
DSW_PATTERNS = ((128, 1), (512, 4), (2048, 16))
DSW_DIL_RATIO = 4
DIFF_HEADS = 4
DIFF_V_DIM = 128
SGU_GROUPS = 8
SGU_CHUNK = 128
RMS_EPS = 1e-6
LN_EPS = 1e-5
NEG_INF = -1e30
LAM_INIT = 0.8 - 0.6 * math.exp(-0.3 * 0)
LOG2_E = math.log2(math.e)
DIFF_UNSHIFTED_SCORE_BOUND = 96.0
DSW_UNSHIFTED_SCORE_BOUND = 96.0

V7X_VMEM_LIMIT_BYTES = 56 * 1024 * 1024

ROW_TILE = 512
ATT_IN_PARTS = 2
DSW_BLOCKS_PER_STEP = 8
DIFF_TQ = 1024
DIFF_TK = 1024
DIFF_DIAG_BLOCK = 512


def _params(*sem):
    return pltpu.CompilerParams(dimension_semantics=sem, vmem_limit_bytes=V7X_VMEM_LIMIT_BYTES)


def _const_spec(shape):
    nd = len(shape)
    return pl.BlockSpec(shape, lambda *_: (0,) * nd)


def _norm_rope_fm(p, gain, cos, sin, scale):
    t = p.shape[1]
    p3 = p.reshape(N_GROUPS64, HEAD_DIM, t)
    ms = jnp.mean(p3 * p3, axis=1, keepdims=True)
    y = p3 * lax.rsqrt(ms + RMS_EPS) * gain
    x1 = y[:, 0:ROT_HALF, :]
    x2 = y[:, ROT_HALF:2 * ROT_HALF, :]
    out = jnp.concatenate([x1 * cos - x2 * sin, x2 * cos + x1 * sin, y[:, 2 * ROT_HALF:, :]], axis=1)
    if scale != 1.0:
        out = out * scale
    return out.reshape(SPLIT, t)


def _emit_pattern_views(tok, tok_sc, view_refs, part, n_parts):
    t = tok.shape[0]
    n_lane_tiles = SPLIT // LANES
    for c in range(n_lane_tiles):
        tok_sc[c] = tok[:, c * LANES:(c + 1) * LANES]
    for (_, dil), ref in zip(DSW_PATTERNS, view_refs):
        if dil == 1:
            ref[part * t:(part + 1) * t, :] = tok.astype(BF16)
            continue
        n = t // dil
        for r in range(dil):
            rows = jnp.concatenate([tok_sc[c, pl.ds(r, n, stride=dil), :] for c in range(n_lane_tiles)], axis=1)
            ref[0, part * n:(part + 1) * n, r * SPLIT:(r + 1) * SPLIT] = rows.astype(BF16)


def _att_in_kernel(x_ref, ng_ref, wtok_ref, wfm_ref, gains_ref, cos_ref, sin_ref, *refs, n_cast):
    n_pat = len(DSW_PATTERNS)
    cast_in, refs = refs[:n_cast], refs[n_cast:]
    qa_views, ka_views, va_views = refs[0:n_pat], refs[n_pat:2 * n_pat], refs[2 * n_pat:3 * n_pat]
    g_ref, kd_ref, qd_ref, vd_ref = refs[3 * n_pat:3 * n_pat + 4]
    cast_out = refs[3 * n_pat + 4:3 * n_pat + 4 + n_cast]
    tok_sc = refs[-1]
    for src, dst in zip(cast_in, cast_out):
        dst[...] = src[...].astype(BF16)
    scales = (HEAD_DIM ** -0.5 * LOG2_E, 1.0, HEAD_DIM ** -0.5 * LOG2_E, 1.0)
    n_parts = ATT_IN_PARTS
    t = x_ref.shape[0] // n_parts
    for part in range(n_parts):
        rows = slice(part * t, (part + 1) * t)
        x = x_ref[rows, :]
        ms = jnp.mean(x * x, axis=-1, keepdims=True)
        h = (x * lax.rsqrt(ms + RMS_EPS) * ng_ref[...]).astype(BF16)
        tok = jnp.dot(h, wtok_ref[...], preferred_element_type=F32)
        _emit_pattern_views(tok[:, :SPLIT], tok_sc.at[part, 0], va_views, part, n_parts)
        g_ref[rows, :] = tok[:, SPLIT:]
        cos = cos_ref[:, rows][None]
        sin = sin_ref[:, rows][None]
        def feature_major(idx):
            return lax.dot_general(wfm_ref[idx * SPLIT:(idx + 1) * SPLIT, :], h, (((1,), (1,)), ((), ())),
                                   preferred_element_type=F32)

        def normed(idx):
            return _norm_rope_fm(feature_major(idx), gains_ref[idx], cos, sin, scales[idx])

        _emit_pattern_views(normed(0).T, tok_sc.at[part, 1], qa_views, part, n_parts)
        _emit_pattern_views(normed(1).T, tok_sc.at[part, 2], ka_views, part, n_parts)
        qd_ref[:, rows] = normed(2).astype(BF16)
        kd_ref[rows, :] = normed(3).T.astype(BF16)
        vd_ref[:, rows] = feature_major(4).astype(BF16)


def _att_in(x2d, norm_g, w_in, gains, cos_t, sin_t, later_weights):
    s, d = x2d.shape
    tm = ROW_TILE
    n_steps = s // tm
    cast_specs = []
    for w_later in later_weights:
        rows = w_later.shape[0] // n_steps
        assert w_later.shape[0] % n_steps == 0 and rows % 16 == 0
        cast_specs.append(pl.BlockSpec((rows, w_later.shape[1]), lambda i: (i, 0)))
    w = w_in.astype(BF16)
    w_tok = jnp.concatenate([w[:, 2 * SPLIT:4 * SPLIT], w[:, 7 * SPLIT:8 * SPLIT]], axis=1)
    w_fm = jnp.concatenate([w[:, 0:2 * SPLIT], w[:, 4 * SPLIT:7 * SPLIT]], axis=1).T
    view_shapes, view_specs = [], []
    for _, dil in DSW_PATTERNS:
        if dil == 1:
            view_shapes.append(jax.ShapeDtypeStruct((s, SPLIT), BF16))
            view_specs.append(pl.BlockSpec((tm, SPLIT), lambda i: (i, 0)))
            continue
        rows = tm // dil
        assert tm % dil == 0 and BLK % rows == 0 and rows % (16 * ATT_IN_PARTS) == 0
        per_blk = BLK // rows
        view_shapes.append(jax.ShapeDtypeStruct((s // (BLK * dil), BLK, dil * SPLIT), BF16))
        view_specs.append(pl.BlockSpec((1, rows, dil * SPLIT),
                                       functools.partial(lambda i, per_blk: (i // per_blk, i % per_blk, 0),
                                                         per_blk=per_blk)))
    fm = jax.ShapeDtypeStruct((SPLIT, s), BF16)
    fm_spec = pl.BlockSpec((SPLIT, tm), lambda i: (0, i))
    outs = pl.pallas_call(
        functools.partial(_att_in_kernel, n_cast=len(later_weights)),
        grid=(n_steps,),
        in_specs=[
            pl.BlockSpec((tm, d), lambda i: (i, 0)),
            _const_spec((1, d)),
            _const_spec(w_tok.shape),
            _const_spec(w_fm.shape),
            _const_spec(gains.shape),
            pl.BlockSpec((ROT_HALF, tm), lambda i: (0, i)),
            pl.BlockSpec((ROT_HALF, tm), lambda i: (0, i)),
        ] + cast_specs,
        out_specs=view_specs * 3 + [
            pl.BlockSpec((tm, 2 * SPLIT), lambda i: (i, 0)),
            pl.BlockSpec((tm, SPLIT), lambda i: (i, 0)),
            fm_spec, fm_spec,
        ] + cast_specs,
        out_shape=view_shapes * 3 + [
            jax.ShapeDtypeStruct((s, 2 * SPLIT), F32),
            jax.ShapeDtypeStruct((s, SPLIT), BF16),
            fm, fm,
        ] + [jax.ShapeDtypeStruct(w_later.shape, BF16) for w_later in later_weights],
        scratch_shapes=[pltpu.VMEM((ATT_IN_PARTS, 3, SPLIT // LANES, tm // ATT_IN_PARTS, LANES), F32)],
        compiler_params=_params("parallel"),
        name="att_in_proj",
    )(x2d, norm_g.reshape(1, d), w_tok, w_fm, gains, cos_t, sin_t, *later_weights)
    n_pat = len(DSW_PATTERNS)
    return ((outs[0:n_pat], outs[n_pat:2 * n_pat], outs[2 * n_pat:3 * n_pat]) + tuple(outs[3 * n_pat:3 * n_pat + 4])
            + (tuple(outs[3 * n_pat + 4:]),))


DSW_L_LANES_PER_HEAD = BLK // N_GROUPS64


def _dsw_online_pair(q_pair, kk, vv, mask, acc_in, l_tile_in, hp, last):
    low_half = lax.broadcasted_iota(jnp.int32, (BLK, 2 * HEAD_DIM), 1) < HEAD_DIM
    halves, state_cols = [], []
    for e in range(2):
        head = 2 * hp + e
        sel = low_half if e == 0 else jnp.logical_not(low_half)
        qm = jnp.where(sel, q_pair, jnp.zeros_like(q_pair))
        sc = lax.dot_general(qm, kk, (((1,), (1,)), ((), ())), preferred_element_type=F32)
        sc = jnp.where(mask, sc, NEG_INF)
        m_new = jnp.max(sc, axis=-1, keepdims=True)
        if acc_in is not None:
            m_old = l_tile_in[:, head * DSW_L_LANES_PER_HEAD:head * DSW_L_LANES_PER_HEAD + 1]
            m_new = jnp.maximum(m_old, m_new)
            alpha = jnp.exp2(m_old - m_new)
        p = jnp.exp2(sc - m_new)
        l = jnp.sum(p, axis=-1, keepdims=True)
        pv = jnp.dot(p.astype(BF16), vv, preferred_element_type=F32)
        if acc_in is not None:
            l = l + alpha
            pv = pv + alpha * acc_in
        halves.append(pv / l)
        state_cols.append(jnp.ones_like(l) if last else m_new + jnp.log2(l))
    return jnp.where(low_half, halves[0], halves[1]), state_cols


def _dsw_hop_kernel(bound_ref, *refs, has_state, last, blocks_per_step):
    refs = list(refs)
    q_ref, kprev_ref, kcur_ref, vprev_ref, vcur_ref, ehot_ref = refs[:6]
    pos = 6
    if has_state:
        accin_ref, lin_ref = refs[pos:pos + 2]
        pos += 2
    acc_ref, l_ref = refs[pos:pos + 2]
    perm_sc = None if last else refs[pos + 2]
    g_blocks = blocks_per_step
    step = pl.program_id(1)

    def window(prev_ref, cur_ref, g, cols):
        before = prev_ref[0, :, cols] if g == 0 else cur_ref[g - 1, :, cols]
        return jnp.concatenate([before, cur_ref[g, :, cols]], axis=0)

    def emit(g, accs, l_tile):
        n_pairs = N_GROUPS64 // 2
        if last:
            for hp in range(n_pairs):
                acc_ref[g, :, hp * 2 * HEAD_DIM:(hp + 1) * 2 * HEAD_DIM] = accs[hp]
            l_ref[g] = l_tile
            return
        slot = perm_sc.at[g % 2]
        for hp in range(n_pairs):
            slot[hp] = accs[hp]
        slot[n_pairs] = l_tile
        ratio = DSW_DIL_RATIO
        rows = BLK // ratio
        out_blk = g // ratio
        row_dst = slice((g % ratio) * rows, (g % ratio + 1) * rows)
        for m in range(ratio):
            src = pl.ds(m, rows, stride=ratio)
            for hp in range(n_pairs):
                col0 = m * SPLIT + hp * 2 * HEAD_DIM
                acc_ref[out_blk, row_dst, col0:col0 + 2 * HEAD_DIM] = slot[hp, src, :]
            l_ref[out_blk, row_dst, m * BLK:(m + 1) * BLK] = slot[n_pairs, src, :]

    unshifted = bound_ref[0] <= DSW_UNSHIFTED_SCORE_BOUND

    @pl.when(jnp.logical_not(unshifted))
    def _():
        qi = lax.broadcasted_iota(jnp.int32, (BLK, 2 * BLK), 0) + BLK
        kj = lax.broadcasted_iota(jnp.int32, (BLK, 2 * BLK), 1)
        dist = qi - kj
        band = (dist >= 0) & (dist <= BLK)
        lane_head = lax.broadcasted_iota(jnp.int32, (BLK, BLK), 1) // DSW_L_LANES_PER_HEAD
        for g in range(g_blocks):
            mask = band & (kj >= jnp.where(step == 0, BLK, 0)) if g == 0 else band
            l_in = lin_ref[g] if has_state else None
            accs = []
            l_tile = jnp.zeros((BLK, BLK), F32)
            for hp in range(N_GROUPS64 // 2):
                cols = slice(hp * 2 * HEAD_DIM, (hp + 1) * 2 * HEAD_DIM)
                acc, state_cols = _dsw_online_pair(
                    q_ref[g, :, cols], window(kprev_ref, kcur_ref, g, cols), window(vprev_ref, vcur_ref, g, cols),
                    mask, accin_ref[g, :, cols] if has_state else None, l_in, hp, last)
                accs.append(acc)
                for e in range(2):
                    l_tile = jnp.where(lane_head == 2 * hp + e, state_cols[e], l_tile)
            emit(g, accs, l_tile)

    @pl.when(unshifted)
    def _():
        _dsw_unshifted_blocks(q_ref, kprev_ref, kcur_ref, vprev_ref, vcur_ref, ehot_ref,
                              accin_ref if has_state else None, lin_ref if has_state else None,
                              window, emit, step, g_blocks)


def _dsw_unshifted_blocks(q_ref, kprev_ref, kcur_ref, vprev_ref, vcur_ref, ehot_ref, accin_ref, lin_ref,
                          window, emit, step, g_blocks):
    has_state = accin_ref is not None

    qi = (lax.broadcasted_iota(jnp.int32, (2 * BLK, 2 * BLK), 0) & (BLK - 1)) + BLK
    kj = lax.broadcasted_iota(jnp.int32, (2 * BLK, 2 * BLK), 1)
    dist = qi - kj
    band = (dist >= 0) & (dist <= BLK)
    q_low = lax.broadcasted_iota(jnp.int32, (BLK, 2 * HEAD_DIM), 1) < HEAD_DIM
    v_low = lax.broadcasted_iota(jnp.int32, (2 * BLK, 2 * HEAD_DIM), 1) < HEAD_DIM

    for g in range(g_blocks):
        if g == 0:
            mask = band & (kj >= jnp.where(step == 0, BLK, 0))
        else:
            mask = band
        l_tile = lin_ref[g] if has_state else jnp.zeros((BLK, BLK), F32)
        accs = []
        for hp in range(N_GROUPS64 // 2):
            cols = slice(hp * 2 * HEAD_DIM, (hp + 1) * 2 * HEAD_DIM)
            q_pair = q_ref[g, :, cols]
            kk = window(kprev_ref, kcur_ref, g, cols)
            vv = window(vprev_ref, vcur_ref, g, cols)
            qz = jnp.zeros_like(q_pair)
            vz = jnp.zeros_like(vv)
            q2 = jnp.concatenate([jnp.where(q_low, q_pair, qz), jnp.where(q_low, qz, q_pair)], axis=0)
            sc = lax.dot_general(q2, kk, (((1,), (1,)), ((), ())), preferred_element_type=F32)
            p2 = jnp.where(mask, jnp.exp2(sc), 0.0).astype(BF16)
            p_cat = jnp.concatenate([p2[:BLK], p2[BLK:]], axis=1)
            v_split = jnp.concatenate([jnp.where(v_low, vv, vz), jnp.where(v_low, vz, vv)], axis=0)
            rhs = jnp.concatenate([v_split, ehot_ref[hp]], axis=1)
            out2 = jnp.dot(p_cat, rhs, preferred_element_type=F32)
            acc = out2[:, :2 * HEAD_DIM]
            if has_state:
                acc = acc + accin_ref[g, :, cols]
            l_tile = l_tile + out2[:, 2 * HEAD_DIM:]
            accs.append(acc)
        emit(g, accs, l_tile)


def _dsw_residue_of(hop):
    if hop == 0:
        return lambda g: g
    prev = _dsw_residue_of(hop - 1)
    prev_dil = DSW_PATTERNS[hop - 1][1]
    return lambda g: prev_dil * (g % DSW_DIL_RATIO) + prev(g // DSW_DIL_RATIO)


def _dsw_pattern(q, k, v, hop, state, ehot, bound):
    dil = DSW_PATTERNS[hop][1]
    last = hop == len(DSW_PATTERNS) - 1
    s = q.size // SPLIT
    nb = s // (BLK * dil)
    g_blocks = min(DSW_BLOCKS_PER_STEP, nb)
    ratio = DSW_DIL_RATIO
    residue = _dsw_residue_of(hop)
    view = lambda a: a.reshape(nb, BLK, dil * SPLIT)
    qkv_spec = pl.BlockSpec((g_blocks, BLK, SPLIT), lambda r, j, b: (j, 0, residue(r)))
    prev_spec = pl.BlockSpec((1, BLK, SPLIT), lambda r, j, b: (jnp.maximum(j * g_blocks - 1, 0), 0, residue(r)))
    state_spec = lambda width: pl.BlockSpec((g_blocks, BLK, width), lambda r, j, b: (j, 0, r))
    operands = [view(q), view(k), view(k), view(v), view(v), ehot]
    in_specs = [qkv_spec, prev_spec, qkv_spec, prev_spec, qkv_spec,
                pl.BlockSpec(ehot.shape, lambda r, j, b: (0, 0, 0))]
    if state is not None:
        operands += list(state)
        in_specs += [state_spec(SPLIT), state_spec(BLK)]
    scratch = []
    if last:
        own_spec = lambda width: pl.BlockSpec((g_blocks, BLK, width), lambda r, j, b: (j, 0, residue(r)))
        out_specs = [own_spec(SPLIT), own_spec(BLK)]
        out_shape = [jax.ShapeDtypeStruct((nb, BLK, dil * SPLIT), F32),
                     jax.ShapeDtypeStruct((nb, BLK, dil * BLK), F32)]
    else:
        assert DSW_PATTERNS[hop + 1][1] == dil * ratio and g_blocks % ratio == 0
        next_spec = lambda width: pl.BlockSpec((g_blocks // ratio, BLK, ratio * width), lambda r, j, b: (j, 0, r))
        out_specs = [next_spec(SPLIT), next_spec(BLK)]
        out_shape = [jax.ShapeDtypeStruct((nb // ratio, BLK, dil * ratio * SPLIT), F32),
                     jax.ShapeDtypeStruct((nb // ratio, BLK, dil * ratio * BLK), F32)]
        scratch.append(pltpu.VMEM((2, N_GROUPS64 // 2 + 1, BLK, 2 * HEAD_DIM), F32))
    out = pl.pallas_call(
        functools.partial(_dsw_hop_kernel, has_state=state is not None, last=last, blocks_per_step=g_blocks),
        grid_spec=pltpu.PrefetchScalarGridSpec(
            num_scalar_prefetch=1, grid=(dil, nb // g_blocks), in_specs=in_specs, out_specs=out_specs,
            scratch_shapes=scratch),
        out_shape=out_shape,
        compiler_params=_params("parallel", "arbitrary"),
        name=f"dsw_attention_dil{dil}",
    )(bound, *operands)
    return tuple(out)


def _dsw_constants():
    ehot = np.zeros((N_GROUPS64 // 2, 4 * BLK, BLK), np.float32)
    xexp = np.zeros((BLK, SPLIT), np.float32)
    for head in range(N_GROUPS64):
        hp, e = divmod(head, 2)
        lanes = slice(head * DSW_L_LANES_PER_HEAD, (head + 1) * DSW_L_LANES_PER_HEAD)
        ehot[hp, e * 2 * BLK:(e + 1) * 2 * BLK, lanes] = 1.0
        xexp[head * DSW_L_LANES_PER_HEAD, head * HEAD_DIM:(head + 1) * HEAD_DIM] = 1.0
    return jnp.asarray(ehot, BF16), jnp.asarray(xexp, BF16)


def _dsw_attention(q_views, k_views, v_views, score_bound):
    ehot, _ = _dsw_constants()
    bound = score_bound.reshape(1).astype(F32)
    state = None
    for hop in range(len(DSW_PATTERNS)):
        state = _dsw_pattern(q_views[hop], k_views[hop], v_views[hop], hop, state, ehot, bound)
    return state


def _diff_finalize(o0, o1, lam_ref, sg_ref, o_ref):
    lam_p = lam_ref[...]
    lam = (jnp.exp(jnp.sum(lam_p[0:1] * lam_p[1:2], axis=-1, keepdims=True))
           - jnp.exp(jnp.sum(lam_p[2:3] * lam_p[3:4], axis=-1, keepdims=True)) + LAM_INIT)
    od = o0 - lam * o1
    ms = jnp.mean(od * od, axis=0, keepdims=True)
    y = od * lax.rsqrt(ms + RMS_EPS) * sg_ref[...] * (1.0 - LAM_INIT)
    o_ref[...] = y.T


def _component_queries(qt):
    zeros = jnp.zeros((HEAD_DIM, qt.shape[1]), qt.dtype)
    return (jnp.concatenate([qt[:HEAD_DIM], zeros], axis=0), jnp.concatenate([zeros, qt[HEAD_DIM:]], axis=0))


def _causal_keep(i, k0, tq, tk):
    kpos = k0 + lax.broadcasted_iota(jnp.int32, (tk, tq), 0)
    qpos = i * tq + lax.broadcasted_iota(jnp.int32, (tk, tq), 1)
    return kpos <= qpos


def _diff_unshifted_kernel(qt_ref, k_ref, vt_ref, lam_ref, sg_ref, o_ref, l_sc, acc_sc, *, tq, tk):
    i = pl.program_id(1)
    qc = _component_queries(qt_ref[...])
    l_sc[...] = jnp.zeros(l_sc.shape, F32)
    acc_sc[...] = jnp.zeros(acc_sc.shape, F32)

    def full_chunk(j):
        k0 = pl.multiple_of(j * tk, tk)
        vt = vt_ref[:, pl.ds(k0, tk)]
        for c in range(2):
            st = jnp.dot(k_ref[pl.ds(k0, tk), :], qc[c], preferred_element_type=F32)
            p = jnp.exp2(st)
            l_sc[c] += jnp.sum(p.reshape(tk // SUBLANES, SUBLANES, tq), axis=0)
            acc_sc[c] += jnp.dot(vt, p.astype(BF16), preferred_element_type=F32)

    def chunk_quad(jq, carry):
        for u in range(4):
            full_chunk(4 * jq + u)
        return carry

    n_quads = lax.shift_right_logical(i, 2)
    lax.fori_loop(0, n_quads, chunk_quad, 0)

    @pl.when((i & 2) == 2)
    def _():
        full_chunk(4 * n_quads)
        full_chunk(4 * n_quads + 1)

    @pl.when((i & 1) == 1)
    def _():
        full_chunk(i - 1)

    sb = DIFF_DIAG_BLOCK
    k0 = pl.multiple_of(i * tk, tk)
    tri = lax.broadcasted_iota(jnp.int32, (sb, sb), 0) <= lax.broadcasted_iota(jnp.int32, (sb, sb), 1)
    for qb in range(tq // sb):
        nk = (qb + 1) * sb
        lanes = slice(qb * sb, (qb + 1) * sb)
        vt = vt_ref[:, pl.ds(k0, nk)]
        for c in range(2):
            st = jnp.dot(k_ref[pl.ds(k0, nk), :], qc[c][:, lanes], preferred_element_type=F32)
            p = jnp.exp2(st)
            p_tri = jnp.where(tri, p[qb * sb:, :], 0.0)
            p = jnp.concatenate([p[:qb * sb, :], p_tri], axis=0) if qb else p_tri
            l_sc[c, :, lanes] += jnp.sum(p.reshape(nk // SUBLANES, SUBLANES, sb), axis=0)
            acc_sc[c, :, lanes] += jnp.dot(vt, p.astype(BF16), preferred_element_type=F32)

    l0 = jnp.sum(l_sc[0], axis=0, keepdims=True)
    l1 = jnp.sum(l_sc[1], axis=0, keepdims=True)
    _diff_finalize(acc_sc[0] / l0, acc_sc[1] / l1, lam_ref, sg_ref, o_ref)


def _diff_kernel(qt_ref, k_ref, vt_ref, lam_ref, sg_ref, o_ref, m_sc, l_sc, acc_sc, *, tq, tk):
    i = pl.program_id(1)
    qc = _component_queries(qt_ref[...])
    m_sc[...] = jnp.full(m_sc.shape, NEG_INF, F32)
    l_sc[...] = jnp.zeros(l_sc.shape, F32)
    acc_sc[...] = jnp.zeros(acc_sc.shape, F32)

    def chunk(j, masked):
        k0 = pl.multiple_of(j * tk, tk)
        vt = vt_ref[:, pl.ds(k0, tk)]
        for c in range(2):
            st = jnp.dot(k_ref[pl.ds(k0, tk), :], qc[c], preferred_element_type=F32)
            if masked:
                st = jnp.where(_causal_keep(i, k0, tq, tk), st, NEG_INF)
            m_old = m_sc[c]
            m_new = jnp.maximum(m_old, jnp.max(st, axis=0, keepdims=True))
            alpha = jnp.exp2(m_old - m_new)
            p = jnp.exp2(st - m_new)
            l_sc[c] = alpha * l_sc[c] + jnp.sum(p, axis=0, keepdims=True)
            acc_sc[c] = alpha * acc_sc[c] + jnp.dot(vt, p.astype(BF16), preferred_element_type=F32)
            m_sc[c] = m_new

    n_full = (i * tq) // tk

    def full_chunk(j, carry):
        chunk(j, False)
        return carry

    lax.fori_loop(0, n_full, full_chunk, 0)
    for jj in range(tq // tk):
        chunk(n_full + jj, True)

    _diff_finalize(acc_sc[0] / l_sc[0], acc_sc[1] / l_sc[1], lam_ref, sg_ref, o_ref)


def _diff_dispatch_kernel(bound_ref, qt_ref, k_ref, vt_ref, lam_ref, sg_ref, o_ref, m_sc, l1_sc, l8_sc, acc_sc, *,
                          tq, tk):
    unshifted = bound_ref[0] <= DIFF_UNSHIFTED_SCORE_BOUND

    @pl.when(unshifted)
    def _():
        _diff_unshifted_kernel(qt_ref, k_ref, vt_ref, lam_ref, sg_ref, o_ref, l8_sc, acc_sc, tq=tq, tk=tk)

    @pl.when(jnp.logical_not(unshifted))
    def _():
        _diff_kernel(qt_ref, k_ref, vt_ref, lam_ref, sg_ref, o_ref, m_sc, l1_sc, acc_sc, tq=tq, tk=tk)


def _diff_attention(qd_t, kd_tok, vd_t, lam_params, subln_col, score_bound):
    s = qd_t.shape[1]
    tq, tk = DIFF_TQ, DIFF_TK
    assert tq == tk and s % tq == 0 and tq % DIFF_DIAG_BLOCK == 0
    grid_spec = pltpu.PrefetchScalarGridSpec(
        num_scalar_prefetch=1,
        grid=(DIFF_HEADS, s // tq),
        in_specs=[
            pl.BlockSpec((DIFF_V_DIM, tq), lambda h, i, bound: (h, i)),
            pl.BlockSpec((s, 2 * HEAD_DIM), lambda h, i, bound: (0, h)),
            pl.BlockSpec((DIFF_V_DIM, s), lambda h, i, bound: (h, 0)),
            pl.BlockSpec(lam_params.shape, lambda h, i, bound: (0, 0)),
            pl.BlockSpec(subln_col.shape, lambda h, i, bound: (0, 0)),
        ],
        out_specs=pl.BlockSpec((tq, DIFF_V_DIM), lambda h, i, bound: (i, h)),
        scratch_shapes=[pltpu.VMEM((2, 1, tq), F32), pltpu.VMEM((2, 1, tq), F32), pltpu.VMEM((2, SUBLANES, tq), F32),
                        pltpu.VMEM((2, DIFF_V_DIM, tq), F32)],
    )
    return pl.pallas_call(
        functools.partial(_diff_dispatch_kernel, tq=tq, tk=tk),
        grid_spec=grid_spec,
        out_shape=jax.ShapeDtypeStruct((s, DIFF_HEADS * DIFF_V_DIM), F32),
        compiler_params=_params("parallel", "arbitrary"),
        name="diff_attention",
    )(score_bound.reshape(1).astype(F32), qd_t, kd_tok, vd_t, lam_params, subln_col)


def _att_out_compute(x_ref, acc_ref, l_ref, xexp_ref, od_ref, g_ref, w_ref, tok_sc, dil):
    n_lane_tiles = SPLIT // LANES
    rows = tok_sc.shape[1] // dil
    for r in range(dil):
        dst = pl.ds(r, rows, stride=dil)
        for c in range(n_lane_tiles):
            col0 = r * SPLIT + c * LANES
            tok_sc[c, dst, :] = acc_ref[0, :, col0:col0 + LANES]
        tok_sc[n_lane_tiles, dst, :] = l_ref[0, :, r * LANES:(r + 1) * LANES]
    acc = jnp.concatenate([tok_sc[c] for c in range(n_lane_tiles)], axis=1)
    r = 1.0 / tok_sc[n_lane_tiles]
    r_hi = r.astype(BF16)
    r_lo = (r - r_hi.astype(F32)).astype(BF16)
    r_exp = (jnp.dot(r_hi, xexp_ref[...], preferred_element_type=F32)
             + jnp.dot(r_lo, xexp_ref[...], preferred_element_type=F32))
    g = g_ref[...]
    y = jnp.concatenate([acc * r_exp, od_ref[...]], axis=-1) * (g * jax.nn.sigmoid(g))
    return x_ref[...] + jnp.dot(y.astype(BF16), w_ref[...], preferred_element_type=F32)


def _gelu(x):
    return 0.5 * x * (1.0 + lax.erf(x * (2.0 ** -0.5)))


def _sgu_compute(x, ng_ref, win_ref, lng_ref, lnb_ref, ws_ref, bs_ref, wout_ref, width):
    tm = x.shape[0]
    ms = jnp.mean(x * x, axis=-1, keepdims=True)
    h = (x * lax.rsqrt(ms + RMS_EPS) * ng_ref[...]).astype(BF16)
    v_raw = jnp.dot(h, win_ref[:, width:2 * width], preferred_element_type=F32)
    u_raw = jnp.dot(h, win_ref[:, 0:width], preferred_element_type=F32)
    v = _gelu(v_raw)
    mu = jnp.mean(v, axis=-1, keepdims=True)
    vc = v - mu
    vn = vc * lax.rsqrt(jnp.mean(vc * vc, axis=-1, keepdims=True) + LN_EPS)
    vn = (vn * lng_ref[...] + lnb_ref[...]).astype(BF16)
    gw = width // SGU_GROUPS
    row = lax.broadcasted_iota(jnp.int32, (SGU_CHUNK, SGU_CHUNK), 0)
    col = lax.broadcasted_iota(jnp.int32, (SGU_CHUNK, SGU_CHUNK), 1)
    causal = col <= row
    sp_groups = []
    for grp in range(SGU_GROUPS):
        ws = jnp.where(causal, ws_ref[grp], 0.0).astype(BF16)
        bias = bs_ref[grp]
        chunks = []
        for c in range(tm // SGU_CHUNK):
            vg = vn[c * SGU_CHUNK:(c + 1) * SGU_CHUNK, grp * gw:(grp + 1) * gw]
            chunks.append(jnp.dot(ws, vg, preferred_element_type=F32) + bias)
        sp_groups.append(jnp.concatenate(chunks, axis=0))
    sp = jnp.concatenate(sp_groups, axis=1)
    g = jnp.dot(h, win_ref[:, 2 * width:3 * width], preferred_element_type=F32)
    u = _gelu(u_raw)
    y = u * sp * (g * jax.nn.sigmoid(g))
    return x + jnp.dot(y.astype(BF16), wout_ref[...], preferred_element_type=F32)


def _att_out_sgu_kernel(x_ref, acc_ref, l_ref, xexp_ref, od_ref, g_ref, watt_ref,
                        ng_ref, win_ref, lng_ref, lnb_ref, ws_ref, bs_ref, wout_ref, o_ref, tok_sc, *, dil, width):
    x1 = _att_out_compute(x_ref, acc_ref, l_ref, xexp_ref, od_ref, g_ref, watt_ref, tok_sc, dil)
    o_ref[...] = _sgu_compute(x1, ng_ref, win_ref, lng_ref, lnb_ref, ws_ref, bs_ref, wout_ref, width)


def _att_out_sgu(x2d, acc_view, l_view, od, g, w_att_out, norm_g, w_in, ln_g, ln_b, w_s, b_s, w_out):
    assert w_att_out.dtype == w_in.dtype == w_out.dtype == BF16
    s, d = x2d.shape
    width = w_out.shape[0]
    tm = ROW_TILE
    dil = DSW_PATTERNS[-1][1]
    rows = tm // dil
    per_blk = BLK // rows
    assert acc_view.shape == (s // (BLK * dil), BLK, dil * SPLIT) and tm % dil == 0 and BLK % rows == 0
    assert l_view.shape == (s // (BLK * dil), BLK, dil * LANES)
    assert tm % SGU_CHUNK == 0 and rows % 8 == 0
    _, xexp = _dsw_constants()
    row = lambda w: pl.BlockSpec((tm, w), lambda i: (i, 0))
    view_spec = lambda w: pl.BlockSpec((1, rows, dil * w), lambda i: (i // per_blk, i % per_blk, 0))
    single = lambda shape: pl.BlockSpec(shape, lambda *_: (0,) * len(shape), pipeline_mode=pl.Buffered(1))
    return pl.pallas_call(
        functools.partial(_att_out_sgu_kernel, dil=dil, width=width),
        grid=(s // tm,),
        in_specs=[row(d), view_spec(SPLIT), view_spec(LANES), _const_spec(xexp.shape), row(SPLIT), row(2 * SPLIT),
                  single(w_att_out.shape),
                  _const_spec((1, d)), single(w_in.shape), _const_spec((1, width)), _const_spec((1, width)),
                  _const_spec(w_s.shape), _const_spec((SGU_GROUPS, SGU_CHUNK, 1)), single(w_out.shape)],
        out_specs=row(d),
        out_shape=jax.ShapeDtypeStruct((s, d), F32),
        scratch_shapes=[pltpu.VMEM((SPLIT // LANES + 1, tm, LANES), F32)],
        compiler_params=_params("parallel"),
        name="att_out_sgu",
    )(x2d, acc_view, l_view, xexp, od, g, w_att_out,
      norm_g.reshape(1, d), w_in, ln_g.reshape(1, width), ln_b.reshape(1, width),
      w_s, b_s.reshape(SGU_GROUPS, SGU_CHUNK, 1), w_out)


def _rope_tables_fm(seq_len):
    pos = jnp.arange(seq_len, dtype=F32)
    inv = 1.0 / (ROPE_THETA ** (jnp.arange(0, 2 * ROT_HALF, 2, dtype=F32) / (2 * ROT_HALF)))
    ang = inv[:, None] * pos[None, :]
    return jnp.cos(ang), jnp.sin(ang)


def kernel(x, att_norm, att_w_in, dsw_q_norm, dsw_k_norm, diff_q_norm, diff_k_norm, diff_lam_q1, diff_lam_k1,
           diff_lam_q2, diff_lam_k2, diff_subln, att_w_out, sgu_norm, sgu_w_in, sgu_ln_g, sgu_ln_b, sgu_w_s,
           sgu_b_s, sgu_w_out):
    b, s, d = x.shape
    assert b == 1 and s % (BLK * DSW_PATTERNS[-1][1]) == 0
    x2d = x.reshape(s, d)
    cos_t, sin_t = _rope_tables_fm(s)

    gains = jnp.stack([dsw_q_norm[0], dsw_k_norm[0], diff_q_norm[0], diff_k_norm[0]])
    gains = jnp.broadcast_to(gains[:, None, :, None], (4, N_GROUPS64, HEAD_DIM, 1))
    qa_views, ka_views, va_views, g, kd_tok, qd_t, vd_t, later_bf16 = _att_in(
        x2d, att_norm[0], att_w_in[0], gains, cos_t, sin_t, [att_w_out[0], sgu_w_in[0], sgu_w_out[0]])
    att_w_out_bf16, sgu_w_in_bf16, sgu_w_out_bf16 = later_bf16
    bound = lambda gq, gk: 1.02 * LOG2_E * HEAD_DIM ** 0.5 * jnp.max(jnp.abs(gq)) * jnp.max(jnp.abs(gk))
    oa_acc, oa_l = _dsw_attention(qa_views, ka_views, va_views, bound(dsw_q_norm[0], dsw_k_norm[0]))
    lam_params = jnp.stack([diff_lam_q1[0], diff_lam_k1[0], diff_lam_q2[0], diff_lam_k2[0]])
    od = _diff_attention(qd_t, kd_tok, vd_t, lam_params, diff_subln[0].reshape(DIFF_V_DIM, 1),
                         bound(diff_q_norm[0], diff_k_norm[0]))
    x2d = _att_out_sgu(x2d, oa_acc, oa_l, od, g, att_w_out_bf16,
                       sgu_norm[0], sgu_w_in_bf16, sgu_ln_g[0], sgu_ln_b[0], sgu_w_s[0], sgu_b_s[0], sgu_w_out_bf16)
    return x2d.reshape(b, s, d)
```

```python
import functools
import math

import numpy as np
import jax
import jax.numpy as jnp
from jax import lax
from jax.experimental import pallas as pl
from jax.experimental.pallas import tpu as pltpu

F32 = jnp.float32
BF16 = jnp.bfloat16

LANES = 128
SUBLANES = 8
HEAD_DIM = 64
ROT_HALF = 8
ROPE_THETA = 500000.0
BLK = 128
N_GROUPS64 = 8
SPLIT = N_GROUPS64 * HEAD_DIM
DSW_PATTERNS = ((128, 1), (512, 4), (2048, 16))
DSW_DIL_RATIO = 4
DIFF_HEADS = 4
DIFF_V_DIM = 128
SGU_GROUPS = 8
SGU_CHUNK = 128
RMS_EPS = 1e-6
LN_EPS = 1e-5
NEG_INF = -1e30
LAM_INIT = 0.8 - 0.6 * math.exp(-0.3 * 0)
LOG2_E = math.log2(math.e)
DIFF_UNSHIFTED_SCORE_BOUND = 96.0
DSW_UNSHIFTED_SCORE_BOUND = 96.0

V7X_VMEM_LIMIT_BYTES = 56 * 1024 * 1024

ROW_TILE = 512
ATT_IN_PARTS = 2
DSW_BLOCKS_PER_STEP = 8
DIFF_TQ = 1024
DIFF_TK = 1024
DIFF_DIAG_BLOCK = 512


def _params(*sem):
    return pltpu.CompilerParams(dimension_semantics=sem, vmem_limit_bytes=V7X_VMEM_LIMIT_BYTES)


def _const_spec(shape):
    nd = len(shape)
    return pl.BlockSpec(shape, lambda *_: (0,) * nd)


def _norm_rope_fm(p, gain, cos, sin, scale):
    t = p.shape[1]
    p3 = p.reshape(N_GROUPS64, HEAD_DIM, t)
    ms = jnp.mean(p3 * p3, axis=1, keepdims=True)
    y = p3 * lax.rsqrt(ms + RMS_EPS) * gain
    x1 = y[:, 0:ROT_HALF, :]
    x2 = y[:, ROT_HALF:2 * ROT_HALF, :]
    out = jnp.concatenate([x1 * cos - x2 * sin, x2 * cos + x1 * sin, y[:, 2 * ROT_HALF:, :]], axis=1)
    if scale != 1.0:
        out = out * scale
    return out.reshape(SPLIT, t)


def _emit_pattern_views(tok, tok_sc, view_refs, part, n_parts):
    t = tok.shape[0]
    n_lane_tiles = SPLIT // LANES
    for c in range(n_lane_tiles):
        tok_sc[c] = tok[:, c * LANES:(c + 1) * LANES]
    for (_, dil), ref in zip(DSW_PATTERNS, view_refs):
        if dil == 1:
            ref[part * t:(part + 1) * t, :] = tok.astype(BF16)
            continue
        n = t // dil
        for r in range(dil):
            rows = jnp.concatenate([tok_sc[c, pl.ds(r, n, stride=dil), :] for c in range(n_lane_tiles)], axis=1)
            ref[0, part * n:(part + 1) * n, r * SPLIT:(r + 1) * SPLIT] = rows.astype(BF16)


def _att_in_kernel(x_ref, ng_ref, wtok_ref, wfm_ref, gains_ref, cos_ref, sin_ref, *refs, n_cast):
    n_pat = len(DSW_PATTERNS)
    cast_in, refs = refs[:n_cast], refs[n_cast:]
    qa_views, ka_views, va_views = refs[0:n_pat], refs[n_pat:2 * n_pat], refs[2 * n_pat:3 * n_pat]
    g_ref, kd_ref, qd_ref, vd_ref = refs[3 * n_pat:3 * n_pat + 4]
    cast_out = refs[3 * n_pat + 4:3 * n_pat + 4 + n_cast]
    tok_sc = refs[-1]
    for src, dst in zip(cast_in, cast_out):
        dst[...] = src[...].astype(BF16)
    scales = (HEAD_DIM ** -0.5 * LOG2_E, 1.0, HEAD_DIM ** -0.5 * LOG2_E, 1.0)
    n_parts = ATT_IN_PARTS
    t = x_ref.shape[0] // n_parts
    for part in range(n_parts):
        rows = slice(part * t, (part + 1) * t)
        x = x_ref[rows, :]
        ms = jnp.mean(x * x, axis=-1, keepdims=True)
        h = (x * lax.rsqrt(ms + RMS_EPS) * ng_ref[...]).astype(BF16)
        tok = jnp.dot(h, wtok_ref[...], preferred_element_type=F32)
        _emit_pattern_views(tok[:, :SPLIT], tok_sc.at[part, 0], va_views, part, n_parts)
        g_ref[rows, :] = tok[:, SPLIT:]
        cos = cos_ref[:, rows][None]
        sin = sin_ref[:, rows][None]
        def feature_major(idx):
            return lax.dot_general(wfm_ref[idx * SPLIT:(idx + 1) * SPLIT, :], h, (((1,), (1,)), ((), ())),
                                   preferred_element_type=F32)

        def normed(idx):
            return _norm_rope_fm(feature_major(idx), gains_ref[idx], cos, sin, scales[idx])

        _emit_pattern_views(normed(0).T, tok_sc.at[part, 1], qa_views, part, n_parts)
        _emit_pattern_views(normed(1).T, tok_sc.at[part, 2], ka_views, part, n_parts)
        qd_ref[:, rows] = normed(2).astype(BF16)
        kd = normed(3).T.astype(BF16)
        for head in range(DIFF_HEADS):
            kd_ref[head, rows, :] = kd[:, head * 2 * HEAD_DIM:(head + 1) * 2 * HEAD_DIM]
        vd_ref[:, rows] = feature_major(4).astype(BF16)


def _att_in(x2d, norm_g, w_in, gains, cos_t, sin_t, later_weights):
    s, d = x2d.shape
    tm = ROW_TILE
    n_steps = s // tm
    cast_specs = []
    for w_later in later_weights:
        rows = w_later.shape[0] // n_steps
        assert w_later.shape[0] % n_steps == 0 and rows % 16 == 0
        cast_specs.append(pl.BlockSpec((rows, w_later.shape[1]), lambda i: (i, 0)))
    w = w_in.astype(BF16)
    w_tok = jnp.concatenate([w[:, 2 * SPLIT:4 * SPLIT], w[:, 7 * SPLIT:8 * SPLIT]], axis=1)
    w_fm = jnp.concatenate([w[:, 0:2 * SPLIT], w[:, 4 * SPLIT:7 * SPLIT]], axis=1).T
    view_shapes, view_specs = [], []
    for _, dil in DSW_PATTERNS:
        if dil == 1:
            view_shapes.append(jax.ShapeDtypeStruct((s, SPLIT), BF16))
            view_specs.append(pl.BlockSpec((tm, SPLIT), lambda i: (i, 0)))
            continue
        rows = tm // dil
        assert tm % dil == 0 and BLK % rows == 0 and rows % (16 * ATT_IN_PARTS) == 0
        per_blk = BLK // rows
        view_shapes.append(jax.ShapeDtypeStruct((s // (BLK * dil), BLK, dil * SPLIT), BF16))
        view_specs.append(pl.BlockSpec((1, rows, dil * SPLIT),
                                       functools.partial(lambda i, per_blk: (i // per_blk, i % per_blk, 0),
                                                         per_blk=per_blk)))
    fm = jax.ShapeDtypeStruct((SPLIT, s), BF16)
    fm_spec = pl.BlockSpec((SPLIT, tm), lambda i: (0, i))
    outs = pl.pallas_call(
        functools.partial(_att_in_kernel, n_cast=len(later_weights)),
        grid=(n_steps,),
        in_specs=[
            pl.BlockSpec((tm, d), lambda i: (i, 0)),
            _const_spec((1, d)),
            _const_spec(w_tok.shape),
            _const_spec(w_fm.shape),
            _const_spec(gains.shape),
            pl.BlockSpec((ROT_HALF, tm), lambda i: (0, i)),
            pl.BlockSpec((ROT_HALF, tm), lambda i: (0, i)),
        ] + cast_specs,
        out_specs=view_specs * 3 + [
            pl.BlockSpec((tm, 2 * SPLIT), lambda i: (i, 0)),
            pl.BlockSpec((DIFF_HEADS, tm, 2 * HEAD_DIM), lambda i: (0, i, 0)),
            fm_spec, fm_spec,
        ] + cast_specs,
        out_shape=view_shapes * 3 + [
            jax.ShapeDtypeStruct((s, 2 * SPLIT), F32),
            jax.ShapeDtypeStruct((DIFF_HEADS, s, 2 * HEAD_DIM), BF16),
            fm, fm,
        ] + [jax.ShapeDtypeStruct(w_later.shape, BF16) for w_later in later_weights],
        scratch_shapes=[pltpu.VMEM((ATT_IN_PARTS, 3, SPLIT // LANES, tm // ATT_IN_PARTS, LANES), F32)],
        compiler_params=_params("parallel"),
        name="att_in_proj",
    )(x2d, norm_g.reshape(1, d), w_tok, w_fm, gains, cos_t, sin_t, *later_weights)
    n_pat = len(DSW_PATTERNS)
    return ((outs[0:n_pat], outs[n_pat:2 * n_pat], outs[2 * n_pat:3 * n_pat]) + tuple(outs[3 * n_pat:3 * n_pat + 4])
            + (tuple(outs[3 * n_pat + 4:]),))


DSW_L_LANES_PER_HEAD = BLK // N_GROUPS64


def _dsw_online_pair(q_pair, kk, vv, mask, acc_in, l_tile_in, hp, last):
    low_half = lax.broadcasted_iota(jnp.int32, (BLK, 2 * HEAD_DIM), 1) < HEAD_DIM
    halves, state_cols = [], []
    for e in range(2):
        head = 2 * hp + e
        sel = low_half if e == 0 else jnp.logical_not(low_half)
        qm = jnp.where(sel, q_pair, jnp.zeros_like(q_pair))
        sc = lax.dot_general(qm, kk, (((1,), (1,)), ((), ())), preferred_element_type=F32)
        sc = jnp.where(mask, sc, NEG_INF)
        m_new = jnp.max(sc, axis=-1, keepdims=True)
        if acc_in is not None:
            m_old = l_tile_in[:, head * DSW_L_LANES_PER_HEAD:head * DSW_L_LANES_PER_HEAD + 1]
            m_new = jnp.maximum(m_old, m_new)
            alpha = jnp.exp2(m_old - m_new)
        p = jnp.exp2(sc - m_new)
        l = jnp.sum(p, axis=-1, keepdims=True)
        pv = jnp.dot(p.astype(BF16), vv, preferred_element_type=F32)
        if acc_in is not None:
            l = l + alpha
            pv = pv + alpha * acc_in
        halves.append(pv / l)
        state_cols.append(jnp.ones_like(l) if last else m_new + jnp.log2(l))
    return jnp.where(low_half, halves[0], halves[1]), state_cols


def _dsw_hop_kernel(bound_ref, *refs, has_state, last, blocks_per_step):
    refs = list(refs)
    q_ref, kprev_ref, kcur_ref, vprev_ref, vcur_ref, ehot_ref = refs[:6]
    pos = 6
    if has_state:
        accin_ref, lin_ref = refs[pos:pos + 2]
        pos += 2
    acc_ref, l_ref = refs[pos:pos + 2]
    perm_sc = None if last else refs[pos + 2]
    g_blocks = blocks_per_step
    step = pl.program_id(1)

    def window(prev_ref, cur_ref, g, cols):
        before = prev_ref[0, :, cols] if g == 0 else cur_ref[g - 1, :, cols]
        return jnp.concatenate([before, cur_ref[g, :, cols]], axis=0)

    def emit(g, accs, l_tile):
        n_pairs = N_GROUPS64 // 2
        if last:
            for hp in range(n_pairs):
                acc_ref[g, :, hp * 2 * HEAD_DIM:(hp + 1) * 2 * HEAD_DIM] = accs[hp]
            l_ref[g] = l_tile
            return
        slot = perm_sc.at[g % 2]
        for hp in range(n_pairs):
            slot[hp] = accs[hp]
        slot[n_pairs] = l_tile
        ratio = DSW_DIL_RATIO
        rows = BLK // ratio
        out_blk = g // ratio
        row_dst = slice((g % ratio) * rows, (g % ratio + 1) * rows)
        for m in range(ratio):
            src = pl.ds(m, rows, stride=ratio)
            for hp in range(n_pairs):
                col0 = m * SPLIT + hp * 2 * HEAD_DIM
                acc_ref[out_blk, row_dst, col0:col0 + 2 * HEAD_DIM] = slot[hp, src, :]
            l_ref[out_blk, row_dst, m * BLK:(m + 1) * BLK] = slot[n_pairs, src, :]

    unshifted = bound_ref[0] <= DSW_UNSHIFTED_SCORE_BOUND

    @pl.when(jnp.logical_not(unshifted))
    def _():
        qi = lax.broadcasted_iota(jnp.int32, (BLK, 2 * BLK), 0) + BLK
        kj = lax.broadcasted_iota(jnp.int32, (BLK, 2 * BLK), 1)
        dist = qi - kj
        band = (dist >= 0) & (dist <= BLK)
        lane_head = lax.broadcasted_iota(jnp.int32, (BLK, BLK), 1) // DSW_L_LANES_PER_HEAD
        for g in range(g_blocks):
            mask = band & (kj >= jnp.where(step == 0, BLK, 0)) if g == 0 else band
            l_in = lin_ref[g] if has_state else None
            accs = []
            l_tile = jnp.zeros((BLK, BLK), F32)
            for hp in range(N_GROUPS64 // 2):
                cols = slice(hp * 2 * HEAD_DIM, (hp + 1) * 2 * HEAD_DIM)
                acc, state_cols = _dsw_online_pair(
                    q_ref[g, :, cols], window(kprev_ref, kcur_ref, g, cols), window(vprev_ref, vcur_ref, g, cols),
                    mask, accin_ref[g, :, cols] if has_state else None, l_in, hp, last)
                accs.append(acc)
                for e in range(2):
                    l_tile = jnp.where(lane_head == 2 * hp + e, state_cols[e], l_tile)
            emit(g, accs, l_tile)

    @pl.when(unshifted)
    def _():
        _dsw_unshifted_blocks(q_ref, kprev_ref, kcur_ref, vprev_ref, vcur_ref, ehot_ref,
                              accin_ref if has_state else None, lin_ref if has_state else None,
                              window, emit, step, g_blocks)


def _dsw_unshifted_blocks(q_ref, kprev_ref, kcur_ref, vprev_ref, vcur_ref, ehot_ref, accin_ref, lin_ref,
                          window, emit, step, g_blocks):
    has_state = accin_ref is not None

    qi = (lax.broadcasted_iota(jnp.int32, (2 * BLK, 2 * BLK), 0) & (BLK - 1)) + BLK
    kj = lax.broadcasted_iota(jnp.int32, (2 * BLK, 2 * BLK), 1)
    dist = qi - kj
    band = (dist >= 0) & (dist <= BLK)
    q_low = lax.broadcasted_iota(jnp.int32, (BLK, 2 * HEAD_DIM), 1) < HEAD_DIM
    v_low = lax.broadcasted_iota(jnp.int32, (2 * BLK, 2 * HEAD_DIM), 1) < HEAD_DIM

    for g in range(g_blocks):
        if g == 0:
            mask = band & (kj >= jnp.where(step == 0, BLK, 0))
        else:
            mask = band
        l_tile = lin_ref[g] if has_state else jnp.zeros((BLK, BLK), F32)
        accs = []
        for hp in range(N_GROUPS64 // 2):
            cols = slice(hp * 2 * HEAD_DIM, (hp + 1) * 2 * HEAD_DIM)
            q_pair = q_ref[g, :, cols]
            kk = window(kprev_ref, kcur_ref, g, cols)
            vv = window(vprev_ref, vcur_ref, g, cols)
            qz = jnp.zeros_like(q_pair)
            vz = jnp.zeros_like(vv)
            q2 = jnp.concatenate([jnp.where(q_low, q_pair, qz), jnp.where(q_low, qz, q_pair)], axis=0)
            sc = lax.dot_general(q2, kk, (((1,), (1,)), ((), ())), preferred_element_type=F32)
            p2 = jnp.where(mask, jnp.exp2(sc), 0.0).astype(BF16)
            p_cat = jnp.concatenate([p2[:BLK], p2[BLK:]], axis=1)
            v_split = jnp.concatenate([jnp.where(v_low, vv, vz), jnp.where(v_low, vz, vv)], axis=0)
            rhs = jnp.concatenate([v_split, ehot_ref[hp]], axis=1)
            out2 = jnp.dot(p_cat, rhs, preferred_element_type=F32)
            acc = out2[:, :2 * HEAD_DIM]
            if has_state:
                acc = acc + accin_ref[g, :, cols]
            l_tile = l_tile + out2[:, 2 * HEAD_DIM:]
            accs.append(acc)
        emit(g, accs, l_tile)


def _dsw_residue_of(hop):
    if hop == 0:
        return lambda g: g
    prev = _dsw_residue_of(hop - 1)
    prev_dil = DSW_PATTERNS[hop - 1][1]
    return lambda g: prev_dil * (g % DSW_DIL_RATIO) + prev(g // DSW_DIL_RATIO)


def _dsw_pattern(q, k, v, hop, state, ehot, bound):
    dil = DSW_PATTERNS[hop][1]
    last = hop == len(DSW_PATTERNS) - 1
    s = q.size // SPLIT
    nb = s // (BLK * dil)
    g_blocks = min(DSW_BLOCKS_PER_STEP, nb)
    ratio = DSW_DIL_RATIO
    residue = _dsw_residue_of(hop)
    view = lambda a: a.reshape(nb, BLK, dil * SPLIT)
    qkv_spec = pl.BlockSpec((g_blocks, BLK, SPLIT), lambda r, j, b: (j, 0, residue(r)))
    prev_spec = pl.BlockSpec((1, BLK, SPLIT), lambda r, j, b: (jnp.maximum(j * g_blocks - 1, 0), 0, residue(r)))
    state_spec = lambda width: pl.BlockSpec((g_blocks, BLK, width), lambda r, j, b: (j, 0, r))
    operands = [view(q), view(k), view(k), view(v), view(v), ehot]
    in_specs = [qkv_spec, prev_spec, qkv_spec, prev_spec, qkv_spec,
                pl.BlockSpec(ehot.shape, lambda r, j, b: (0, 0, 0))]
    if state is not None:
        operands += list(state)
        in_specs += [state_spec(SPLIT), state_spec(BLK)]
    scratch = []
    if last:
        own_spec = lambda width: pl.BlockSpec((g_blocks, BLK, width), lambda r, j, b: (j, 0, residue(r)))
        out_specs = [own_spec(SPLIT), own_spec(BLK)]
        out_shape = [jax.ShapeDtypeStruct((nb, BLK, dil * SPLIT), F32),
                     jax.ShapeDtypeStruct((nb, BLK, dil * BLK), F32)]
    else:
        assert DSW_PATTERNS[hop + 1][1] == dil * ratio and g_blocks % ratio == 0
        next_spec = lambda width: pl.BlockSpec((g_blocks // ratio, BLK, ratio * width), lambda r, j, b: (j, 0, r))
        out_specs = [next_spec(SPLIT), next_spec(BLK)]
        out_shape = [jax.ShapeDtypeStruct((nb // ratio, BLK, dil * ratio * SPLIT), F32),
                     jax.ShapeDtypeStruct((nb // ratio, BLK, dil * ratio * BLK), F32)]
        scratch.append(pltpu.VMEM((2, N_GROUPS64 // 2 + 1, BLK, 2 * HEAD_DIM), F32))
    out = pl.pallas_call(
        functools.partial(_dsw_hop_kernel, has_state=state is not None, last=last, blocks_per_step=g_blocks),
        grid_spec=pltpu.PrefetchScalarGridSpec(
            num_scalar_prefetch=1, grid=(dil, nb // g_blocks), in_specs=in_specs, out_specs=out_specs,
            scratch_shapes=scratch),
        out_shape=out_shape,
        compiler_params=_params("parallel", "arbitrary"),
        name=f"dsw_attention_dil{dil}",
    )(bound, *operands)
    return tuple(out)


def _dsw_constants():
    ehot = np.zeros((N_GROUPS64 // 2, 4 * BLK, BLK), np.float32)
    xexp = np.zeros((BLK, SPLIT), np.float32)
    for head in range(N_GROUPS64):
        hp, e = divmod(head, 2)
        lanes = slice(head * DSW_L_LANES_PER_HEAD, (head + 1) * DSW_L_LANES_PER_HEAD)
        ehot[hp, e * 2 * BLK:(e + 1) * 2 * BLK, lanes] = 1.0
        xexp[head * DSW_L_LANES_PER_HEAD, head * HEAD_DIM:(head + 1) * HEAD_DIM] = 1.0
    return jnp.asarray(ehot, BF16), jnp.asarray(xexp, BF16)


def _dsw_attention(q_views, k_views, v_views, score_bound):
    ehot, _ = _dsw_constants()
    bound = score_bound.reshape(1).astype(F32)
    state = None
    for hop in range(len(DSW_PATTERNS)):
        state = _dsw_pattern(q_views[hop], k_views[hop], v_views[hop], hop, state, ehot, bound)
    return state


def _diff_finalize(o0, o1, lam_ref, sg_ref, o_ref):
    lam_p = lam_ref[...]
    lam = (jnp.exp(jnp.sum(lam_p[0:1] * lam_p[1:2], axis=-1, keepdims=True))
           - jnp.exp(jnp.sum(lam_p[2:3] * lam_p[3:4], axis=-1, keepdims=True)) + LAM_INIT)
    od = o0 - lam * o1
    ms = jnp.mean(od * od, axis=0, keepdims=True)
    y = od * lax.rsqrt(ms + RMS_EPS) * sg_ref[...] * (1.0 - LAM_INIT)
    o_ref[...] = y.T


def _component_queries(qt):
    zeros = jnp.zeros((HEAD_DIM, qt.shape[1]), qt.dtype)
    return (jnp.concatenate([qt[:HEAD_DIM], zeros], axis=0), jnp.concatenate([zeros, qt[HEAD_DIM:]], axis=0))


def _causal_keep(i, k0, tq, tk):
    kpos = k0 + lax.broadcasted_iota(jnp.int32, (tk, tq), 0)
    qpos = i * tq + lax.broadcasted_iota(jnp.int32, (tk, tq), 1)
    return kpos <= qpos


def _diff_unshifted_kernel(qt_ref, k_ref, vt_ref, lam_ref, sg_ref, o_ref, l_sc, acc_sc, *, tq, tk):
    i = pl.program_id(1)
    qc = _component_queries(qt_ref[...])
    l_sc[...] = jnp.zeros(l_sc.shape, F32)
    acc_sc[...] = jnp.zeros(acc_sc.shape, F32)

    def full_chunk(j):
        k0 = pl.multiple_of(j * tk, tk)
        vt = vt_ref[:, pl.ds(k0, tk)]
        for c in range(2):
            st = jnp.dot(k_ref[pl.ds(k0, tk), :], qc[c], preferred_element_type=F32)
            p = jnp.exp2(st)
            l_sc[c] += jnp.sum(p.reshape(tk // SUBLANES, SUBLANES, tq), axis=0)
            acc_sc[c] += jnp.dot(vt, p.astype(BF16), preferred_element_type=F32)

    def chunk_quad(jq, carry):
        for u in range(4):
            full_chunk(4 * jq + u)
        return carry

    n_quads = lax.shift_right_logical(i, 2)
    lax.fori_loop(0, n_quads, chunk_quad, 0)

    @pl.when((i & 2) == 2)
    def _():
        full_chunk(4 * n_quads)
        full_chunk(4 * n_quads + 1)

    @pl.when((i & 1) == 1)
    def _():
        full_chunk(i - 1)

    sb = DIFF_DIAG_BLOCK
    k0 = pl.multiple_of(i * tk, tk)
    tri = lax.broadcasted_iota(jnp.int32, (sb, sb), 0) <= lax.broadcasted_iota(jnp.int32, (sb, sb), 1)
    for qb in range(tq // sb):
        nk = (qb + 1) * sb
        lanes = slice(qb * sb, (qb + 1) * sb)
        vt = vt_ref[:, pl.ds(k0, nk)]
        for c in range(2):
            st = jnp.dot(k_ref[pl.ds(k0, nk), :], qc[c][:, lanes], preferred_element_type=F32)
            p = jnp.exp2(st)
            p_tri = jnp.where(tri, p[qb * sb:, :], 0.0)
            p = jnp.concatenate([p[:qb * sb, :], p_tri], axis=0) if qb else p_tri
            l_sc[c, :, lanes] += jnp.sum(p.reshape(nk // SUBLANES, SUBLANES, sb), axis=0)
            acc_sc[c, :, lanes] += jnp.dot(vt, p.astype(BF16), preferred_element_type=F32)

    l0 = jnp.sum(l_sc[0], axis=0, keepdims=True)
    l1 = jnp.sum(l_sc[1], axis=0, keepdims=True)
    _diff_finalize(acc_sc[0] / l0, acc_sc[1] / l1, lam_ref, sg_ref, o_ref)


def _diff_kernel(qt_ref, k_ref, vt_ref, lam_ref, sg_ref, o_ref, m_sc, l_sc, acc_sc, *, tq, tk):
    i = pl.program_id(1)
    qc = _component_queries(qt_ref[...])
    m_sc[...] = jnp.full(m_sc.shape, NEG_INF, F32)
    l_sc[...] = jnp.zeros(l_sc.shape, F32)
    acc_sc[...] = jnp.zeros(acc_sc.shape, F32)

    def chunk(j, masked):
        k0 = pl.multiple_of(j * tk, tk)
        vt = vt_ref[:, pl.ds(k0, tk)]
        for c in range(2):
            st = jnp.dot(k_ref[pl.ds(k0, tk), :], qc[c], preferred_element_type=F32)
            if masked:
                st = jnp.where(_causal_keep(i, k0, tq, tk), st, NEG_INF)
            m_old = m_sc[c]
            m_new = jnp.maximum(m_old, jnp.max(st, axis=0, keepdims=True))
            alpha = jnp.exp2(m_old - m_new)
            p = jnp.exp2(st - m_new)
            l_sc[c] = alpha * l_sc[c] + jnp.sum(p, axis=0, keepdims=True)
            acc_sc[c] = alpha * acc_sc[c] + jnp.dot(vt, p.astype(BF16), preferred_element_type=F32)
            m_sc[c] = m_new

    n_full = (i * tq) // tk

    def full_chunk(j, carry):
        chunk(j, False)
        return carry

    lax.fori_loop(0, n_full, full_chunk, 0)
    for jj in range(tq // tk):
        chunk(n_full + jj, True)

    _diff_finalize(acc_sc[0] / l_sc[0], acc_sc[1] / l_sc[1], lam_ref, sg_ref, o_ref)


def _diff_dispatch_kernel(bound_ref, qt_ref, k_ref, vt_ref, lam_ref, sg_ref, o_ref, m_sc, l1_sc, l8_sc, acc_sc, *,
                          tq, tk):
    unshifted = bound_ref[0] <= DIFF_UNSHIFTED_SCORE_BOUND

    @pl.when(unshifted)
    def _():
        _diff_unshifted_kernel(qt_ref, k_ref, vt_ref, lam_ref, sg_ref, o_ref, l8_sc, acc_sc, tq=tq, tk=tk)

    @pl.when(jnp.logical_not(unshifted))
    def _():
        _diff_kernel(qt_ref, k_ref, vt_ref, lam_ref, sg_ref, o_ref, m_sc, l1_sc, acc_sc, tq=tq, tk=tk)


def _diff_attention(qd_t, kd_tok, vd_t, lam_params, subln_col, score_bound):
    s = qd_t.shape[1]
    tq, tk = DIFF_TQ, DIFF_TK
    assert tq == tk and s % tq == 0 and tq % DIFF_DIAG_BLOCK == 0
    grid_spec = pltpu.PrefetchScalarGridSpec(
        num_scalar_prefetch=1,
        grid=(DIFF_HEADS, s // tq),
        in_specs=[
            pl.BlockSpec((DIFF_V_DIM, tq), lambda h, i, bound: (h, i)),
            pl.BlockSpec((None, s, 2 * HEAD_DIM), lambda h, i, bound: (h, 0, 0)),
            pl.BlockSpec((DIFF_V_DIM, s), lambda h, i, bound: (h, 0)),
            pl.BlockSpec(lam_params.shape, lambda h, i, bound: (0, 0)),
            pl.BlockSpec(subln_col.shape, lambda h, i, bound: (0, 0)),
        ],
        out_specs=pl.BlockSpec((tq, DIFF_V_DIM), lambda h, i, bound: (i, h)),
        scratch_shapes=[pltpu.VMEM((2, 1, tq), F32), pltpu.VMEM((2, 1, tq), F32), pltpu.VMEM((2, SUBLANES, tq), F32),
                        pltpu.VMEM((2, DIFF_V_DIM, tq), F32)],
    )
    return pl.pallas_call(
        functools.partial(_diff_dispatch_kernel, tq=tq, tk=tk),
        grid_spec=grid_spec,
        out_shape=jax.ShapeDtypeStruct((s, DIFF_HEADS * DIFF_V_DIM), F32),
        compiler_params=_params("parallel", "arbitrary"),
        name="diff_attention",
    )(score_bound.reshape(1).astype(F32), qd_t, kd_tok, vd_t, lam_params, subln_col)


def _att_out_compute(x_ref, acc_ref, l_ref, xexp_ref, od_ref, g_ref, w_ref, tok_sc, dil):
    n_lane_tiles = SPLIT // LANES
    rows = tok_sc.shape[1] // dil
    for r in range(dil):
        dst = pl.ds(r, rows, stride=dil)
        for c in range(n_lane_tiles):
            col0 = r * SPLIT + c * LANES
            tok_sc[c, dst, :] = acc_ref[0, :, col0:col0 + LANES]
        tok_sc[n_lane_tiles, dst, :] = l_ref[0, :, r * LANES:(r + 1) * LANES]
    acc = jnp.concatenate([tok_sc[c] for c in range(n_lane_tiles)], axis=1)
    r = 1.0 / tok_sc[n_lane_tiles]
    r_hi = r.astype(BF16)
    r_lo = (r - r_hi.astype(F32)).astype(BF16)
    r_exp = (jnp.dot(r_hi, xexp_ref[...], preferred_element_type=F32)
             + jnp.dot(r_lo, xexp_ref[...], preferred_element_type=F32))
    g = g_ref[...]
    y = jnp.concatenate([acc * r_exp, od_ref[...]], axis=-1) * (g * jax.nn.sigmoid(g))
    return x_ref[...] + jnp.dot(y.astype(BF16), w_ref[...], preferred_element_type=F32)


def _gelu(x):
    return 0.5 * x * (1.0 + lax.erf(x * (2.0 ** -0.5)))


def _sgu_compute(x, ng_ref, win_ref, lng_ref, lnb_ref, ws_ref, bs_ref, wout_ref, width):
    tm = x.shape[0]
    ms = jnp.mean(x * x, axis=-1, keepdims=True)
    h = (x * lax.rsqrt(ms + RMS_EPS) * ng_ref[...]).astype(BF16)
    v_raw = jnp.dot(h, win_ref[:, width:2 * width], preferred_element_type=F32)
    u_raw = jnp.dot(h, win_ref[:, 0:width], preferred_element_type=F32)
    v = _gelu(v_raw)
    mu = jnp.mean(v, axis=-1, keepdims=True)
    vc = v - mu
    vn = vc * lax.rsqrt(jnp.mean(vc * vc, axis=-1, keepdims=True) + LN_EPS)
    vn = (vn * lng_ref[...] + lnb_ref[...]).astype(BF16)
    gw = width // SGU_GROUPS
    row = lax.broadcasted_iota(jnp.int32, (SGU_CHUNK, SGU_CHUNK), 0)
    col = lax.broadcasted_iota(jnp.int32, (SGU_CHUNK, SGU_CHUNK), 1)
    causal = col <= row
    sp_groups = []
    for grp in range(SGU_GROUPS):
        ws = jnp.where(causal, ws_ref[grp], 0.0).astype(BF16)
        bias = bs_ref[grp]
        chunks = []
        for c in range(tm // SGU_CHUNK):
            vg = vn[c * SGU_CHUNK:(c + 1) * SGU_CHUNK, grp * gw:(grp + 1) * gw]
            chunks.append(jnp.dot(ws, vg, preferred_element_type=F32) + bias)
        sp_groups.append(jnp.concatenate(chunks, axis=0))
    sp = jnp.concatenate(sp_groups, axis=1)
    g = jnp.dot(h, win_ref[:, 2 * width:3 * width], preferred_element_type=F32)
    u = _gelu(u_raw)
    y = u * sp * (g * jax.nn.sigmoid(g))
    return x + jnp.dot(y.astype(BF16), wout_ref[...], preferred_element_type=F32)


def _att_out_sgu_kernel(x_ref, acc_ref, l_ref, xexp_ref, od_ref, g_ref, watt_ref,
                        ng_ref, win_ref, lng_ref, lnb_ref, ws_ref, bs_ref, wout_ref, o_ref, tok_sc, *, dil, width):
    x1 = _att_out_compute(x_ref, acc_ref, l_ref, xexp_ref, od_ref, g_ref, watt_ref, tok_sc, dil)
    o_ref[...] = _sgu_compute(x1, ng_ref, win_ref, lng_ref, lnb_ref, ws_ref, bs_ref, wout_ref, width)


def _att_out_sgu(x2d, acc_view, l_view, od, g, w_att_out, norm_g, w_in, ln_g, ln_b, w_s, b_s, w_out):
    assert w_att_out.dtype == w_in.dtype == w_out.dtype == BF16
    s, d = x2d.shape
    width = w_out.shape[0]
    tm = ROW_TILE
    dil = DSW_PATTERNS[-1][1]
    rows = tm // dil
    per_blk = BLK // rows
    assert acc_view.shape == (s // (BLK * dil), BLK, dil * SPLIT) and tm % dil == 0 and BLK % rows == 0
    assert l_view.shape == (s // (BLK * dil), BLK, dil * LANES)
    assert tm % SGU_CHUNK == 0 and rows % 8 == 0
    _, xexp = _dsw_constants()
    row = lambda w: pl.BlockSpec((tm, w), lambda i: (i, 0))
    view_spec = lambda w: pl.BlockSpec((1, rows, dil * w), lambda i: (i // per_blk, i % per_blk, 0))
    single = lambda shape: pl.BlockSpec(shape, lambda *_: (0,) * len(shape), pipeline_mode=pl.Buffered(1))
    return pl.pallas_call(
        functools.partial(_att_out_sgu_kernel, dil=dil, width=width),
        grid=(s // tm,),
        in_specs=[row(d), view_spec(SPLIT), view_spec(LANES), _const_spec(xexp.shape), row(SPLIT), row(2 * SPLIT),
                  single(w_att_out.shape),
                  _const_spec((1, d)), single(w_in.shape), _const_spec((1, width)), _const_spec((1, width)),
                  _const_spec(w_s.shape), _const_spec((SGU_GROUPS, SGU_CHUNK, 1)), single(w_out.shape)],
        out_specs=row(d),
        out_shape=jax.ShapeDtypeStruct((s, d), F32),
        scratch_shapes=[pltpu.VMEM((SPLIT // LANES + 1, tm, LANES), F32)],
        compiler_params=_params("parallel"),
        name="att_out_sgu",
    )(x2d, acc_view, l_view, xexp, od, g, w_att_out,
      norm_g.reshape(1, d), w_in, ln_g.reshape(1, width), ln_b.reshape(1, width),
      w_s, b_s.reshape(SGU_GROUPS, SGU_CHUNK, 1), w_out)


def _rope_tables_fm(seq_len):
    pos = jnp.arange(seq_len, dtype=F32)
    inv = 1.0 / (ROPE_THETA ** (jnp.arange(0, 2 * ROT_HALF, 2, dtype=F32) / (2 * ROT_HALF)))
    ang = inv[:, None] * pos[None, :]
    return jnp.cos(ang), jnp.sin(ang)


def kernel(x, att_norm, att_w_in, dsw_q_norm, dsw_k_norm, diff_q_norm, diff_k_norm, diff_lam_q1, diff_lam_k1,
           diff_lam_q2, diff_lam_k2, diff_subln, att_w_out, sgu_norm, sgu_w_in, sgu_ln_g, sgu_ln_b, sgu_w_s,
           sgu_b_s, sgu_w_out):
    b, s, d = x.shape
    assert b == 1 and s % (BLK * DSW_PATTERNS[-1][1]) == 0
    x2d = x.reshape(s, d)
    cos_t, sin_t = _rope_tables_fm(s)

    gains = jnp.stack([dsw_q_norm[0], dsw_k_norm[0], diff_q_norm[0], diff_k_norm[0]])
    gains = jnp.broadcast_to(gains[:, None, :, None], (4, N_GROUPS64, HEAD_DIM, 1))
    qa_views, ka_views, va_views, g, kd_tok, qd_t, vd_t, later_bf16 = _att_in(
        x2d, att_norm[0], att_w_in[0], gains, cos_t, sin_t, [att_w_out[0], sgu_w_in[0], sgu_w_out[0]])
    att_w_out_bf16, sgu_w_in_bf16, sgu_w_out_bf16 = later_bf16
    bound = lambda gq, gk: 1.02 * LOG2_E * HEAD_DIM ** 0.5 * jnp.max(jnp.abs(gq)) * jnp.max(jnp.abs(gk))
    oa_acc, oa_l = _dsw_attention(qa_views, ka_views, va_views, bound(dsw_q_norm[0], dsw_k_norm[0]))
    lam_params = jnp.stack([diff_lam_q1[0], diff_lam_k1[0], diff_lam_q2[0], diff_lam_k2[0]])
    od = _diff_attention(qd_t, kd_tok, vd_t, lam_params, diff_subln[0].reshape(DIFF_V_DIM, 1),
                         bound(diff_q_norm[0], diff_k_norm[0]))
    x2d = _att_out_sgu(x2d, oa_acc, oa_l, od, g, att_w_out_bf16,
                       sgu_norm[0], sgu_w_in_bf16, sgu_ln_g[0], sgu_ln_b[0], sgu_w_s[0], sgu_b_s[0], sgu_w_out_bf16)
    return x2d.reshape(b, s, d)
```

```python
import functools
import math

import numpy as np
import jax
import jax.numpy as jnp
from jax import lax
from jax.experimental import pallas as pl
from jax.experimental.pallas import tpu as pltpu

F32 = jnp.float32
BF16 = jnp.bfloat16

LANES = 128
SUBLANES = 8
HEAD_DIM = 64
ROT_HALF = 8
ROPE_THETA = 500000.0
BLK = 128
N_GROUPS64 = 8
SPLIT = N_GROUPS64 * HEAD_DIM
DSW_PATTERNS = ((128, 1), (512, 4), (2048, 16))
DSW_DIL_RATIO = 4
DIFF_HEADS = 4
DIFF_V_DIM = 128
SGU_GROUPS = 8
SGU_CHUNK = 128
RMS_EPS = 1e-6
LN_EPS = 1e-5
NEG_INF = -1e30
LAM_INIT = 0.8 - 0.6 * math.exp(-0.3 * 0)
LOG2_E = math.log2(math.e)
DIFF_UNSHIFTED_SCORE_BOUND = 96.0
DSW_UNSHIFTED_SCORE_BOUND = 96.0

V7X_VMEM_LIMIT_BYTES = 56 * 1024 * 1024

ROW_TILE = 512
ATT_IN_PARTS = 2
DSW_BLOCKS_PER_STEP = 8
DIFF_TQ = 1024
DIFF_TK = 1024
DIFF_DIAG_BLOCK = 512


def _params(*sem):
    return pltpu.CompilerParams(dimension_semantics=sem, vmem_limit_bytes=V7X_VMEM_LIMIT_BYTES)


def _const_spec(shape):
    nd = len(shape)
    return pl.BlockSpec(shape, lambda *_: (0,) * nd)


def _norm_rope_fm(p, gain, cos, sin, scale):
    t = p.shape[1]
    p3 = p.reshape(N_GROUPS64, HEAD_DIM, t)
    ms = jnp.mean(p3 * p3, axis=1, keepdims=True)
    y = p3 * lax.rsqrt(ms + RMS_EPS) * gain
    x1 = y[:, 0:ROT_HALF, :]
    x2 = y[:, ROT_HALF:2 * ROT_HALF, :]
    out = jnp.concatenate([x1 * cos - x2 * sin, x2 * cos + x1 * sin, y[:, 2 * ROT_HALF:, :]], axis=1)
    if scale != 1.0:
        out = out * scale
    return out.reshape(SPLIT, t)


def _emit_pattern_views(tok, tok_sc, view_refs, part, n_parts):
    t = tok.shape[0]
    n_lane_tiles = SPLIT // LANES
    for c in range(n_lane_tiles):
        tok_sc[c] = tok[:, c * LANES:(c + 1) * LANES]
    for (_, dil), ref in zip(DSW_PATTERNS, view_refs):
        if dil == 1:
            ref[part * t:(part + 1) * t, :] = tok.astype(BF16)
            continue
        n = t // dil
        for r in range(dil):
            rows = jnp.concatenate([tok_sc[c, pl.ds(r, n, stride=dil), :] for c in range(n_lane_tiles)], axis=1)
            ref[0, part * n:(part + 1) * n, r * SPLIT:(r + 1) * SPLIT] = rows.astype(BF16)


def _att_in_kernel(x_ref, ng_ref, wtok_ref, wfm_ref, gains_ref, cos_ref, sin_ref, *refs, n_cast):
    n_pat = len(DSW_PATTERNS)
    cast_in, refs = refs[:n_cast], refs[n_cast:]
    qa_views, ka_views, va_views = refs[0:n_pat], refs[n_pat:2 * n_pat], refs[2 * n_pat:3 * n_pat]
    g_ref, kd_ref, qd_ref, vd_ref = refs[3 * n_pat:3 * n_pat + 4]
    cast_out = refs[3 * n_pat + 4:3 * n_pat + 4 + n_cast]
    tok_sc = refs[-1]
    for src, dst in zip(cast_in, cast_out):
        dst[...] = src[...].astype(BF16)
    scales = (HEAD_DIM ** -0.5 * LOG2_E, 1.0, HEAD_DIM ** -0.5 * LOG2_E, 1.0)
    n_parts = ATT_IN_PARTS
    t = x_ref.shape[0] // n_parts
    for part in range(n_parts):
        rows = slice(part * t, (part + 1) * t)
        x = x_ref[rows, :]
        ms = jnp.mean(x * x, axis=-1, keepdims=True)
        h = (x * lax.rsqrt(ms + RMS_EPS) * ng_ref[...]).astype(BF16)
        tok = jnp.dot(h, wtok_ref[...], preferred_element_type=F32)
        _emit_pattern_views(tok[:, :SPLIT], tok_sc.at[part, 0], va_views, part, n_parts)
        g_ref[rows, :] = tok[:, SPLIT:]
        cos = cos_ref[:, rows][None]
        sin = sin_ref[:, rows][None]
        def feature_major(idx):
            return lax.dot_general(wfm_ref[idx * SPLIT:(idx + 1) * SPLIT, :], h, (((1,), (1,)), ((), ())),
                                   preferred_element_type=F32)

        def normed(idx):
            return _norm_rope_fm(feature_major(idx), gains_ref[idx], cos, sin, scales[idx])

        _emit_pattern_views(normed(0).T, tok_sc.at[part, 1], qa_views, part, n_parts)
        _emit_pattern_views(normed(1).T, tok_sc.at[part, 2], ka_views, part, n_parts)
        qd_ref[:, rows] = normed(2).astype(BF16)
        kd = normed(3).T.astype(BF16)
        for head in range(DIFF_HEADS):
            kd_ref[head, rows, :] = kd[:, head * 2 * HEAD_DIM:(head + 1) * 2 * HEAD_DIM]
        vd_ref[:, rows] = feature_major(4).astype(BF16)


def _att_in(x2d, norm_g, w_in, gains, cos_t, sin_t, later_weights):
    s, d = x2d.shape
    tm = ROW_TILE
    n_steps = s // tm
    cast_specs = []
    for w_later in later_weights:
        rows = w_later.shape[0] // n_steps
        assert w_later.shape[0] % n_steps == 0 and rows % 16 == 0
        cast_specs.append(pl.BlockSpec((rows, w_later.shape[1]), lambda i: (i, 0)))
    w = w_in.astype(BF16)
    w_tok = jnp.concatenate([w[:, 2 * SPLIT:4 * SPLIT], w[:, 7 * SPLIT:8 * SPLIT]], axis=1)
    w_fm = jnp.concatenate([w[:, 0:2 * SPLIT], w[:, 4 * SPLIT:7 * SPLIT]], axis=1).T
    view_shapes, view_specs = [], []
    for _, dil in DSW_PATTERNS:
        if dil == 1:
            view_shapes.append(jax.ShapeDtypeStruct((s, SPLIT), BF16))
            view_specs.append(pl.BlockSpec((tm, SPLIT), lambda i: (i, 0)))
            continue
        rows = tm // dil
        assert tm % dil == 0 and BLK % rows == 0 and rows % (16 * ATT_IN_PARTS) == 0
        per_blk = BLK // rows
        view_shapes.append(jax.ShapeDtypeStruct((s // (BLK * dil), BLK, dil * SPLIT), BF16))
        view_specs.append(pl.BlockSpec((1, rows, dil * SPLIT),
                                       functools.partial(lambda i, per_blk: (i // per_blk, i % per_blk, 0),
                                                         per_blk=per_blk)))
    fm = jax.ShapeDtypeStruct((SPLIT, s), BF16)
    fm_spec = pl.BlockSpec((SPLIT, tm), lambda i: (0, i))
    outs = pl.pallas_call(
        functools.partial(_att_in_kernel, n_cast=len(later_weights)),
        grid=(n_steps,),
        in_specs=[
            pl.BlockSpec((tm, d), lambda i: (i, 0)),
            _const_spec((1, d)),
            _const_spec(w_tok.shape),
            _const_spec(w_fm.shape),
            _const_spec(gains.shape),
            pl.BlockSpec((ROT_HALF, tm), lambda i: (0, i)),
            pl.BlockSpec((ROT_HALF, tm), lambda i: (0, i)),
        ] + cast_specs,
        out_specs=view_specs * 3 + [
            pl.BlockSpec((tm, 2 * SPLIT), lambda i: (i, 0)),
            pl.BlockSpec((DIFF_HEADS, tm, 2 * HEAD_DIM), lambda i: (0, i, 0)),
            fm_spec, fm_spec,
        ] + cast_specs,
        out_shape=view_shapes * 3 + [
            jax.ShapeDtypeStruct((s, 2 * SPLIT), F32),
            jax.ShapeDtypeStruct((DIFF_HEADS, s, 2 * HEAD_DIM), BF16),
            fm, fm,
        ] + [jax.ShapeDtypeStruct(w_later.shape, BF16) for w_later in later_weights],
        scratch_shapes=[pltpu.VMEM((ATT_IN_PARTS, 3, SPLIT // LANES, tm // ATT_IN_PARTS, LANES), F32)],
        compiler_params=_params("parallel"),
        name="att_in_proj",
    )(x2d, norm_g.reshape(1, d), w_tok, w_fm, gains, cos_t, sin_t, *later_weights)
    n_pat = len(DSW_PATTERNS)
    return ((outs[0:n_pat], outs[n_pat:2 * n_pat], outs[2 * n_pat:3 * n_pat]) + tuple(outs[3 * n_pat:3 * n_pat + 4])
            + (tuple(outs[3 * n_pat + 4:]),))


DSW_L_LANES_PER_HEAD = BLK // N_GROUPS64


def _dsw_online_pair(q_pair, kk, vv, mask, acc_in, l_tile_in, hp, last):
    low_half = lax.broadcasted_iota(jnp.int32, (BLK, 2 * HEAD_DIM), 1) < HEAD_DIM
    halves, state_cols = [], []
    for e in range(2):
        head = 2 * hp + e
        sel = low_half if e == 0 else jnp.logical_not(low_half)
        qm = jnp.where(sel, q_pair, jnp.zeros_like(q_pair))
        sc = lax.dot_general(qm, kk, (((1,), (1,)), ((), ())), preferred_element_type=F32)
        sc = jnp.where(mask, sc, NEG_INF)
        m_new = jnp.max(sc, axis=-1, keepdims=True)
        if acc_in is not None:
            m_old = l_tile_in[:, head * DSW_L_LANES_PER_HEAD:head * DSW_L_LANES_PER_HEAD + 1]
            m_new = jnp.maximum(m_old, m_new)
            alpha = jnp.exp2(m_old - m_new)
        p = jnp.exp2(sc - m_new)
        l = jnp.sum(p, axis=-1, keepdims=True)
        pv = jnp.dot(p.astype(BF16), vv, preferred_element_type=F32)
        if acc_in is not None:
            l = l + alpha
            pv = pv + alpha * acc_in
        halves.append(pv / l)
        state_cols.append(jnp.ones_like(l) if last else m_new + jnp.log2(l))
    return jnp.where(low_half, halves[0], halves[1]), state_cols


def _dsw_hop_kernel(bound_ref, *refs, has_state, last, blocks_per_step):
    refs = list(refs)
    q_ref, kprev_ref, kcur_ref, vprev_ref, vcur_ref, ehot_ref = refs[:6]
    pos = 6
    if has_state:
        accin_ref, lin_ref = refs[pos:pos + 2]
        pos += 2
    acc_ref, l_ref = refs[pos:pos + 2]
    perm_sc = None if last else refs[pos + 2]
    g_blocks = blocks_per_step
    step = pl.program_id(1)

    def window(prev_ref, cur_ref, g, cols):
        before = prev_ref[0, :, cols] if g == 0 else cur_ref[g - 1, :, cols]
        return jnp.concatenate([before, cur_ref[g, :, cols]], axis=0)

    def emit(g, accs, l_tile):
        n_pairs = N_GROUPS64 // 2
        if last:
            for hp in range(n_pairs):
                acc_ref[g, :, hp * 2 * HEAD_DIM:(hp + 1) * 2 * HEAD_DIM] = accs[hp]
            l_ref[g] = l_tile
            return
        slot = perm_sc.at[g % 2]
        for hp in range(n_pairs):
            slot[hp] = accs[hp]
        slot[n_pairs] = l_tile
        ratio = DSW_DIL_RATIO
        rows = BLK // ratio
        out_blk = g // ratio
        row_dst = slice((g % ratio) * rows, (g % ratio + 1) * rows)
        for m in range(ratio):
            src = pl.ds(m, rows, stride=ratio)
            for hp in range(n_pairs):
                col0 = m * SPLIT + hp * 2 * HEAD_DIM
                acc_ref[out_blk, row_dst, col0:col0 + 2 * HEAD_DIM] = slot[hp, src, :]
            l_ref[out_blk, row_dst, m * BLK:(m + 1) * BLK] = slot[n_pairs, src, :]

    unshifted = bound_ref[0] <= DSW_UNSHIFTED_SCORE_BOUND

    @pl.when(jnp.logical_not(unshifted))
    def _():
        qi = lax.broadcasted_iota(jnp.int32, (BLK, 2 * BLK), 0) + BLK
        kj = lax.broadcasted_iota(jnp.int32, (BLK, 2 * BLK), 1)
        dist = qi - kj
        band = (dist >= 0) & (dist <= BLK)
        lane_head = lax.broadcasted_iota(jnp.int32, (BLK, BLK), 1) // DSW_L_LANES_PER_HEAD
        for g in range(g_blocks):
            mask = band & (kj >= jnp.where(step == 0, BLK, 0)) if g == 0 else band
            l_in = lin_ref[g] if has_state else None
            accs = []
            l_tile = jnp.zeros((BLK, BLK), F32)
            for hp in range(N_GROUPS64 // 2):
                cols = slice(hp * 2 * HEAD_DIM, (hp + 1) * 2 * HEAD_DIM)
                acc, state_cols = _dsw_online_pair(
                    q_ref[g, :, cols], window(kprev_ref, kcur_ref, g, cols), window(vprev_ref, vcur_ref, g, cols),
                    mask, accin_ref[g, :, cols] if has_state else None, l_in, hp, last)
                accs.append(acc)
                for e in range(2):
                    l_tile = jnp.where(lane_head == 2 * hp + e, state_cols[e], l_tile)
            emit(g, accs, l_tile)

    @pl.when(unshifted)
    def _():
        _dsw_unshifted_blocks(q_ref, kprev_ref, kcur_ref, vprev_ref, vcur_ref, ehot_ref,
                              accin_ref if has_state else None, lin_ref if has_state else None,
                              window, emit, step, g_blocks)


def _dsw_unshifted_blocks(q_ref, kprev_ref, kcur_ref, vprev_ref, vcur_ref, ehot_ref, accin_ref, lin_ref,
                          window, emit, step, g_blocks):
    has_state = accin_ref is not None

    qi = (lax.broadcasted_iota(jnp.int32, (2 * BLK, 2 * BLK), 0) & (BLK - 1)) + BLK
    kj = lax.broadcasted_iota(jnp.int32, (2 * BLK, 2 * BLK), 1)
    dist = qi - kj
    band = (dist >= 0) & (dist <= BLK)
    q_low = lax.broadcasted_iota(jnp.int32, (BLK, 2 * HEAD_DIM), 1) < HEAD_DIM
    v_low = lax.broadcasted_iota(jnp.int32, (2 * BLK, 2 * HEAD_DIM), 1) < HEAD_DIM

    for g in range(g_blocks):
        if g == 0:
            mask = band & (kj >= jnp.where(step == 0, BLK, 0))
        else:
            mask = band
        l_tile = lin_ref[g] if has_state else jnp.zeros((BLK, BLK), F32)
        accs = []
        for hp in range(N_GROUPS64 // 2):
            cols = slice(hp * 2 * HEAD_DIM, (hp + 1) * 2 * HEAD_DIM)
            q_pair = q_ref[g, :, cols]
            kk = window(kprev_ref, kcur_ref, g, cols)
            vv = window(vprev_ref, vcur_ref, g, cols)
            qz = jnp.zeros_like(q_pair)
            vz = jnp.zeros_like(vv)
            q2 = jnp.concatenate([jnp.where(q_low, q_pair, qz), jnp.where(q_low, qz, q_pair)], axis=0)
            sc = lax.dot_general(q2, kk, (((1,), (1,)), ((), ())), preferred_element_type=F32)
            p2 = jnp.where(mask, jnp.exp2(sc), 0.0).astype(BF16)
            p_cat = jnp.concatenate([p2[:BLK], p2[BLK:]], axis=1)
            v_split = jnp.concatenate([jnp.where(v_low, vv, vz), jnp.where(v_low, vz, vv)], axis=0)
            rhs = jnp.concatenate([v_split, ehot_ref[hp]], axis=1)
            out2 = jnp.dot(p_cat, rhs, preferred_element_type=F32)
            acc = out2[:, :2 * HEAD_DIM]
            if has_state:
                acc = acc + accin_ref[g, :, cols]
            l_tile = l_tile + out2[:, 2 * HEAD_DIM:]
            accs.append(acc)
        emit(g, accs, l_tile)


def _dsw_residue_of(hop):
    if hop == 0:
        return lambda g: g
    prev = _dsw_residue_of(hop - 1)
    prev_dil = DSW_PATTERNS[hop - 1][1]
    return lambda g: prev_dil * (g % DSW_DIL_RATIO) + prev(g // DSW_DIL_RATIO)


def _dsw_pattern(q, k, v, hop, state, ehot, bound):
    dil = DSW_PATTERNS[hop][1]
    last = hop == len(DSW_PATTERNS) - 1
    s = q.size // SPLIT
    nb = s // (BLK * dil)
    g_blocks = min(DSW_BLOCKS_PER_STEP, nb)
    ratio = DSW_DIL_RATIO
    residue = _dsw_residue_of(hop)
    view = lambda a: a.reshape(nb, BLK, dil * SPLIT)
    qkv_spec = pl.BlockSpec((g_blocks, BLK, SPLIT), lambda r, j, b: (j, 0, residue(r)))
    prev_spec = pl.BlockSpec((1, BLK, SPLIT), lambda r, j, b: (jnp.maximum(j * g_blocks - 1, 0), 0, residue(r)))
    state_spec = lambda width: pl.BlockSpec((g_blocks, BLK, width), lambda r, j, b: (j, 0, r))
    operands = [view(q), view(k), view(k), view(v), view(v), ehot]
    in_specs = [qkv_spec, prev_spec, qkv_spec, prev_spec, qkv_spec,
                pl.BlockSpec(ehot.shape, lambda r, j, b: (0, 0, 0))]
    if state is not None:
        operands += list(state)
        in_specs += [state_spec(SPLIT), state_spec(BLK)]
    scratch = []
    if last:
        own_spec = lambda width: pl.BlockSpec((g_blocks, BLK, width), lambda r, j, b: (j, 0, residue(r)))
        out_specs = [own_spec(SPLIT), own_spec(BLK)]
        out_shape = [jax.ShapeDtypeStruct((nb, BLK, dil * SPLIT), F32),
                     jax.ShapeDtypeStruct((nb, BLK, dil * BLK), F32)]
    else:
        assert DSW_PATTERNS[hop + 1][1] == dil * ratio and g_blocks % ratio == 0
        next_spec = lambda width: pl.BlockSpec((g_blocks // ratio, BLK, ratio * width), lambda r, j, b: (j, 0, r))
        out_specs = [next_spec(SPLIT), next_spec(BLK)]
        out_shape = [jax.ShapeDtypeStruct((nb // ratio, BLK, dil * ratio * SPLIT), F32),
                     jax.ShapeDtypeStruct((nb // ratio, BLK, dil * ratio * BLK), F32)]
        scratch.append(pltpu.VMEM((2, N_GROUPS64 // 2 + 1, BLK, 2 * HEAD_DIM), F32))
    out = pl.pallas_call(
        functools.partial(_dsw_hop_kernel, has_state=state is not None, last=last, blocks_per_step=g_blocks),
        grid_spec=pltpu.PrefetchScalarGridSpec(
            num_scalar_prefetch=1, grid=(dil, nb // g_blocks), in_specs=in_specs, out_specs=out_specs,
            scratch_shapes=scratch),
        out_shape=out_shape,
        compiler_params=_params("parallel", "arbitrary"),
        name=f"dsw_attention_dil{dil}",
    )(bound, *operands)
    return tuple(out)


def _dsw_constants():
    ehot = np.zeros((N_GROUPS64 // 2, 4 * BLK, BLK), np.float32)
    xexp = np.zeros((BLK, SPLIT), np.float32)
    for head in range(N_GROUPS64):
        hp, e = divmod(head, 2)
        lanes = slice(head * DSW_L_LANES_PER_HEAD, (head + 1) * DSW_L_LANES_PER_HEAD)
        ehot[hp, e * 2 * BLK:(e + 1) * 2 * BLK, lanes] = 1.0
        xexp[head * DSW_L_LANES_PER_HEAD, head * HEAD_DIM:(head + 1) * HEAD_DIM] = 1.0
    return jnp.asarray(ehot, BF16), jnp.asarray(xexp, BF16)


def _dsw_attention(q_views, k_views, v_views, score_bound):
    ehot, _ = _dsw_constants()
    bound = score_bound.reshape(1).astype(F32)
    state = None
    for hop in range(len(DSW_PATTERNS)):
        state = _dsw_pattern(q_views[hop], k_views[hop], v_views[hop], hop, state, ehot, bound)
    return state


def _diff_finalize(o0, o1, lam_ref, sg_ref, o_ref):
    lam_p = lam_ref[...]
    lam = (jnp.exp(jnp.sum(lam_p[0:1] * lam_p[1:2], axis=-1, keepdims=True))
           - jnp.exp(jnp.sum(lam_p[2:3] * lam_p[3:4], axis=-1, keepdims=True)) + LAM_INIT)
    od = o0 - lam * o1
    ms = jnp.mean(od * od, axis=0, keepdims=True)
    y = od * lax.rsqrt(ms + RMS_EPS) * sg_ref[...] * (1.0 - LAM_INIT)
    o_ref[...] = y.T


def _component_queries(qt):
    zeros = jnp.zeros((HEAD_DIM, qt.shape[1]), qt.dtype)
    return (jnp.concatenate([qt[:HEAD_DIM], zeros], axis=0), jnp.concatenate([zeros, qt[HEAD_DIM:]], axis=0))


def _causal_keep(i, k0, tq, tk):
    kpos = k0 + lax.broadcasted_iota(jnp.int32, (tk, tq), 0)
    qpos = i * tq + lax.broadcasted_iota(jnp.int32, (tk, tq), 1)
    return kpos <= qpos


def _diff_unshifted_kernel(qt_ref, k_ref, vt_ref, lam_ref, sg_ref, o_ref, l_sc, acc_sc, *, tq, tk):
    i = pl.program_id(1)
    qc = _component_queries(qt_ref[...])
    l_sc[...] = jnp.zeros(l_sc.shape, F32)
    acc_sc[...] = jnp.zeros(acc_sc.shape, F32)

    def full_chunk(j):
        k0 = pl.multiple_of(j * tk, tk)
        vt = vt_ref[:, pl.ds(k0, tk)]
        for c in range(2):
            st = jnp.dot(k_ref[pl.ds(k0, tk), :], qc[c], preferred_element_type=F32)
            p = jnp.exp2(st)
            l_sc[c] += jnp.sum(p.reshape(tk // SUBLANES, SUBLANES, tq), axis=0)
            acc_sc[c] += jnp.dot(vt, p.astype(BF16), preferred_element_type=F32)

    def chunk_quad(jq, carry):
        for u in range(4):
            full_chunk(4 * jq + u)
        return carry

    n_quads = lax.shift_right_logical(i, 2)
    lax.fori_loop(0, n_quads, chunk_quad, 0)

    @pl.when((i & 2) == 2)
    def _():
        full_chunk(4 * n_quads)
        full_chunk(4 * n_quads + 1)

    @pl.when((i & 1) == 1)
    def _():
        full_chunk(i - 1)

    sb = DIFF_DIAG_BLOCK
    k0 = pl.multiple_of(i * tk, tk)
    tri = lax.broadcasted_iota(jnp.int32, (sb, sb), 0) <= lax.broadcasted_iota(jnp.int32, (sb, sb), 1)
    for qb in range(tq // sb):
        nk = (qb + 1) * sb
        lanes = slice(qb * sb, (qb + 1) * sb)
        vt = vt_ref[:, pl.ds(k0, nk)]
        for c in range(2):
            st = jnp.dot(k_ref[pl.ds(k0, nk), :], qc[c][:, lanes], preferred_element_type=F32)
            p = jnp.exp2(st)
            p_tri = jnp.where(tri, p[qb * sb:, :], 0.0)
            p = jnp.concatenate([p[:qb * sb, :], p_tri], axis=0) if qb else p_tri
            l_sc[c, :, lanes] += jnp.sum(p.reshape(nk // SUBLANES, SUBLANES, sb), axis=0)
            acc_sc[c, :, lanes] += jnp.dot(vt, p.astype(BF16), preferred_element_type=F32)

    l0 = jnp.sum(l_sc[0], axis=0, keepdims=True)
    l1 = jnp.sum(l_sc[1], axis=0, keepdims=True)
    _diff_finalize(acc_sc[0] / l0, acc_sc[1] / l1, lam_ref, sg_ref, o_ref)


def _diff_kernel(qt_ref, k_ref, vt_ref, lam_ref, sg_ref, o_ref, m_sc, l_sc, acc_sc, *, tq, tk):
    i = pl.program_id(1)
    qc = _component_queries(qt_ref[...])
    m_sc[...] = jnp.full(m_sc.shape, NEG_INF, F32)
    l_sc[...] = jnp.zeros(l_sc.shape, F32)
    acc_sc[...] = jnp.zeros(acc_sc.shape, F32)

    def chunk(j, masked):
        k0 = pl.multiple_of(j * tk, tk)
        vt = vt_ref[:, pl.ds(k0, tk)]
        for c in range(2):
            st = jnp.dot(k_ref[pl.ds(k0, tk), :], qc[c], preferred_element_type=F32)
            if masked:
                st = jnp.where(_causal_keep(i, k0, tq, tk), st, NEG_INF)
            m_old = m_sc[c]
            m_new = jnp.maximum(m_old, jnp.max(st, axis=0, keepdims=True))
            alpha = jnp.exp2(m_old - m_new)
            p = jnp.exp2(st - m_new)
            l_sc[c] = alpha * l_sc[c] + jnp.sum(p, axis=0, keepdims=True)
            acc_sc[c] = alpha * acc_sc[c] + jnp.dot(vt, p.astype(BF16), preferred_element_type=F32)
            m_sc[c] = m_new

    n_full = (i * tq) // tk

    def full_chunk(j, carry):
        chunk(j, False)
        return carry

    lax.fori_loop(0, n_full, full_chunk, 0)
    for jj in range(tq // tk):
        chunk(n_full + jj, True)

    _diff_finalize(acc_sc[0] / l_sc[0], acc_sc[1] / l_sc[1], lam_ref, sg_ref, o_ref)


def _diff_dispatch_kernel(bound_ref, qt_ref, k_ref, vt_ref, lam_ref, sg_ref, o_ref, m_sc, l1_sc, l8_sc, acc_sc, *,
                          tq, tk):
    unshifted = bound_ref[0] <= DIFF_UNSHIFTED_SCORE_BOUND

    @pl.when(unshifted)
    def _():
        _diff_unshifted_kernel(qt_ref, k_ref, vt_ref, lam_ref, sg_ref, o_ref, l8_sc, acc_sc, tq=tq, tk=tk)

    @pl.when(jnp.logical_not(unshifted))
    def _():
        _diff_kernel(qt_ref, k_ref, vt_ref, lam_ref, sg_ref, o_ref, m_sc, l1_sc, acc_sc, tq=tq, tk=tk)


def _diff_attention(qd_t, kd_tok, vd_t, lam_params, subln_col, score_bound):
    s = qd_t.shape[1]
    tq, tk = DIFF_TQ, DIFF_TK
    assert tq == tk and s % tq == 0 and tq % DIFF_DIAG_BLOCK == 0
    grid_spec = pltpu.PrefetchScalarGridSpec(
        num_scalar_prefetch=1,
        grid=(DIFF_HEADS, s // tq),
        in_specs=[
            pl.BlockSpec((DIFF_V_DIM, tq), lambda h, i, bound: (h, i)),
            pl.BlockSpec((None, s, 2 * HEAD_DIM), lambda h, i, bound: (h, 0, 0)),
            pl.BlockSpec((DIFF_V_DIM, s), lambda h, i, bound: (h, 0)),
            pl.BlockSpec(lam_params.shape, lambda h, i, bound: (0, 0)),
            pl.BlockSpec(subln_col.shape, lambda h, i, bound: (0, 0)),
        ],
        out_specs=pl.BlockSpec((tq, DIFF_V_DIM), lambda h, i, bound: (i, h)),
        scratch_shapes=[pltpu.VMEM((2, 1, tq), F32), pltpu.VMEM((2, 1, tq), F32), pltpu.VMEM((2, SUBLANES, tq), F32),
                        pltpu.VMEM((2, DIFF_V_DIM, tq), F32)],
    )
    return pl.pallas_call(
        functools.partial(_diff_dispatch_kernel, tq=tq, tk=tk),
        grid_spec=grid_spec,
        out_shape=jax.ShapeDtypeStruct((s, DIFF_HEADS * DIFF_V_DIM), F32),
        compiler_params=_params("parallel", "arbitrary"),
        name="diff_attention",
    )(score_bound.reshape(1).astype(F32), qd_t, kd_tok, vd_t, lam_params, subln_col)


def _att_out_compute(x_ref, acc_ref, l_ref, xexp_ref, od_ref, g_ref, w_ref, tok_sc, dil):
    n_lane_tiles = SPLIT // LANES
    rows = tok_sc.shape[1] // dil
    for r in range(dil):
        dst = pl.ds(r, rows, stride=dil)
        for c in range(n_lane_tiles):
            col0 = r * SPLIT + c * LANES
            tok_sc[c, dst, :] = acc_ref[0, :, col0:col0 + LANES]
        tok_sc[n_lane_tiles, dst, :] = l_ref[0, :, r * LANES:(r + 1) * LANES]
    acc = jnp.concatenate([tok_sc[c] for c in range(n_lane_tiles)], axis=1)
    r = 1.0 / tok_sc[n_lane_tiles]
    r_hi = r.astype(BF16)
    r_lo = (r - r_hi.astype(F32)).astype(BF16)
    r_exp = (jnp.dot(r_hi, xexp_ref[...], preferred_element_type=F32)
             + jnp.dot(r_lo, xexp_ref[...], preferred_element_type=F32))
    g = g_ref[...]
    y = jnp.concatenate([acc * r_exp, od_ref[...]], axis=-1) * (g * jax.nn.sigmoid(g))
    return x_ref[...] + jnp.dot(y.astype(BF16), w_ref[...], preferred_element_type=F32)


def _gelu(x):
    return 0.5 * x * (1.0 + lax.erf(x * (2.0 ** -0.5)))


def _sgu_compute(x, ng_ref, win_ref, lng_ref, lnb_ref, ws_ref, bs_ref, wout_ref, width):
    tm = x.shape[0]
    ms = jnp.mean(x * x, axis=-1, keepdims=True)
    h = (x * lax.rsqrt(ms + RMS_EPS) * ng_ref[...]).astype(BF16)
    v_raw = jnp.dot(h, win_ref[:, width:2 * width], preferred_element_type=F32)
    u_raw = jnp.dot(h, win_ref[:, 0:width], preferred_element_type=F32)
    v = _gelu(v_raw)
    mu = jnp.mean(v, axis=-1, keepdims=True)
    vc = v - mu
    vn = vc * lax.rsqrt(jnp.mean(vc * vc, axis=-1, keepdims=True) + LN_EPS)
    vn = (vn * lng_ref[...] + lnb_ref[...]).astype(BF16)
    gw = width // SGU_GROUPS
    row = lax.broadcasted_iota(jnp.int32, (SGU_CHUNK, SGU_CHUNK), 0)
    col = lax.broadcasted_iota(jnp.int32, (SGU_CHUNK, SGU_CHUNK), 1)
    causal = col <= row
    sp_groups = []
    for grp in range(SGU_GROUPS):
        ws = jnp.where(causal, ws_ref[grp], 0.0).astype(BF16)
        bias = bs_ref[grp]
        chunks = []
        for c in range(tm // SGU_CHUNK):
            vg = vn[c * SGU_CHUNK:(c + 1) * SGU_CHUNK, grp * gw:(grp + 1) * gw]
            chunks.append(jnp.dot(ws, vg, preferred_element_type=F32) + bias)
        sp_groups.append(jnp.concatenate(chunks, axis=0))
    sp = jnp.concatenate(sp_groups, axis=1)
    g = jnp.dot(h, win_ref[:, 2 * width:3 * width], preferred_element_type=F32)
    u = _gelu(u_raw)
    y = u * sp * (g * jax.nn.sigmoid(g))
    return x + jnp.dot(y.astype(BF16), wout_ref[...], preferred_element_type=F32)


def _att_out_sgu_kernel(x_ref, acc_ref, l_ref, xexp_ref, od_ref, g_ref, watt_ref,
                        ng_ref, win_ref, lng_ref, lnb_ref, ws_ref, bs_ref, wout_ref, o_ref, tok_sc, *, dil, width):
    x1 = _att_out_compute(x_ref, acc_ref, l_ref, xexp_ref, od_ref, g_ref, watt_ref, tok_sc, dil)
    o_ref[...] = _sgu_compute(x1, ng_ref, win_ref, lng_ref, lnb_ref, ws_ref, bs_ref, wout_ref, width)


def _att_out_sgu(x2d, acc_view, l_view, od, g, w_att_out, norm_g, w_in, ln_g, ln_b, w_s, b_s, w_out):
    assert w_att_out.dtype == w_in.dtype == w_out.dtype == BF16
    s, d = x2d.shape
    width = w_out.shape[0]
    tm = ROW_TILE
    dil = DSW_PATTERNS[-1][1]
    rows = tm // dil
    per_blk = BLK // rows
    assert acc_view.shape == (s // (BLK * dil), BLK, dil * SPLIT) and tm % dil == 0 and BLK % rows == 0
    assert l_view.shape == (s // (BLK * dil), BLK, dil * LANES)
    assert tm % SGU_CHUNK == 0 and rows % 8 == 0
    _, xexp = _dsw_constants()
    row = lambda w: pl.BlockSpec((tm, w), lambda i: (i, 0))
    view_spec = lambda w: pl.BlockSpec((1, rows, dil * w), lambda i: (i // per_blk, i % per_blk, 0))
    single = lambda shape: pl.BlockSpec(shape, lambda *_: (0,) * len(shape), pipeline_mode=pl.Buffered(1))
    return pl.pallas_call(
        functools.partial(_att_out_sgu_kernel, dil=dil, width=width),
        grid=(s // tm,),
        in_specs=[row(d), view_spec(SPLIT), view_spec(LANES), _const_spec(xexp.shape), row(SPLIT), row(2 * SPLIT),
                  single(w_att_out.shape),
                  _const_spec((1, d)), single(w_in.shape), _const_spec((1, width)), _const_spec((1, width)),
                  _const_spec(w_s.shape), _const_spec((SGU_GROUPS, SGU_CHUNK, 1)), single(w_out.shape)],
        out_specs=row(d),
        out_shape=jax.ShapeDtypeStruct((s, d), F32),
        scratch_shapes=[pltpu.VMEM((SPLIT // LANES + 1, tm, LANES), F32)],
        compiler_params=_params("parallel"),
        name="att_out_sgu",
    )(x2d, acc_view, l_view, xexp, od, g, w_att_out,
      norm_g.reshape(1, d), w_in, ln_g.reshape(1, width), ln_b.reshape(1, width),
      w_s, b_s.reshape(SGU_GROUPS, SGU_CHUNK, 1), w_out)


def _rope_tables_fm(seq_len):
    pos = jnp.arange(seq_len, dtype=F32)
    inv = 1.0 / (ROPE_THETA ** (jnp.arange(0, 2 * ROT_HALF, 2, dtype=F32) / (2 * ROT_HALF)))
    ang = inv[:, None] * pos[None, :]
    return jnp.cos(ang), jnp.sin(ang)


def kernel(x, att_norm, att_w_in, dsw_q_norm, dsw_k_norm, diff_q_norm, diff_k_norm, diff_lam_q1, diff_lam_k1,
           diff_lam_q2, diff_lam_k2, diff_subln, att_w_out, sgu_norm, sgu_w_in, sgu_ln_g, sgu_ln_b, sgu_w_s,
           sgu_b_s, sgu_w_out):
    b, s, d = x.shape
    assert b == 1 and s % (BLK * DSW_PATTERNS[-1][1]) == 0
    x2d = x.reshape(s, d)
    cos_t, sin_t = _rope_tables_fm(s)

    gains = jnp.stack([dsw_q_norm[0], dsw_k_norm[0], diff_q_norm[0], diff_k_norm[0]])
    gains = jnp.broadcast_to(gains[:, None, :, None], (4, N_GROUPS64, HEAD_DIM, 1))
    qa_views, ka_views, va_views, g, kd_tok, qd_t, vd_t, later_bf16 = _att_in(
        x2d, att_norm[0], att_w_in[0], gains, cos_t, sin_t, [att_w_out[0], sgu_w_in[0], sgu_w_out[0]])
    att_w_out_bf16, sgu_w_in_bf16, sgu_w_out_bf16 = later_bf16
    bound = lambda gq, gk: 1.02 * LOG2_E * HEAD_DIM ** 0.5 * jnp.max(jnp.abs(gq)) * jnp.max(jnp.abs(gk))
    oa_acc, oa_l = _dsw_attention(qa_views, ka_views, va_views, bound(dsw_q_norm[0], dsw_k_norm[0]))
    oa_l, qd_t = lax.optimization_barrier((oa_l, qd_t))
    lam_params = jnp.stack([diff_lam_q1[0], diff_lam_k1[0], diff_lam_q2[0], diff_lam_k2[0]])
    od = _diff_attention(qd_t, kd_tok, vd_t, lam_params, diff_subln[0].reshape(DIFF_V_DIM, 1),
                         bound(diff_q_norm[0], diff_k_norm[0]))
    x2d = _att_out_sgu(x2d, oa_acc, oa_l, od, g, att_w_out_bf16,
                       sgu_norm[0], sgu_w_in_bf16, sgu_ln_g[0], sgu_ln_b[0], sgu_w_s[0], sgu_b_s[0], sgu_w_out_bf16)
    return x2d.reshape(b, s, d)
```

```python
import functools
import math

import numpy as np
import jax
import jax.numpy as jnp
from jax import lax
from jax.experimental import pallas as pl
from jax.experimental.pallas import tpu as pltpu

F32 = jnp.float32
BF16 = jnp.bfloat16

LANES = 128
SUBLANES = 8
HEAD_DIM = 64
ROT_HALF = 8
ROPE_THETA = 500000.0
BLK = 128
N_GROUPS64 = 8
SPLIT = N_GROUPS64 * HEAD_DIM
ATT_TOKEN_MAJOR_SPLITS = (2, 3, 7)
ATT_FEATURE_MAJOR_SPLITS = (0, 1, 4, 5, 6)
DSW_PATTERNS = ((128, 1), (512, 4), (2048, 16))
DSW_DIL_RATIO = 4
DIFF_HEADS = 4
DIFF_V_DIM = 128
SGU_GROUPS = 8
SGU_CHUNK = 128
RMS_EPS = 1e-6
LN_EPS = 1e-5
NEG_INF = -1e30
LAM_INIT = 0.8 - 0.6 * math.exp(-0.3 * 0)
LOG2_E = math.log2(math.e)
DIFF_UNSHIFTED_SCORE_BOUND = 96.0
DSW_UNSHIFTED_SCORE_BOUND = 96.0

V7X_VMEM_LIMIT_BYTES = 56 * 1024 * 1024

ROW_TILE = 512
ATT_IN_PARTS = 2
DSW_BLOCKS_PER_STEP = 8
DIFF_TQ = 1024
DIFF_TK = 1024
DIFF_DIAG_BLOCK = 512


def _params(*sem):
    return pltpu.CompilerParams(dimension_semantics=sem, vmem_limit_bytes=V7X_VMEM_LIMIT_BYTES)


def _const_spec(shape):
    nd = len(shape)
    return pl.BlockSpec(shape, lambda *_: (0,) * nd)


def _norm_rope_fm(p, gain, cos, sin, scale):
    t = p.shape[1]
    p3 = p.reshape(N_GROUPS64, HEAD_DIM, t)
    ms = jnp.mean(p3 * p3, axis=1, keepdims=True)
    y = p3 * lax.rsqrt(ms + RMS_EPS) * gain
    x1 = y[:, 0:ROT_HALF, :]
    x2 = y[:, ROT_HALF:2 * ROT_HALF, :]
    out = jnp.concatenate([x1 * cos - x2 * sin, x2 * cos + x1 * sin, y[:, 2 * ROT_HALF:, :]], axis=1)
    if scale != 1.0:
        out = out * scale
    return out.reshape(SPLIT, t)


def _emit_pattern_views(tok, tok_sc, view_refs, part, n_parts):
    t = tok.shape[0]
    n_lane_tiles = SPLIT // LANES
    for c in range(n_lane_tiles):
        tok_sc[c] = tok[:, c * LANES:(c + 1) * LANES]
    for (_, dil), ref in zip(DSW_PATTERNS, view_refs):
        if dil == 1:
            ref[part * t:(part + 1) * t, :] = tok.astype(BF16)
            continue
        n = t // dil
        for r in range(dil):
            rows = jnp.concatenate([tok_sc[c, pl.ds(r, n, stride=dil), :] for c in range(n_lane_tiles)], axis=1)
            ref[0, part * n:(part + 1) * n, r * SPLIT:(r + 1) * SPLIT] = rows.astype(BF16)


def _att_in_kernel(x_ref, ng_ref, w_ref, gains_ref, cos_ref, sin_ref, *refs, n_cast):
    n_pat = len(DSW_PATTERNS)
    cast_in, refs = refs[:n_cast], refs[n_cast:]
    qa_views, ka_views, va_views = refs[0:n_pat], refs[n_pat:2 * n_pat], refs[2 * n_pat:3 * n_pat]
    g_ref, kd_ref, qd_ref, vd_ref = refs[3 * n_pat:3 * n_pat + 4]
    cast_out = refs[3 * n_pat + 4:3 * n_pat + 4 + n_cast]
    tok_sc, wtok_ref, wfm_ref = refs[-3:]

    @pl.when(pl.program_id(0) == 0)
    def _():
        for dst, src in enumerate(ATT_TOKEN_MAJOR_SPLITS):
            wtok_ref[:, dst * SPLIT:(dst + 1) * SPLIT] = w_ref[:, src * SPLIT:(src + 1) * SPLIT].astype(BF16)
        for dst, src in enumerate(ATT_FEATURE_MAJOR_SPLITS):
            wfm_ref[dst * SPLIT:(dst + 1) * SPLIT, :] = w_ref[:, src * SPLIT:(src + 1) * SPLIT].T.astype(BF16)

    for src, dst in zip(cast_in, cast_out):
        dst[...] = src[...].astype(BF16)
    scales = (HEAD_DIM ** -0.5 * LOG2_E, 1.0, HEAD_DIM ** -0.5 * LOG2_E, 1.0)
    n_parts = ATT_IN_PARTS
    t = x_ref.shape[0] // n_parts
    for part in range(n_parts):
        rows = slice(part * t, (part + 1) * t)
        x = x_ref[rows, :]
        ms = jnp.mean(x * x, axis=-1, keepdims=True)
        h = (x * lax.rsqrt(ms + RMS_EPS) * ng_ref[...]).astype(BF16)
        tok = jnp.dot(h, wtok_ref[...], preferred_element_type=F32)
        _emit_pattern_views(tok[:, :SPLIT], tok_sc.at[part, 0], va_views, part, n_parts)
        g_ref[rows, :] = tok[:, SPLIT:]
        cos = cos_ref[:, rows][None]
        sin = sin_ref[:, rows][None]
        def feature_major(idx):
            return lax.dot_general(wfm_ref[idx * SPLIT:(idx + 1) * SPLIT, :], h, (((1,), (1,)), ((), ())),
                                   preferred_element_type=F32)

        def normed(idx):
            return _norm_rope_fm(feature_major(idx), gains_ref[idx], cos, sin, scales[idx])

        _emit_pattern_views(normed(0).T, tok_sc.at[part, 1], qa_views, part, n_parts)
        _emit_pattern_views(normed(1).T, tok_sc.at[part, 2], ka_views, part, n_parts)
        qd_ref[:, rows] = normed(2).astype(BF16)
        kd = normed(3).T.astype(BF16)
        for head in range(DIFF_HEADS):
            kd_ref[head, rows, :] = kd[:, head * 2 * HEAD_DIM:(head + 1) * 2 * HEAD_DIM]
        vd_ref[:, rows] = feature_major(4).astype(BF16)


def _att_in(x2d, norm_g, w_in, gains, cos_t, sin_t, later_weights):
    s, d = x2d.shape
    tm = ROW_TILE
    n_steps = s // tm
    cast_specs = []
    for w_later in later_weights:
        rows = w_later.shape[0] // n_steps
        assert w_later.shape[0] % n_steps == 0 and rows % 16 == 0
        cast_specs.append(pl.BlockSpec((rows, w_later.shape[1]), lambda i: (i, 0)))
    view_shapes, view_specs = [], []
    for _, dil in DSW_PATTERNS:
        if dil == 1:
            view_shapes.append(jax.ShapeDtypeStruct((s, SPLIT), BF16))
            view_specs.append(pl.BlockSpec((tm, SPLIT), lambda i: (i, 0)))
            continue
        rows = tm // dil
        assert tm % dil == 0 and BLK % rows == 0 and rows % (16 * ATT_IN_PARTS) == 0
        per_blk = BLK // rows
        view_shapes.append(jax.ShapeDtypeStruct((s // (BLK * dil), BLK, dil * SPLIT), BF16))
        view_specs.append(pl.BlockSpec((1, rows, dil * SPLIT),
                                       functools.partial(lambda i, per_blk: (i // per_blk, i % per_blk, 0),
                                                         per_blk=per_blk)))
    fm = jax.ShapeDtypeStruct((SPLIT, s), BF16)
    fm_spec = pl.BlockSpec((SPLIT, tm), lambda i: (0, i))
    outs = pl.pallas_call(
        functools.partial(_att_in_kernel, n_cast=len(later_weights)),
        grid=(n_steps,),
        in_specs=[
            pl.BlockSpec((tm, d), lambda i: (i, 0)),
            _const_spec((1, d)),
            pl.BlockSpec(w_in.shape, lambda i: (0, 0), pipeline_mode=pl.Buffered(1)),
            _const_spec(gains.shape),
            pl.BlockSpec((ROT_HALF, tm), lambda i: (0, i)),
            pl.BlockSpec((ROT_HALF, tm), lambda i: (0, i)),
        ] + cast_specs,
        out_specs=view_specs * 3 + [
            pl.BlockSpec((tm, 2 * SPLIT), lambda i: (i, 0)),
            pl.BlockSpec((DIFF_HEADS, tm, 2 * HEAD_DIM), lambda i: (0, i, 0)),
            fm_spec, fm_spec,
        ] + cast_specs,
        out_shape=view_shapes * 3 + [
            jax.ShapeDtypeStruct((s, 2 * SPLIT), F32),
            jax.ShapeDtypeStruct((DIFF_HEADS, s, 2 * HEAD_DIM), BF16),
            fm, fm,
        ] + [jax.ShapeDtypeStruct(w_later.shape, BF16) for w_later in later_weights],
        scratch_shapes=[pltpu.VMEM((ATT_IN_PARTS, 3, SPLIT // LANES, tm // ATT_IN_PARTS, LANES), F32),
                        pltpu.VMEM((d, len(ATT_TOKEN_MAJOR_SPLITS) * SPLIT), BF16),
                        pltpu.VMEM((len(ATT_FEATURE_MAJOR_SPLITS) * SPLIT, d), BF16)],
        compiler_params=_params("arbitrary"),
        name="att_in_proj",
    )(x2d, norm_g.reshape(1, d), w_in, gains, cos_t, sin_t, *later_weights)
    n_pat = len(DSW_PATTERNS)
    return ((outs[0:n_pat], outs[n_pat:2 * n_pat], outs[2 * n_pat:3 * n_pat]) + tuple(outs[3 * n_pat:3 * n_pat + 4])
            + (tuple(outs[3 * n_pat + 4:]),))


DSW_L_LANES_PER_HEAD = BLK // N_GROUPS64


def _dsw_online_pair(q_pair, kk, vv, mask, acc_in, l_tile_in, hp, last):
    low_half = lax.broadcasted_iota(jnp.int32, (BLK, 2 * HEAD_DIM), 1) < HEAD_DIM
    halves, state_cols = [], []
    for e in range(2):
        head = 2 * hp + e
        sel = low_half if e == 0 else jnp.logical_not(low_half)
        qm = jnp.where(sel, q_pair, jnp.zeros_like(q_pair))
        sc = lax.dot_general(qm, kk, (((1,), (1,)), ((), ())), preferred_element_type=F32)
        sc = jnp.where(mask, sc, NEG_INF)
        m_new = jnp.max(sc, axis=-1, keepdims=True)
        if acc_in is not None:
            m_old = l_tile_in[:, head * DSW_L_LANES_PER_HEAD:head * DSW_L_LANES_PER_HEAD + 1]
            m_new = jnp.maximum(m_old, m_new)
            alpha = jnp.exp2(m_old - m_new)
        p = jnp.exp2(sc - m_new)
        l = jnp.sum(p, axis=-1, keepdims=True)
        pv = jnp.dot(p.astype(BF16), vv, preferred_element_type=F32)
        if acc_in is not None:
            l = l + alpha
            pv = pv + alpha * acc_in
        halves.append(pv / l)
        state_cols.append(jnp.ones_like(l) if last else m_new + jnp.log2(l))
    return jnp.where(low_half, halves[0], halves[1]), state_cols


def _dsw_hop_kernel(bound_ref, *refs, has_state, last, blocks_per_step):
    refs = list(refs)
    q_ref, kprev_ref, kcur_ref, vprev_ref, vcur_ref, ehot_ref = refs[:6]
    pos = 6
    if has_state:
        accin_ref, lin_ref = refs[pos:pos + 2]
        pos += 2
    acc_ref, l_ref = refs[pos:pos + 2]
    perm_sc = None if last else refs[pos + 2]
    g_blocks = blocks_per_step
    step = pl.program_id(1)

    def window(prev_ref, cur_ref, g, cols):
        before = prev_ref[0, :, cols] if g == 0 else cur_ref[g - 1, :, cols]
        return jnp.concatenate([before, cur_ref[g, :, cols]], axis=0)

    def emit(g, accs, l_tile):
        n_pairs = N_GROUPS64 // 2
        if last:
            for hp in range(n_pairs):
                acc_ref[g, :, hp * 2 * HEAD_DIM:(hp + 1) * 2 * HEAD_DIM] = accs[hp]
            l_ref[g] = l_tile
            return
        slot = perm_sc.at[g % 2]
        for hp in range(n_pairs):
            slot[hp] = accs[hp]
        slot[n_pairs] = l_tile
        ratio = DSW_DIL_RATIO
        rows = BLK // ratio
        out_blk = g // ratio
        row_dst = slice((g % ratio) * rows, (g % ratio + 1) * rows)
        for m in range(ratio):
            src = pl.ds(m, rows, stride=ratio)
            for hp in range(n_pairs):
                col0 = m * SPLIT + hp * 2 * HEAD_DIM
                acc_ref[out_blk, row_dst, col0:col0 + 2 * HEAD_DIM] = slot[hp, src, :]
            l_ref[out_blk, row_dst, m * BLK:(m + 1) * BLK] = slot[n_pairs, src, :]

    unshifted = bound_ref[0] <= DSW_UNSHIFTED_SCORE_BOUND

    @pl.when(jnp.logical_not(unshifted))
    def _():
        qi = lax.broadcasted_iota(jnp.int32, (BLK, 2 * BLK), 0) + BLK
        kj = lax.broadcasted_iota(jnp.int32, (BLK, 2 * BLK), 1)
        dist = qi - kj
        band = (dist >= 0) & (dist <= BLK)
        lane_head = lax.broadcasted_iota(jnp.int32, (BLK, BLK), 1) // DSW_L_LANES_PER_HEAD
        for g in range(g_blocks):
            mask = band & (kj >= jnp.where(step == 0, BLK, 0)) if g == 0 else band
            l_in = lin_ref[g] if has_state else None
            accs = []
            l_tile = jnp.zeros((BLK, BLK), F32)
            for hp in range(N_GROUPS64 // 2):
                cols = slice(hp * 2 * HEAD_DIM, (hp + 1) * 2 * HEAD_DIM)
                acc, state_cols = _dsw_online_pair(
                    q_ref[g, :, cols], window(kprev_ref, kcur_ref, g, cols), window(vprev_ref, vcur_ref, g, cols),
                    mask, accin_ref[g, :, cols] if has_state else None, l_in, hp, last)
                accs.append(acc)
                for e in range(2):
                    l_tile = jnp.where(lane_head == 2 * hp + e, state_cols[e], l_tile)
            emit(g, accs, l_tile)

    @pl.when(unshifted)
    def _():
        _dsw_unshifted_blocks(q_ref, kprev_ref, kcur_ref, vprev_ref, vcur_ref, ehot_ref,
                              accin_ref if has_state else None, lin_ref if has_state else None,
                              window, emit, step, g_blocks)


def _dsw_unshifted_blocks(q_ref, kprev_ref, kcur_ref, vprev_ref, vcur_ref, ehot_ref, accin_ref, lin_ref,
                          window, emit, step, g_blocks):
    has_state = accin_ref is not None

    qi = (lax.broadcasted_iota(jnp.int32, (2 * BLK, 2 * BLK), 0) & (BLK - 1)) + BLK
    kj = lax.broadcasted_iota(jnp.int32, (2 * BLK, 2 * BLK), 1)
    dist = qi - kj
    band = (dist >= 0) & (dist <= BLK)
    q_low = lax.broadcasted_iota(jnp.int32, (BLK, 2 * HEAD_DIM), 1) < HEAD_DIM
    v_low = lax.broadcasted_iota(jnp.int32, (2 * BLK, 2 * HEAD_DIM), 1) < HEAD_DIM

    for g in range(g_blocks):
        if g == 0:
            mask = band & (kj >= jnp.where(step == 0, BLK, 0))
        else:
            mask = band
        l_tile = lin_ref[g] if has_state else jnp.zeros((BLK, BLK), F32)
        accs = []
        for hp in range(N_GROUPS64 // 2):
            cols = slice(hp * 2 * HEAD_DIM, (hp + 1) * 2 * HEAD_DIM)
            q_pair = q_ref[g, :, cols]
            kk = window(kprev_ref, kcur_ref, g, cols)
            vv = window(vprev_ref, vcur_ref, g, cols)
            qz = jnp.zeros_like(q_pair)
            vz = jnp.zeros_like(vv)
            q2 = jnp.concatenate([jnp.where(q_low, q_pair, qz), jnp.where(q_low, qz, q_pair)], axis=0)
            sc = lax.dot_general(q2, kk, (((1,), (1,)), ((), ())), preferred_element_type=F32)
            p2 = jnp.where(mask, jnp.exp2(sc), 0.0).astype(BF16)
            p_cat = jnp.concatenate([p2[:BLK], p2[BLK:]], axis=1)
            v_split = jnp.concatenate([jnp.where(v_low, vv, vz), jnp.where(v_low, vz, vv)], axis=0)
            rhs = jnp.concatenate([v_split, ehot_ref[hp]], axis=1)
            out2 = jnp.dot(p_cat, rhs, preferred_element_type=F32)
            acc = out2[:, :2 * HEAD_DIM]
            if has_state:
                acc = acc + accin_ref[g, :, cols]
            l_tile = l_tile + out2[:, 2 * HEAD_DIM:]
            accs.append(acc)
        emit(g, accs, l_tile)


def _dsw_residue_of(hop):
    if hop == 0:
        return lambda g: g
    prev = _dsw_residue_of(hop - 1)
    prev_dil = DSW_PATTERNS[hop - 1][1]
    return lambda g: prev_dil * (g % DSW_DIL_RATIO) + prev(g // DSW_DIL_RATIO)


def _dsw_pattern(q, k, v, hop, state, ehot, bound):
    dil = DSW_PATTERNS[hop][1]
    last = hop == len(DSW_PATTERNS) - 1
    s = q.size // SPLIT
    nb = s // (BLK * dil)
    g_blocks = min(DSW_BLOCKS_PER_STEP, nb)
    ratio = DSW_DIL_RATIO
    residue = _dsw_residue_of(hop)
    view = lambda a: a.reshape(nb, BLK, dil * SPLIT)
    qkv_spec = pl.BlockSpec((g_blocks, BLK, SPLIT), lambda r, j, b: (j, 0, residue(r)))
    prev_spec = pl.BlockSpec((1, BLK, SPLIT), lambda r, j, b: (jnp.maximum(j * g_blocks - 1, 0), 0, residue(r)))
    state_spec = lambda width: pl.BlockSpec((g_blocks, BLK, width), lambda r, j, b: (j, 0, r))
    operands = [view(q), view(k), view(k), view(v), view(v), ehot]
    in_specs = [qkv_spec, prev_spec, qkv_spec, prev_spec, qkv_spec,
                pl.BlockSpec(ehot.shape, lambda r, j, b: (0, 0, 0))]
    if state is not None:
        operands += list(state)
        in_specs += [state_spec(SPLIT), state_spec(BLK)]
    scratch = []
    if last:
        own_spec = lambda width: pl.BlockSpec((g_blocks, BLK, width), lambda r, j, b: (j, 0, residue(r)))
        out_specs = [own_spec(SPLIT), own_spec(BLK)]
        out_shape = [jax.ShapeDtypeStruct((nb, BLK, dil * SPLIT), F32),
                     jax.ShapeDtypeStruct((nb, BLK, dil * BLK), F32)]
    else:
        assert DSW_PATTERNS[hop + 1][1] == dil * ratio and g_blocks % ratio == 0
        next_spec = lambda width: pl.BlockSpec((g_blocks // ratio, BLK, ratio * width), lambda r, j, b: (j, 0, r))
        out_specs = [next_spec(SPLIT), next_spec(BLK)]
        out_shape = [jax.ShapeDtypeStruct((nb // ratio, BLK, dil * ratio * SPLIT), F32),
                     jax.ShapeDtypeStruct((nb // ratio, BLK, dil * ratio * BLK), F32)]
        scratch.append(pltpu.VMEM((2, N_GROUPS64 // 2 + 1, BLK, 2 * HEAD_DIM), F32))
    out = pl.pallas_call(
        functools.partial(_dsw_hop_kernel, has_state=state is not None, last=last, blocks_per_step=g_blocks),
        grid_spec=pltpu.PrefetchScalarGridSpec(
            num_scalar_prefetch=1, grid=(dil, nb // g_blocks), in_specs=in_specs, out_specs=out_specs,
            scratch_shapes=scratch),
        out_shape=out_shape,
        compiler_params=_params("parallel", "arbitrary"),
        name=f"dsw_attention_dil{dil}",
    )(bound, *operands)
    return tuple(out)


def _dsw_constants():
    ehot = np.zeros((N_GROUPS64 // 2, 4 * BLK, BLK), np.float32)
    xexp = np.zeros((BLK, SPLIT), np.float32)
    for head in range(N_GROUPS64):
        hp, e = divmod(head, 2)
        lanes = slice(head * DSW_L_LANES_PER_HEAD, (head + 1) * DSW_L_LANES_PER_HEAD)
        ehot[hp, e * 2 * BLK:(e + 1) * 2 * BLK, lanes] = 1.0
        xexp[head * DSW_L_LANES_PER_HEAD, head * HEAD_DIM:(head + 1) * HEAD_DIM] = 1.0
    return jnp.asarray(ehot, BF16), jnp.asarray(xexp, BF16)


def _dsw_attention(q_views, k_views, v_views, score_bound):
    ehot, _ = _dsw_constants()
    bound = score_bound.reshape(1).astype(F32)
    state = None
    for hop in range(len(DSW_PATTERNS)):
        state = _dsw_pattern(q_views[hop], k_views[hop], v_views[hop], hop, state, ehot, bound)
    return state


def _diff_finalize(o0, o1, lam_ref, sg_ref, o_ref):
    lam_p = lam_ref[...]
    lam = (jnp.exp(jnp.sum(lam_p[0:1] * lam_p[1:2], axis=-1, keepdims=True))
           - jnp.exp(jnp.sum(lam_p[2:3] * lam_p[3:4], axis=-1, keepdims=True)) + LAM_INIT)
    od = o0 - lam * o1
    ms = jnp.mean(od * od, axis=0, keepdims=True)
    y = od * lax.rsqrt(ms + RMS_EPS) * sg_ref[...] * (1.0 - LAM_INIT)
    o_ref[...] = y.T


def _component_queries(qt):
    zeros = jnp.zeros((HEAD_DIM, qt.shape[1]), qt.dtype)
    return (jnp.concatenate([qt[:HEAD_DIM], zeros], axis=0), jnp.concatenate([zeros, qt[HEAD_DIM:]], axis=0))


def _causal_keep(i, k0, tq, tk):
    kpos = k0 + lax.broadcasted_iota(jnp.int32, (tk, tq), 0)
    qpos = i * tq + lax.broadcasted_iota(jnp.int32, (tk, tq), 1)
    return kpos <= qpos


def _diff_unshifted_kernel(qt_ref, k_ref, vt_ref, lam_ref, sg_ref, o_ref, l_sc, acc_sc, *, tq, tk):
    i = pl.program_id(1)
    qc = _component_queries(qt_ref[...])
    l_sc[...] = jnp.zeros(l_sc.shape, F32)
    acc_sc[...] = jnp.zeros(acc_sc.shape, F32)

    def full_chunk(j):
        k0 = pl.multiple_of(j * tk, tk)
        vt = vt_ref[:, pl.ds(k0, tk)]
        for c in range(2):
            st = jnp.dot(k_ref[pl.ds(k0, tk), :], qc[c], preferred_element_type=F32)
            p = jnp.exp2(st)
            l_sc[c] += jnp.sum(p.reshape(tk // SUBLANES, SUBLANES, tq), axis=0)
            acc_sc[c] += jnp.dot(vt, p.astype(BF16), preferred_element_type=F32)

    def chunk_quad(jq, carry):
        for u in range(4):
            full_chunk(4 * jq + u)
        return carry

    n_quads = lax.shift_right_logical(i, 2)
    lax.fori_loop(0, n_quads, chunk_quad, 0)

    @pl.when((i & 2) == 2)
    def _():
        full_chunk(4 * n_quads)
        full_chunk(4 * n_quads + 1)

    @pl.when((i & 1) == 1)
    def _():
        full_chunk(i - 1)

    sb = DIFF_DIAG_BLOCK
    k0 = pl.multiple_of(i * tk, tk)
    tri = lax.broadcasted_iota(jnp.int32, (sb, sb), 0) <= lax.broadcasted_iota(jnp.int32, (sb, sb), 1)
    for qb in range(tq // sb):
        nk = (qb + 1) * sb
        lanes = slice(qb * sb, (qb + 1) * sb)
        vt = vt_ref[:, pl.ds(k0, nk)]
        for c in range(2):
            st = jnp.dot(k_ref[pl.ds(k0, nk), :], qc[c][:, lanes], preferred_element_type=F32)
            p = jnp.exp2(st)
            p_tri = jnp.where(tri, p[qb * sb:, :], 0.0)
            p = jnp.concatenate([p[:qb * sb, :], p_tri], axis=0) if qb else p_tri
            l_sc[c, :, lanes] += jnp.sum(p.reshape(nk // SUBLANES, SUBLANES, sb), axis=0)
            acc_sc[c, :, lanes] += jnp.dot(vt, p.astype(BF16), preferred_element_type=F32)

    l0 = jnp.sum(l_sc[0], axis=0, keepdims=True)
    l1 = jnp.sum(l_sc[1], axis=0, keepdims=True)
    _diff_finalize(acc_sc[0] / l0, acc_sc[1] / l1, lam_ref, sg_ref, o_ref)


def _diff_kernel(qt_ref, k_ref, vt_ref, lam_ref, sg_ref, o_ref, m_sc, l_sc, acc_sc, *, tq, tk):
    i = pl.program_id(1)
    qc = _component_queries(qt_ref[...])
    m_sc[...] = jnp.full(m_sc.shape, NEG_INF, F32)
    l_sc[...] = jnp.zeros(l_sc.shape, F32)
    acc_sc[...] = jnp.zeros(acc_sc.shape, F32)

    def chunk(j, masked):
        k0 = pl.multiple_of(j * tk, tk)
        vt = vt_ref[:, pl.ds(k0, tk)]
        for c in range(2):
            st = jnp.dot(k_ref[pl.ds(k0, tk), :], qc[c], preferred_element_type=F32)
            if masked:
                st = jnp.where(_causal_keep(i, k0, tq, tk), st, NEG_INF)
            m_old = m_sc[c]
            m_new = jnp.maximum(m_old, jnp.max(st, axis=0, keepdims=True))
            alpha = jnp.exp2(m_old - m_new)
            p = jnp.exp2(st - m_new)
            l_sc[c] = alpha * l_sc[c] + jnp.sum(p, axis=0, keepdims=True)
            acc_sc[c] = alpha * acc_sc[c] + jnp.dot(vt, p.astype(BF16), preferred_element_type=F32)
            m_sc[c] = m_new

    n_full = (i * tq) // tk

    def full_chunk(j, carry):
        chunk(j, False)
        return carry

    lax.fori_loop(0, n_full, full_chunk, 0)
    for jj in range(tq // tk):
        chunk(n_full + jj, True)

    _diff_finalize(acc_sc[0] / l_sc[0], acc_sc[1] / l_sc[1], lam_ref, sg_ref, o_ref)


def _diff_dispatch_kernel(bound_ref, qt_ref, k_ref, vt_ref, lam_ref, sg_ref, o_ref, m_sc, l1_sc, l8_sc, acc_sc, *,
                          tq, tk):
    unshifted = bound_ref[0] <= DIFF_UNSHIFTED_SCORE_BOUND

    @pl.when(unshifted)
    def _():
        _diff_unshifted_kernel(qt_ref, k_ref, vt_ref, lam_ref, sg_ref, o_ref, l8_sc, acc_sc, tq=tq, tk=tk)

    @pl.when(jnp.logical_not(unshifted))
    def _():
        _diff_kernel(qt_ref, k_ref, vt_ref, lam_ref, sg_ref, o_ref, m_sc, l1_sc, acc_sc, tq=tq, tk=tk)


def _diff_attention(qd_t, kd_tok, vd_t, lam_params, subln_col, score_bound):
    s = qd_t.shape[1]
    tq, tk = DIFF_TQ, DIFF_TK
    assert tq == tk and s % tq == 0 and tq % DIFF_DIAG_BLOCK == 0
    grid_spec = pltpu.PrefetchScalarGridSpec(
        num_scalar_prefetch=1,
        grid=(DIFF_HEADS, s // tq),
        in_specs=[
            pl.BlockSpec((DIFF_V_DIM, tq), lambda h, i, bound: (h, i)),
            pl.BlockSpec((None, s, 2 * HEAD_DIM), lambda h, i, bound: (h, 0, 0)),
            pl.BlockSpec((DIFF_V_DIM, s), lambda h, i, bound: (h, 0)),
            pl.BlockSpec(lam_params.shape, lambda h, i, bound: (0, 0)),
            pl.BlockSpec(subln_col.shape, lambda h, i, bound: (0, 0)),
        ],
        out_specs=pl.BlockSpec((tq, DIFF_V_DIM), lambda h, i, bound: (i, h)),
        scratch_shapes=[pltpu.VMEM((2, 1, tq), F32), pltpu.VMEM((2, 1, tq), F32), pltpu.VMEM((2, SUBLANES, tq), F32),
                        pltpu.VMEM((2, DIFF_V_DIM, tq), F32)],
    )
    return pl.pallas_call(
        functools.partial(_diff_dispatch_kernel, tq=tq, tk=tk),
        grid_spec=grid_spec,
        out_shape=jax.ShapeDtypeStruct((s, DIFF_HEADS * DIFF_V_DIM), F32),
        compiler_params=_params("parallel", "arbitrary"),
        name="diff_attention",
    )(score_bound.reshape(1).astype(F32), qd_t, kd_tok, vd_t, lam_params, subln_col)


def _att_out_compute(x_ref, acc_ref, l_ref, xexp_ref, od_ref, g_ref, w_ref, tok_sc, dil):
    n_lane_tiles = SPLIT // LANES
    rows = tok_sc.shape[1] // dil
    for r in range(dil):
        dst = pl.ds(r, rows, stride=dil)
        for c in range(n_lane_tiles):
            col0 = r * SPLIT + c * LANES
            tok_sc[c, dst, :] = acc_ref[0, :, col0:col0 + LANES]
        tok_sc[n_lane_tiles, dst, :] = l_ref[0, :, r * LANES:(r + 1) * LANES]
    acc = jnp.concatenate([tok_sc[c] for c in range(n_lane_tiles)], axis=1)
    r = 1.0 / tok_sc[n_lane_tiles]
    r_hi = r.astype(BF16)
    r_lo = (r - r_hi.astype(F32)).astype(BF16)
    r_exp = (jnp.dot(r_hi, xexp_ref[...], preferred_element_type=F32)
             + jnp.dot(r_lo, xexp_ref[...], preferred_element_type=F32))
    g = g_ref[...]
    y = jnp.concatenate([acc * r_exp, od_ref[...]], axis=-1) * (g * jax.nn.sigmoid(g))
    return x_ref[...] + jnp.dot(y.astype(BF16), w_ref[...], preferred_element_type=F32)


def _gelu(x):
    return 0.5 * x * (1.0 + lax.erf(x * (2.0 ** -0.5)))


def _sgu_compute(x, ng_ref, win_ref, lng_ref, lnb_ref, ws_ref, bs_ref, wout_ref, width):
    tm = x.shape[0]
    ms = jnp.mean(x * x, axis=-1, keepdims=True)
    h = (x * lax.rsqrt(ms + RMS_EPS) * ng_ref[...]).astype(BF16)
    v_raw = jnp.dot(h, win_ref[:, width:2 * width], preferred_element_type=F32)
    u_raw = jnp.dot(h, win_ref[:, 0:width], preferred_element_type=F32)
    v = _gelu(v_raw)
    mu = jnp.mean(v, axis=-1, keepdims=True)
    vc = v - mu
    vn = vc * lax.rsqrt(jnp.mean(vc * vc, axis=-1, keepdims=True) + LN_EPS)
    vn = (vn * lng_ref[...] + lnb_ref[...]).astype(BF16)
    gw = width // SGU_GROUPS
    row = lax.broadcasted_iota(jnp.int32, (SGU_CHUNK, SGU_CHUNK), 0)
    col = lax.broadcasted_iota(jnp.int32, (SGU_CHUNK, SGU_CHUNK), 1)
    causal = col <= row
    sp_groups = []
    for grp in range(SGU_GROUPS):
        ws = jnp.where(causal, ws_ref[grp], 0.0).astype(BF16)
        bias = bs_ref[grp]
        chunks = []
        for c in range(tm // SGU_CHUNK):
            vg = vn[c * SGU_CHUNK:(c + 1) * SGU_CHUNK, grp * gw:(grp + 1) * gw]
            chunks.append(jnp.dot(ws, vg, preferred_element_type=F32) + bias)
        sp_groups.append(jnp.concatenate(chunks, axis=0))
    sp = jnp.concatenate(sp_groups, axis=1)
    g = jnp.dot(h, win_ref[:, 2 * width:3 * width], preferred_element_type=F32)
    u = _gelu(u_raw)
    y = u * sp * (g * jax.nn.sigmoid(g))
    return x + jnp.dot(y.astype(BF16), wout_ref[...], preferred_element_type=F32)


def _att_out_sgu_kernel(x_ref, acc_ref, l_ref, xexp_ref, od_ref, g_ref, watt_ref,
                        ng_ref, win_ref, lng_ref, lnb_ref, ws_ref, bs_ref, wout_ref, o_ref, tok_sc, *, dil, width):
    x1 = _att_out_compute(x_ref, acc_ref, l_ref, xexp_ref, od_ref, g_ref, watt_ref, tok_sc, dil)
    o_ref[...] = _sgu_compute(x1, ng_ref, win_ref, lng_ref, lnb_ref, ws_ref, bs_ref, wout_ref, width)


def _att_out_sgu(x2d, acc_view, l_view, od, g, w_att_out, norm_g, w_in, ln_g, ln_b, w_s, b_s, w_out):
    assert w_att_out.dtype == w_in.dtype == w_out.dtype == BF16
    s, d = x2d.shape
    width = w_out.shape[0]
    tm = ROW_TILE
    dil = DSW_PATTERNS[-1][1]
    rows = tm // dil
    per_blk = BLK // rows
    assert acc_view.shape == (s // (BLK * dil), BLK, dil * SPLIT) and tm % dil == 0 and BLK % rows == 0
    assert l_view.shape == (s // (BLK * dil), BLK, dil * LANES)
    assert tm % SGU_CHUNK == 0 and rows % 8 == 0
    _, xexp = _dsw_constants()
    row = lambda w: pl.BlockSpec((tm, w), lambda i: (i, 0))
    view_spec = lambda w: pl.BlockSpec((1, rows, dil * w), lambda i: (i // per_blk, i % per_blk, 0))
    single = lambda shape: pl.BlockSpec(shape, lambda *_: (0,) * len(shape), pipeline_mode=pl.Buffered(1))
    return pl.pallas_call(
        functools.partial(_att_out_sgu_kernel, dil=dil, width=width),
        grid=(s // tm,),
        in_specs=[row(d), view_spec(SPLIT), view_spec(LANES), _const_spec(xexp.shape), row(SPLIT), row(2 * SPLIT),
                  single(w_att_out.shape),
                  _const_spec((1, d)), single(w_in.shape), _const_spec((1, width)), _const_spec((1, width)),
                  _const_spec(w_s.shape), _const_spec((SGU_GROUPS, SGU_CHUNK, 1)), single(w_out.shape)],
        out_specs=row(d),
        out_shape=jax.ShapeDtypeStruct((s, d), F32),
        scratch_shapes=[pltpu.VMEM((SPLIT // LANES + 1, tm, LANES), F32)],
        compiler_params=_params("parallel"),
        name="att_out_sgu",
    )(x2d, acc_view, l_view, xexp, od, g, w_att_out,
      norm_g.reshape(1, d), w_in, ln_g.reshape(1, width), ln_b.reshape(1, width),
      w_s, b_s.reshape(SGU_GROUPS, SGU_CHUNK, 1), w_out)


def _rope_tables_fm(seq_len):
    pos = jnp.arange(seq_len, dtype=F32)
    inv = 1.0 / (ROPE_THETA ** (jnp.arange(0, 2 * ROT_HALF, 2, dtype=F32) / (2 * ROT_HALF)))
    ang = inv[:, None] * pos[None, :]
    return jnp.cos(ang), jnp.sin(ang)


def kernel(x, att_norm, att_w_in, dsw_q_norm, dsw_k_norm, diff_q_norm, diff_k_norm, diff_lam_q1, diff_lam_k1,
           diff_lam_q2, diff_lam_k2, diff_subln, att_w_out, sgu_norm, sgu_w_in, sgu_ln_g, sgu_ln_b, sgu_w_s,
           sgu_b_s, sgu_w_out):
    b, s, d = x.shape
    assert b == 1 and s % (BLK * DSW_PATTERNS[-1][1]) == 0
    x2d = x.reshape(s, d)
    cos_t, sin_t = _rope_tables_fm(s)

    gains = jnp.stack([dsw_q_norm[0], dsw_k_norm[0], diff_q_norm[0], diff_k_norm[0]])
    gains = jnp.broadcast_to(gains[:, None, :, None], (4, N_GROUPS64, HEAD_DIM, 1))
    qa_views, ka_views, va_views, g, kd_tok, qd_t, vd_t, later_bf16 = _att_in(
        x2d, att_norm[0], att_w_in[0], gains, cos_t, sin_t, [att_w_out[0], sgu_w_in[0], sgu_w_out[0]])
    att_w_out_bf16, sgu_w_in_bf16, sgu_w_out_bf16 = later_bf16
    bound = lambda gq, gk: 1.02 * LOG2_E * HEAD_DIM ** 0.5 * jnp.max(jnp.abs(gq)) * jnp.max(jnp.abs(gk))
    oa_acc, oa_l = _dsw_attention(qa_views, ka_views, va_views, bound(dsw_q_norm[0], dsw_k_norm[0]))
    oa_l, qd_t = lax.optimization_barrier((oa_l, qd_t))
    lam_params = jnp.stack([diff_lam_q1[0], diff_lam_k1[0], diff_lam_q2[0], diff_lam_k2[0]])
    od = _diff_attention(qd_t, kd_tok, vd_t, lam_params, diff_subln[0].reshape(DIFF_V_DIM, 1),
                         bound(diff_q_norm[0], diff_k_norm[0]))
    x2d = _att_out_sgu(x2d, oa_acc, oa_l, od, g, att_w_out_bf16,
                       sgu_norm[0], sgu_w_in_bf16, sgu_ln_g[0], sgu_ln_b[0], sgu_w_s[0], sgu_b_s[0], sgu_w_out_bf16)
    return x2d.reshape(b, s, d)
```
